```python
import math
import jax, jax.numpy as jnp
from jax import lax
import numpy as np

D_MODEL = 1024
BATCH = 16
SEQ = 2048
DEPTH = 2

HEAD_DIM = 64
A_HEADS = 6
A_BRANCHES = ((128, 1), (512, 4), (2048, 16))
B_HEADS = 4
B_KEY_DIM = 32
B_VAL_DIM = 64
B_GATE_RANK = 16
B_GATE_TAU = 16.0
B_CHUNK = 16
C_HEADS = 6
C_BLOCK = 256
C_TOPK = 3
C_QCHUNK = 16
REL_BUCKETS = 32
REL_MAX_DIST = 2048
D_FF = 4 * D_MODEL
EPS = 1e-6
NEG_INF = -1e30

A_WIDTH = A_HEADS * HEAD_DIM
B_QK = B_HEADS * B_KEY_DIM
B_WIDTH = B_HEADS * B_VAL_DIM
C_WIDTH = C_HEADS * HEAD_DIM
MIX_WIDTH = A_WIDTH + B_WIDTH + C_WIDTH
IN_SIZES = (A_WIDTH, A_WIDTH, A_WIDTH, B_QK, B_QK, B_WIDTH, B_WIDTH, B_GATE_RANK, C_WIDTH, C_WIDTH, C_WIDTH)
IN_WIDTH = sum(IN_SIZES)

kernel_name = "hybrid_dilated_gla_moba_block"


def rmsnorm(x, w):
    xf = x.astype(jnp.float32)
    y = xf * lax.rsqrt(jnp.mean(xf * xf, axis=-1, keepdims=True) + EPS)
    return (y * w.astype(jnp.float32)).astype(x.dtype)


def rel_bucket(dist):
    n = jnp.maximum(dist, 0)
    exact = REL_BUCKETS // 2
    logv = jnp.log(jnp.maximum(n, 1).astype(jnp.float32) / exact) / math.log(REL_MAX_DIST / exact)
    large = jnp.minimum(exact + (logv * (REL_BUCKETS - exact)).astype(jnp.int32), REL_BUCKETS - 1)
    return jnp.where(n < exact, n, large)


def dilated_branch(q, k, v, bias_a, window, dilation):
    Bsz, S, H, Dh = q.shape
    L = S // dilation
    span = window // dilation
    blk = min(span, L)
    nb = -(-L // blk)
    Lp = nb * blk

    def sub(t):
        t = t.reshape(Bsz, L, dilation, H, Dh).transpose(0, 2, 1, 3, 4)
        t = jnp.pad(t, ((0, 0), (0, 0), (0, Lp - L), (0, 0), (0, 0)))
        return t.reshape(Bsz, dilation, nb, blk, H, Dh)

    def with_prev(t):
        prev = jnp.pad(t, ((0, 0), (0, 0), (1, 0), (0, 0), (0, 0), (0, 0)))[:, :, :-1]
        return jnp.concatenate([prev, t], axis=3)

    qb = sub(q)
    kc = with_prev(sub(k))
    vc = with_prev(sub(v))
    logits = jnp.einsum('brnqhe,brnkhe->brnhqk', qb, kc, preferred_element_type=jnp.float32) * (Dh ** -0.5)

    steps = jnp.arange(blk)[:, None] + blk - jnp.arange(2 * blk)[None, :]
    band = (steps >= 0) & (steps <= span)
    key_ok = (jnp.arange(nb)[:, None] * blk + jnp.arange(2 * blk)[None, :] - blk) >= 0
    mask = band[None] & key_ok[:, None, :]
    bias = bias_a[:, rel_bucket(steps * dilation)].astype(jnp.float32)
    logits = jnp.where(mask[None, None, :, None], logits + bias[None, None, None], NEG_INF)

    lse = jax.nn.logsumexp(logits, axis=-1)
    p = jnp.exp(logits - lse[..., None])
    out = jnp.einsum('brnhqk,brnkhe->brnqhe', p.astype(v.dtype), vc)
    out = out.reshape(Bsz, dilation, Lp, H, Dh)[:, :, :L].transpose(0, 2, 1, 3, 4).reshape(Bsz, S, H, Dh)
    lse = lse.transpose(0, 1, 2, 4, 3).reshape(Bsz, dilation, Lp, H)[:, :, :L].transpose(0, 2, 1, 3).reshape(Bsz, S, H)
    return out.astype(jnp.float32), lse


def dilated_attention(q, k, v, bias_a):
    outs, lses = [], []
    for window, dilation in A_BRANCHES:
        o, l = dilated_branch(q, k, v, bias_a, window, dilation)
        outs.append(o)
        lses.append(l)
    wts = jax.nn.softmax(jnp.stack(lses), axis=0)
    out = jnp.einsum('gbsh,gbshe->bshe', wts, jnp.stack(outs))
    return out.astype(q.dtype)


def gla(q, k, v, r, a_lr, w_a2, b_a, norm_w):
    f32 = jnp.float32
    Bsz, S, H, dk = q.shape
    dv = v.shape[-1]
    C = B_CHUNK
    N = S // C
    log_a = jax.nn.log_sigmoid((a_lr @ w_a2 + b_a).astype(f32)) / B_GATE_TAU

    def chunks(t):
        return t.astype(f32).reshape(Bsz, N, C, H, -1).transpose(0, 3, 1, 2, 4)

    qc = chunks(q) * (dk ** -0.5)
    kc = chunks(k)
    vc = chunks(v)
    b = jnp.cumsum(chunks(log_a.reshape(Bsz, S, H, dk)), axis=3)

    causal = jnp.tril(jnp.ones((C, C), dtype=bool))
    diff = b[..., :, None, :] - b[..., None, :, :]
    decay = jnp.exp(jnp.where(causal[:, :, None], diff, NEG_INF))
    attn = jnp.einsum('bhnic,bhnjc,bhnijc->bhnij', qc, kc, decay)
    o_intra = jnp.einsum('bhnij,bhnjv->bhniv', attn, vc)

    b_last = b[..., -1, :]
    u = jnp.einsum('bhnjc,bhnjv->bhncv', kc * jnp.exp(b_last[..., None, :] - b), vc)

    def step(state, inp):
        dec, uu = inp
        return dec[..., None] * state + uu, state

    _, s_prev = lax.scan(step, jnp.zeros((Bsz, H, dk, dv), f32),
                         (jnp.moveaxis(jnp.exp(b_last), 2, 0), jnp.moveaxis(u, 2, 0)))
    s_prev = jnp.moveaxis(s_prev, 0, 2)
    o_inter = jnp.einsum('bhnic,bhncv->bhniv', qc * jnp.exp(b), s_prev)

    o = (o_intra + o_inter).transpose(0, 2, 3, 1, 4).reshape(Bsz, S, H, dv)
    o = o * lax.rsqrt(jnp.mean(o * o, axis=-1, keepdims=True) + EPS)
    o = o.reshape(Bsz, S, H * dv) * norm_w.astype(f32) * jax.nn.silu(r.astype(f32))
    return o.astype(q.dtype)


def moba_attention(q, k, v, bias_c):
    f32 = jnp.float32
    Bsz, S, H, Dh = q.shape
    nblk = -(-S // C_BLOCK)
    Sp = nblk * C_BLOCK

    def pad(t):
        return jnp.pad(t, ((0, 0), (0, Sp - S), (0, 0), (0, 0))).transpose(0, 2, 1, 3)

    qp, kp, vp = pad(q), pad(k), pad(v)
    kb = kp.reshape(Bsz, H, nblk, C_BLOCK, Dh)
    vb = vp.reshape(Bsz, H, nblk, C_BLOCK, Dh)
    k_mean = jnp.mean(kb.astype(f32), axis=3)

    qblk = jnp.arange(Sp) // C_BLOCK
    past = jnp.arange(nblk)[None, :] < qblk[:, None]
    gate = jnp.where(past, jnp.einsum('bhsd,bhnd->bhsn', qp.astype(f32), k_mean), NEG_INF)
    topk = min(C_TOPK, nblk)
    _, idx = lax.top_k(gate, topk)
    valid = idx < qblk[None, None, :, None]

    n_q = Sp // C_QCHUNK
    qch = qp.reshape(Bsz, H, n_q, C_QCHUNK, Dh).transpose(2, 0, 1, 3, 4)
    idxch = idx.reshape(Bsz, H, n_q, C_QCHUNK, topk).transpose(2, 0, 1, 3, 4)
    valch = valid.reshape(Bsz, H, n_q, C_QCHUNK, topk).transpose(2, 0, 1, 3, 4)
    b_ix = jnp.arange(Bsz)[:, None, None, None]
    h_ix = jnp.arange(H)[None, :, None, None]
    scale = Dh ** -0.5
    nsel = topk * C_BLOCK

    def one_chunk(args):
        ci, qc_, idx_, val_ = args
        qpos = ci * C_QCHUNK + jnp.arange(C_QCHUNK)
        ksel = kb[b_ix, h_ix, idx_].reshape(Bsz, H, C_QCHUNK, nsel, Dh)
        vsel = vb[b_ix, h_ix, idx_].reshape(Bsz, H, C_QCHUNK, nsel, Dh)
        kpos = (idx_[..., None] * C_BLOCK + jnp.arange(C_BLOCK)).reshape(Bsz, H, C_QCHUNK, nsel)
        l_sel = jnp.einsum('bhqd,bhqkd->bhqk', qc_, ksel, preferred_element_type=f32) * scale
        l_sel = l_sel + bias_c[h_ix, rel_bucket(qpos[None, None, :, None] - kpos)].astype(f32)
        l_sel = jnp.where(jnp.repeat(val_, C_BLOCK, axis=-1), l_sel, NEG_INF)
        own = (ci * C_QCHUNK) // C_BLOCK
        kown = lax.dynamic_slice_in_dim(kp, own * C_BLOCK, C_BLOCK, axis=2)
        vown = lax.dynamic_slice_in_dim(vp, own * C_BLOCK, C_BLOCK, axis=2)
        d_own = qpos[:, None] - (own * C_BLOCK + jnp.arange(C_BLOCK))[None, :]
        l_own = jnp.einsum('bhqd,bhkd->bhqk', qc_, kown, preferred_element_type=f32) * scale
        l_own = jnp.where(d_own[None, None] >= 0, l_own + bias_c[:, rel_bucket(d_own)].astype(f32)[None], NEG_INF)
        p = jax.nn.softmax(jnp.concatenate([l_sel, l_own], axis=-1), axis=-1)
        out = (jnp.einsum('bhqk,bhqkd->bhqd', p[..., :nsel].astype(vsel.dtype), vsel)
               + jnp.einsum('bhqk,bhkd->bhqd', p[..., nsel:].astype(vown.dtype), vown))
        return out

    outs = lax.map(one_chunk, (jnp.arange(n_q), qch, idxch, valch))
    return outs.transpose(1, 0, 3, 2, 4).reshape(Bsz, Sp, H, Dh)[:, :S]


def hybrid_layer(x, norm1_w, w_in, w_a2, b_a, gla_norm_w, w_out, norm2_w, w_ff1, w_ff2, rel_bias):
    Bsz, S, _ = x.shape
    h = rmsnorm(x, norm1_w)
    proj = h @ w_in
    aq, ak, av, bq, bk, bv, br, ba, cq, ck, cv = jnp.split(proj, list(np.cumsum(IN_SIZES)[:-1]), axis=-1)

    def heads(t, n):
        return t.reshape(Bsz, S, n, -1)

    bias_a = rel_bias[:, :A_HEADS].T
    bias_c = rel_bias[:, A_HEADS:].T
    y_a = dilated_attention(heads(aq, A_HEADS), heads(ak, A_HEADS), heads(av, A_HEADS), bias_a)
    y_b = gla(heads(bq, B_HEADS), heads(bk, B_HEADS), heads(bv, B_HEADS), br, ba, w_a2, b_a, gla_norm_w)
    y_c = moba_attention(heads(cq, C_HEADS), heads(ck, C_HEADS), heads(cv, C_HEADS), bias_c)
    mix = jnp.concatenate([y_a.reshape(Bsz, S, A_WIDTH), y_b, y_c.reshape(Bsz, S, C_WIDTH)], axis=-1)
    x = x + mix @ w_out

    h2 = rmsnorm(x, norm2_w)
    x = x + jnp.square(jax.nn.relu(h2 @ w_ff1)) @ w_ff2
    return x


def setup_inputs(seed: int = 0) -> dict:
    key = jax.random.key(seed)
    ks = jax.random.split(key, 14)
    f32 = jnp.float32
    nrm = lambda k, shape, s: (jax.random.normal(k, shape, f32) * s)
    return {
        "x": nrm(ks[0], (BATCH, SEQ, D_MODEL), 1.0),
        "norm1_w": 1.0 + nrm(ks[1], (DEPTH, D_MODEL), 0.02),
        "w_in": nrm(ks[2], (DEPTH, D_MODEL, IN_WIDTH), D_MODEL ** -0.5),
        "gla_w_a2": nrm(ks[3], (DEPTH, B_GATE_RANK, B_QK), B_GATE_RANK ** -0.5),
        "gla_b_a": nrm(ks[4], (DEPTH, B_QK), 0.1),
        "gla_norm_w": 1.0 + nrm(ks[5], (DEPTH, B_WIDTH), 0.02),
        "w_out": nrm(ks[6], (DEPTH, MIX_WIDTH, D_MODEL), MIX_WIDTH ** -0.5),
        "norm2_w": 1.0 + nrm(ks[7], (DEPTH, D_MODEL), 0.02),
        "w_ff1": nrm(ks[8], (DEPTH, D_MODEL, D_FF), D_MODEL ** -0.5),
        "w_ff2": nrm(ks[9], (DEPTH, D_FF, D_MODEL), D_FF ** -0.5),
        "rel_bias": nrm(ks[10], (REL_BUCKETS, A_HEADS + C_HEADS), 0.2),
        "final_norm_w": 1.0 + nrm(ks[11], (D_MODEL,), 0.02),
    }


def reference(x, norm1_w, w_in, gla_w_a2, gla_b_a, gla_norm_w, w_out, norm2_w, w_ff1, w_ff2, rel_bias, final_norm_w):
    for i in range(DEPTH):
        x = hybrid_layer(x, norm1_w[i], w_in[i], gla_w_a2[i], gla_b_a[i], gla_norm_w[i], w_out[i],
                         norm2_w[i], w_ff1[i], w_ff2[i], rel_bias)
    return rmsnorm(x, final_norm_w)
```

```python
import functools
import math

import numpy as np
import jax
import jax.numpy as jnp
from jax import lax
from jax.experimental import pallas as pl
from jax.experimental.pallas import tpu as pltpu

D_MODEL = 1024
HEAD_DIM = 64
A_HEADS = 6
A_BRANCHES = ((128, 1), (512, 4), (2048, 16))
B_HEADS = 4
B_KEY_DIM = 32
B_VAL_DIM = 64
B_GATE_RANK = 16
B_GATE_TAU = 16.0
B_CHUNK = 16
C_HEADS = 6
C_BLOCK = 256
C_TOPK = 3
REL_BUCKETS = 32
REL_MAX_DIST = 2048
D_FF = 4 * D_MODEL
EPS = 1e-6
NEG_INF = -1e30

A_WIDTH = A_HEADS * HEAD_DIM
B_QK = B_HEADS * B_KEY_DIM
B_WIDTH = B_HEADS * B_VAL_DIM
C_WIDTH = C_HEADS * HEAD_DIM

LANES = 128
A_BLK = 128
PAIR = LANES // HEAD_DIM
VMEM_LIMIT = 56 * 1024 * 1024

F32 = jnp.float32
BF16 = jnp.bfloat16
HIGHEST = lax.Precision.HIGHEST


def _bucket_thresholds():
    n = np.arange(0, REL_MAX_DIST + 1)
    exact = REL_BUCKETS // 2
    logv = (np.log(np.maximum(n, 1).astype(np.float32) / np.float32(exact))
            / np.float32(math.log(REL_MAX_DIST / exact))).astype(np.float32)
    large = np.minimum(exact + (logv * np.float32(REL_BUCKETS - exact)).astype(np.int32), REL_BUCKETS - 1)
    bucket = np.where(n < exact, n, large)
    assert np.all(np.diff(bucket) >= 0)
    return [int(np.argmax(bucket >= k)) for k in range(REL_BUCKETS)]


_THRESH = _bucket_thresholds()


def _nt_dot(a, b):
    return lax.dot_general(a, b, (((1,), (1,)), ((), ())), preferred_element_type=F32)


def _rms(x, w):
    return x * lax.rsqrt(jnp.mean(x * x, axis=-1, keepdims=True) + EPS) * w


def _bias_lookup(dist, rb_ref, col):
    val = jnp.full(dist.shape, rb_ref[0, col], F32)
    for k in range(1, REL_BUCKETS):
        val = jnp.where(dist >= _THRESH[k], rb_ref[k, col], val)
    return val


def _bias_kernel(rb_ref, ba_ref, bc_ref):
    h = pl.program_id(0)
    qi = lax.broadcasted_iota(jnp.int32, (A_BLK, 2 * A_BLK), 0)
    kj = lax.broadcasted_iota(jnp.int32, (A_BLK, 2 * A_BLK), 1)
    steps = qi + A_BLK - kj
    band = (steps >= 0) & (steps <= A_BLK)
    for g, (_, dil) in enumerate(A_BRANCHES):
        ba_ref[0, g] = jnp.where(band, _bias_lookup(steps * dil, rb_ref, h), NEG_INF)
    kj = lax.broadcasted_iota(jnp.int32, (C_BLOCK, C_BLOCK), 0)
    qi = lax.broadcasted_iota(jnp.int32, (C_BLOCK, C_BLOCK), 1)
    for delta in range(bc_ref.shape[1]):
        dist = delta * C_BLOCK + qi - kj
        bias = _bias_lookup(dist, rb_ref, A_HEADS + h)
        if delta == 0:
            bias = jnp.where(dist >= 0, bias, NEG_INF)
        bc_ref[0, delta] = bias


def _bias_tiles(rel_bias, nblk):
    return pl.pallas_call(
        _bias_kernel,
        grid=(A_HEADS,),
        in_specs=[pl.BlockSpec(memory_space=pltpu.SMEM)],
        out_specs=[
            pl.BlockSpec((1, len(A_BRANCHES), A_BLK, 2 * A_BLK), lambda h: (h, 0, 0, 0)),
            pl.BlockSpec((1, nblk, C_BLOCK, C_BLOCK), lambda h: (h, 0, 0, 0)),
        ],
        out_shape=[
            jax.ShapeDtypeStruct((A_HEADS, len(A_BRANCHES), A_BLK, 2 * A_BLK), F32),
            jax.ShapeDtypeStruct((C_HEADS, nblk, C_BLOCK, C_BLOCK), F32),
        ],
        name="bias_tiles",
    )(rel_bias)


def _inproj_kernel(x_ref, nw_ref, wa_ref, wc_ref, wb_ref, pa_ref, pc_ref, pb_ref):
    h = _rms(x_ref[...], nw_ref[...]).astype(BF16)
    pa_ref[...] = jnp.dot(h, wa_ref[...], preferred_element_type=F32)
    pc_ref[...] = jnp.dot(h, wc_ref[...], preferred_element_type=F32)
    pb_ref[...] = jnp.dot(h, wb_ref[...], preferred_element_type=F32)


def _inproj(x2d, nw, wa, wc, wb, tm):
    m = x2d.shape[0]
    const = lambda i: (0, 0)
    row = lambda i: (i, 0)
    return pl.pallas_call(
        _inproj_kernel,
        grid=(m // tm,),
        in_specs=[
            pl.BlockSpec((tm, D_MODEL), row),
            pl.BlockSpec((1, D_MODEL), const),
            pl.BlockSpec(wa.shape, const),
            pl.BlockSpec(wc.shape, const),
            pl.BlockSpec(wb.shape, const),
        ],
        out_specs=[
            pl.BlockSpec((tm, wa.shape[1]), row),
            pl.BlockSpec((tm, wc.shape[1]), row),
            pl.BlockSpec((tm, wb.shape[1]), row),
        ],
        out_shape=[
            jax.ShapeDtypeStruct((m, wa.shape[1]), F32),
            jax.ShapeDtypeStruct((m, wc.shape[1]), F32),
            jax.ShapeDtypeStruct((m, wb.shape[1]), F32),
        ],
        compiler_params=pltpu.CompilerParams(vmem_limit_bytes=VMEM_LIMIT),
        name="inproj",
    )(x2d, nw, wa, wc, wb)


def _dilated_kernel(q_ref, k_ref, v_ref, bias_ref, o_ref, m_s, l_s, acc_s):
    seq = q_ref.shape[1]
    lane = lax.broadcasted_iota(jnp.int32, (A_BLK, LANES), 1)
    head0 = lane < HEAD_DIM

    def rows(start, size, dil):
        if dil == 1:
            return pl.ds(start, size)
        return pl.ds(start, size, stride=dil)

    def block(g, dil, q_start, k_start, nkeys, first):
        qrows = rows(q_start, A_BLK, dil)
        q = (q_ref[0, qrows, :] * (HEAD_DIM ** -0.5)).astype(BF16)
        krows = rows(k_start, nkeys, dil)
        k = k_ref[0, krows, :].astype(BF16)
        v = v_ref[0, krows, :].astype(BF16)
        zero = jnp.zeros_like(q)
        m_blk, l_blk, o_blk = [], [], []
        for hh in range(PAIR):
            qh = jnp.where(head0 if hh == 0 else ~head0, q, zero)
            s = _nt_dot(qh, k)
            if nkeys == A_BLK:
                s = s + bias_ref[hh, g, :, A_BLK:]
            else:
                s = s + bias_ref[hh, g]
            m = jnp.max(s, axis=-1, keepdims=True)
            p = jnp.exp(s - m)
            l_blk.append(jnp.sum(p, axis=-1, keepdims=True))
            m_blk.append(m)
            o_blk.append(jnp.dot(p.astype(BF16), v, preferred_element_type=F32))
        m_b = jnp.where(head0, m_blk[0], m_blk[1])
        l_b = jnp.where(head0, l_blk[0], l_blk[1])
        o_b = jnp.where(head0, o_blk[0], o_blk[1])
        if first:
            m_s[qrows, :] = m_b
            l_s[qrows, :] = l_b
            acc_s[qrows, :] = o_b
        else:
            m_old = m_s[qrows, :]
            m_new = jnp.maximum(m_old, m_b)
            a_old = jnp.exp(m_old - m_new)
            a_blk = jnp.exp(m_b - m_new)
            m_s[qrows, :] = m_new
            l_s[qrows, :] = a_old * l_s[qrows, :] + a_blk * l_b
            acc_s[qrows, :] = a_old * acc_s[qrows, :] + a_blk * o_b

    for g, (window, dil) in enumerate(A_BRANCHES):
        assert window // dil == A_BLK
        nb = seq // dil // A_BLK
        first = g == 0

        def subseq(r, carry, g=g, dil=dil, nb=nb, first=first):
            block(g, dil, r, r, A_BLK, first)

            def later(n, c):
                start = r + pl.multiple_of(n * (A_BLK * dil), A_BLK * dil)
                block(g, dil, start, start - A_BLK * dil, 2 * A_BLK, first)
                return c

            return lax.fori_loop(1, nb, later, carry)

        if dil == 1:
            subseq(0, 0)
        else:
            lax.fori_loop(0, dil, subseq, 0)

    o_ref[0] = acc_s[...] / l_s[...]


def _dilated(pa, bias_a, batch, seq):
    npair = A_HEADS // PAIR
    blk = lambda off: pl.BlockSpec((1, seq, LANES), lambda b, p: (b, 0, off + p))
    return pl.pallas_call(
        _dilated_kernel,
        grid=(batch, npair),
        in_specs=[
            blk(0), blk(npair), blk(2 * npair),
            pl.BlockSpec((PAIR, len(A_BRANCHES), A_BLK, 2 * A_BLK), lambda b, p: (p, 0, 0, 0)),
        ],
        out_specs=pl.BlockSpec((1, seq, LANES), lambda b, p: (b, 0, p)),
        out_shape=jax.ShapeDtypeStruct((batch, seq, A_WIDTH), F32),
        scratch_shapes=[pltpu.VMEM((seq, LANES), F32)] * 3,
        compiler_params=pltpu.CompilerParams(vmem_limit_bytes=VMEM_LIMIT),
        name="dilated_attn",
    )(pa, pa, pa, bias_a)


GLA_TILE = 128
PB_Q, PB_K, PB_V, PB_R, PB_A = 0, B_QK, 2 * B_QK, 2 * B_QK + B_WIDTH, 2 * B_QK + 2 * B_WIDTH
PB_WIDTH = PB_A + LANES


def _log_sigmoid(z):
    return jnp.minimum(z, 0.0) - jnp.log1p(jnp.exp(-jnp.abs(z)))


def _gla_kernel(pb_ref, w2_ref, ba_ref, nw_ref, y_ref, st_s):
    seq = pb_ref.shape[1]
    c = B_CHUNK
    ri = lax.broadcasted_iota(jnp.int32, (GLA_TILE, GLA_TILE), 0)
    ci = lax.broadcasted_iota(jnp.int32, (GLA_TILE, GLA_TILE), 1)
    same = (ri // c) == (ci // c)
    tri = jnp.where(same & (ci <= ri), 1.0, 0.0).astype(F32)
    tot = jnp.where(same, 1.0, 0.0).astype(F32)
    kl = lax.broadcasted_iota(jnp.int32, (B_QK, B_WIDTH), 0) // B_KEY_DIM
    vl = lax.broadcasted_iota(jnp.int32, (B_QK, B_WIDTH), 1) // B_VAL_DIM
    expand = jnp.where(kl == vl, 1.0, 0.0).astype(BF16)
    vlt = lax.broadcasted_iota(jnp.int32, (B_WIDTH, B_QK), 0) // B_VAL_DIM
    klt = lax.broadcasted_iota(jnp.int32, (B_WIDTH, B_QK), 1) // B_KEY_DIM
    head_t = vlt == klt
    va = lax.broadcasted_iota(jnp.int32, (B_WIDTH, B_WIDTH), 0) // B_VAL_DIM
    vb = lax.broadcasted_iota(jnp.int32, (B_WIDTH, B_WIDTH), 1) // B_VAL_DIM
    head_mean = jnp.where(va == vb, 1.0 / B_VAL_DIM, 0.0).astype(F32)
    ii = lax.broadcasted_iota(jnp.int32, (c, c, B_QK), 0)
    jj = lax.broadcasted_iota(jnp.int32, (c, c, B_QK), 1)
    causal = jj <= ii
    chunk_of_lane = lax.broadcasted_iota(jnp.int32, (B_WIDTH, GLA_TILE), 1) // c

    st_s[...] = jnp.zeros_like(st_s)

    def tile(t, carry):
        rows = pl.ds(pl.multiple_of(t * GLA_TILE, GLA_TILE), GLA_TILE)
        z = jnp.dot(pb_ref[0, rows, PB_A:PB_A + LANES], w2_ref[...], precision=HIGHEST,
                    preferred_element_type=F32) + ba_ref[...]
        log_a = _log_sigmoid(z) * (1.0 / B_GATE_TAU)
        b = jnp.dot(tri, log_a, precision=HIGHEST, preferred_element_type=F32)
        b_last = jnp.dot(tot, log_a, precision=HIGHEST, preferred_element_type=F32)
        q = pb_ref[0, rows, PB_Q:PB_Q + B_QK] * (B_KEY_DIM ** -0.5)
        k = pb_ref[0, rows, PB_K:PB_K + B_QK]
        v = pb_ref[0, rows, PB_V:PB_V + B_WIDTH]
        q_dec = (q * jnp.exp(b)).astype(BF16)
        k_dec = (k * jnp.exp(b_last - b)).astype(BF16)
        gate = jnp.exp(b_last)
        v_t = v.T.astype(BF16)
        state = st_s[...]
        outs = []
        for m in range(GLA_TILE // c):
            sl = slice(m * c, (m + 1) * c)
            qn, kn, bn, vn = q[sl], k[sl], b[sl], v[sl]
            diff = bn[:, None, :] - bn[None, :, :]
            decay = jnp.exp(jnp.where(causal, diff, NEG_INF))
            w = (qn[:, None, :] * kn[None, :, :]) * decay
            attn = jnp.dot(w.reshape(c * c, B_QK).astype(BF16), expand,
                           preferred_element_type=F32)
            o_intra = jnp.sum(attn.reshape(c, c, B_WIDTH) * vn[None, :, :], axis=1)
            o_inter = _nt_dot(q_dec[sl], state.astype(BF16))
            outs.append(o_intra + o_inter)
            v_m = jnp.where(chunk_of_lane == m, v_t, jnp.zeros_like(v_t))
            u_t = jnp.dot(v_m, k_dec, preferred_element_type=F32)
            state = state * gate[m * c:m * c + 1, :] + jnp.where(head_t, u_t, 0.0)
        st_s[...] = state
        o = jnp.concatenate(outs, axis=0)
        ms = jnp.dot(o * o, head_mean, precision=HIGHEST, preferred_element_type=F32)
        r = pb_ref[0, rows, PB_R:PB_R + B_WIDTH]
        silu = r / (1.0 + jnp.exp(-r))
        y_ref[0, rows, :] = o * lax.rsqrt(ms + EPS) * nw_ref[...] * silu
        return carry

    lax.fori_loop(0, seq // GLA_TILE, tile, 0)


def _gla(pb, w2p, b_a, norm_w, batch, seq):
    const = lambda b: (0, 0)
    return pl.pallas_call(
        _gla_kernel,
        grid=(batch,),
        in_specs=[
            pl.BlockSpec((1, seq, PB_WIDTH), lambda b: (b, 0, 0)),
            pl.BlockSpec((LANES, B_QK), const),
            pl.BlockSpec((1, B_QK), const),
            pl.BlockSpec((1, B_WIDTH), const),
        ],
        out_specs=pl.BlockSpec((1, seq, B_WIDTH), lambda b: (b, 0, 0)),
        out_shape=jax.ShapeDtypeStruct((batch, seq, B_WIDTH), F32),
        scratch_shapes=[pltpu.VMEM((B_WIDTH, B_QK), F32)],
        compiler_params=pltpu.CompilerParams(vmem_limit_bytes=VMEM_LIMIT),
        name="gla",
    )(pb, w2p, b_a, norm_w)


def _moba_kernel(q_ref, k_ref, v_ref, bias_ref, o_ref, vt_s, sel_s):
    seq = q_ref.shape[1]
    nblk = seq // C_BLOCK
    qb = pl.program_id(2)
    sub = lax.broadcasted_iota(jnp.int32, (LANES, C_BLOCK), 0)
    head0_rows = sub < HEAD_DIM
    lane = lax.broadcasted_iota(jnp.int32, (C_BLOCK, LANES), 1)
    head0 = lane < HEAD_DIM

    @pl.when(qb == 0)
    def _prepare():
        k = k_ref[0]
        k_mean = jnp.mean(k.reshape(nblk, C_BLOCK, LANES), axis=1)
        q = q_ref[0]
        blk = lax.broadcasted_iota(jnp.int32, (nblk, seq), 0)
        q_blk = lax.broadcasted_iota(jnp.int32, (nblk, seq), 1) // C_BLOCK
        lane8 = lax.broadcasted_iota(jnp.int32, (nblk, LANES), 1)
        for hh in range(PAIR):
            km = jnp.where((lane8 < HEAD_DIM) == (hh == 0), k_mean, 0.0)
            gate = lax.dot_general(km, q, (((1,), (1,)), ((), ())), precision=HIGHEST,
                                   preferred_element_type=F32)
            rank = jnp.zeros((nblk, seq), jnp.int32)
            for m in range(nblk):
                gm = gate[m:m + 1, :]
                beats = (gm > gate) | ((gm == gate) & (m < blk))
                rank = rank + jnp.where(beats & (m < q_blk), 1, 0)
            sel = jnp.where((blk < q_blk) & (rank < C_TOPK), 1.0, 0.0)
            for j in range(nblk):
                sel_s[hh, j] = sel[:, j * C_BLOCK:(j + 1) * C_BLOCK]
        for j in range(nblk):
            vt_s[j] = v_ref[0, j * C_BLOCK:(j + 1) * C_BLOCK, :].T.astype(BF16)

    qstart = pl.multiple_of(qb * C_BLOCK, C_BLOCK)
    q = (q_ref[0, pl.ds(qstart, C_BLOCK), :] * (HEAD_DIM ** -0.5)).astype(BF16)
    zero = jnp.zeros_like(q)
    qh = [jnp.where(head0 if hh == 0 else ~head0, q, zero) for hh in range(PAIR)]

    def tile(n, hh, select):
        kstart = pl.multiple_of(n * C_BLOCK, C_BLOCK)
        k = k_ref[0, pl.ds(kstart, C_BLOCK), :].astype(BF16)
        s = _nt_dot(k, qh[hh]) + bias_ref[hh, qb - n]
        if select:
            s = jnp.where(sel_s[hh, qb, pl.ds(n, 1), :] > 0.0, s, NEG_INF)
        return s

    m_run, l_run, acc = [], [], []
    for hh in range(PAIR):
        s = tile(qb, hh, False)
        m = jnp.max(s, axis=0, keepdims=True)
        p = jnp.exp(s - m)
        m_run.append(m)
        l_run.append(jnp.sum(p, axis=0, keepdims=True))
        acc.append(jnp.dot(vt_s[qb], p.astype(BF16), preferred_element_type=F32))

    def past(n, carry):
        new = []
        for hh in range(PAIR):
            m_old, l_old, a_old = carry[hh]
            s = tile(n, hh, True)
            m = jnp.maximum(m_old, jnp.max(s, axis=0, keepdims=True))
            alpha = jnp.exp(m_old - m)
            p = jnp.exp(s - m)
            l_new = alpha * l_old + jnp.sum(p, axis=0, keepdims=True)
            a_new = alpha * a_old + jnp.dot(vt_s[n], p.astype(BF16), preferred_element_type=F32)
            new.append((m, l_new, a_new))
        return tuple(new)

    fin = lax.fori_loop(0, qb, past, tuple(zip(m_run, l_run, acc)))
    out_t = jnp.where(head0_rows, fin[0][2] / fin[0][1], fin[1][2] / fin[1][1])
    o_ref[0, pl.ds(qstart, C_BLOCK), :] = out_t.T


def _moba(pc, bias_c, batch, seq):
    npair = C_HEADS // PAIR
    nblk = seq // C_BLOCK
    blk = lambda off: pl.BlockSpec((1, seq, LANES), lambda b, p, j: (b, 0, off + p))
    return pl.pallas_call(
        _moba_kernel,
        grid=(batch, npair, nblk),
        in_specs=[
            blk(0), blk(npair), blk(2 * npair),
            pl.BlockSpec((PAIR, nblk, C_BLOCK, C_BLOCK), lambda b, p, j: (p, 0, 0, 0)),
        ],
        out_specs=pl.BlockSpec((1, seq, LANES), lambda b, p, j: (b, 0, p)),
        out_shape=jax.ShapeDtypeStruct((batch, seq, C_WIDTH), F32),
        scratch_shapes=[
            pltpu.VMEM((nblk, LANES, C_BLOCK), BF16),
            pltpu.VMEM((PAIR, nblk, nblk, C_BLOCK), F32),
        ],
        compiler_params=pltpu.CompilerParams(vmem_limit_bytes=VMEM_LIMIT),
        name="moba_attn",
    )(pc, pc, pc, bias_c)


FF_CHUNK = 1024


def _outffn_kernel(x_ref, ya_ref, yb_ref, yc_ref, woa_ref, wob_ref, woc_ref, n2_ref,
                   w1_ref, w2_ref, fn_ref, o_ref, *, final):
    x = x_ref[...]
    x = x + jnp.dot(ya_ref[...].astype(BF16), woa_ref[...], preferred_element_type=F32)
    x = x + jnp.dot(yb_ref[...].astype(BF16), wob_ref[...], preferred_element_type=F32)
    x = x + jnp.dot(yc_ref[...].astype(BF16), woc_ref[...], preferred_element_type=F32)
    h = _rms(x, n2_ref[...]).astype(BF16)
    ffn = None
    for j in range(D_FF // FF_CHUNK):
        cols = slice(j * FF_CHUNK, (j + 1) * FF_CHUNK)
        a = jnp.dot(h, w1_ref[:, cols], preferred_element_type=F32)
        a = jnp.square(jnp.maximum(a, 0.0)).astype(BF16)
        part = jnp.dot(a, w2_ref[cols, :], preferred_element_type=F32)
        ffn = part if ffn is None else ffn + part
    x = x + ffn
    if final:
        x = _rms(x, fn_ref[...])
    o_ref[...] = x


def _outffn(x2d, ya, yb, yc, woa, wob, woc, n2, w1, w2, fn, tm, final):
    m = x2d.shape[0]
    const = lambda i: (0, 0)
    row = lambda i: (i, 0)
    full = lambda a: pl.BlockSpec(a.shape, const)
    return pl.pallas_call(
        functools.partial(_outffn_kernel, final=final),
        grid=(m // tm,),
        in_specs=[
            pl.BlockSpec((tm, D_MODEL), row),
            pl.BlockSpec((tm, ya.shape[1]), row),
            pl.BlockSpec((tm, yb.shape[1]), row),
            pl.BlockSpec((tm, yc.shape[1]), row),
            full(woa), full(wob), full(woc), full(n2), full(w1), full(w2), full(fn),
        ],
        out_specs=pl.BlockSpec((tm, D_MODEL), row),
        out_shape=jax.ShapeDtypeStruct((m, D_MODEL), F32),
        compiler_params=pltpu.CompilerParams(vmem_limit_bytes=VMEM_LIMIT),
        name="outproj_ffn",
    )(x2d, ya, yb, yc, woa, wob, woc, n2, w1, w2, fn)


def _split_w_in(w):
    sizes = (A_WIDTH, A_WIDTH, A_WIDTH, B_QK, B_QK, B_WIDTH, B_WIDTH, B_GATE_RANK, C_WIDTH, C_WIDTH, C_WIDTH)
    offs = np.concatenate([[0], np.cumsum(sizes)])
    aq, ak, av, bq, bk, bv, br, ba, cq, ck, cv = [w[:, offs[i]:offs[i + 1]] for i in range(len(sizes))]
    ba = jnp.pad(ba, ((0, 0), (0, LANES - B_GATE_RANK)))
    cat = lambda parts: jnp.concatenate(parts, axis=1).astype(BF16)
    return cat([aq, ak, av]), cat([cq, ck, cv]), cat([bq, bk, bv, br, ba])


def kernel(x, norm1_w, w_in, gla_w_a2, gla_b_a, gla_norm_w, w_out, norm2_w, w_ff1, w_ff2, rel_bias, final_norm_w):
    batch, seq, d = x.shape
    depth = w_in.shape[0]
    assert d == D_MODEL and seq % C_BLOCK == 0 and seq % (A_BLK * A_BRANCHES[-1][1]) == 0
    tokens = batch * seq
    tm = 512 if tokens % 512 == 0 else seq
    bias_a, bias_c = _bias_tiles(rel_bias, seq // C_BLOCK)
    x2d = x.reshape(tokens, d)
    fn = final_norm_w.reshape(1, d)
    for i in range(depth):
        wa, wc, wb = _split_w_in(w_in[i])
        pa, pc, pb = _inproj(x2d, norm1_w[i].reshape(1, d), wa, wc, wb, tm)
        ya = _dilated(pa.reshape(batch, seq, -1), bias_a, batch, seq)
        w2p = jnp.pad(gla_w_a2[i], ((0, LANES - B_GATE_RANK), (0, 0)))
        yb = _gla(pb.reshape(batch, seq, -1), w2p, gla_b_a[i].reshape(1, B_QK),
                  gla_norm_w[i].reshape(1, B_WIDTH), batch, seq)
        yc = _moba(pc.reshape(batch, seq, -1), bias_c, batch, seq)
        wo = w_out[i].astype(BF16)
        x2d = _outffn(
            x2d, ya.reshape(tokens, -1), yb.reshape(tokens, -1), yc.reshape(tokens, -1),
            wo[:A_WIDTH], wo[A_WIDTH:A_WIDTH + B_WIDTH], wo[A_WIDTH + B_WIDTH:],
            norm2_w[i].reshape(1, d), w_ff1[i].astype(BF16), w_ff2[i].astype(BF16), fn,
            tm, final=(i == depth - 1))
    return x2d.reshape(batch, seq, d)
```

```python
import functools
import math

import numpy as np
import jax
import jax.numpy as jnp
from jax import lax
from jax.experimental import pallas as pl
from jax.experimental.pallas import tpu as pltpu

D_MODEL = 1024
HEAD_DIM = 64
A_HEADS = 6
A_BRANCHES = ((128, 1), (512, 4), (2048, 16))
B_HEADS = 4
B_KEY_DIM = 32
B_VAL_DIM = 64
B_GATE_RANK = 16
B_GATE_TAU = 16.0
B_CHUNK = 16
C_HEADS = 6
C_BLOCK = 256
C_TOPK = 3
REL_BUCKETS = 32
REL_MAX_DIST = 2048
D_FF = 4 * D_MODEL
EPS = 1e-6
NEG_INF = -1e30

A_WIDTH = A_HEADS * HEAD_DIM
B_QK = B_HEADS * B_KEY_DIM
B_WIDTH = B_HEADS * B_VAL_DIM
C_WIDTH = C_HEADS * HEAD_DIM

LANES = 128
SUBLANES = 8
A_BLK = 128
PAIR = LANES // HEAD_DIM
VMEM_LIMIT = 56 * 1024 * 1024

F32 = jnp.float32
BF16 = jnp.bfloat16
HIGHEST = lax.Precision.HIGHEST


def _bucket_thresholds():
    n = np.arange(0, REL_MAX_DIST + 1)
    exact = REL_BUCKETS // 2
    logv = (np.log(np.maximum(n, 1).astype(np.float32) / np.float32(exact))
            / np.float32(math.log(REL_MAX_DIST / exact))).astype(np.float32)
    large = np.minimum(exact + (logv * np.float32(REL_BUCKETS - exact)).astype(np.int32), REL_BUCKETS - 1)
    bucket = np.where(n < exact, n, large)
    assert np.all(np.diff(bucket) >= 0)
    return [int(np.argmax(bucket >= k)) for k in range(REL_BUCKETS)]


_THRESH = _bucket_thresholds()


def _nt_dot(a, b):
    return lax.dot_general(a, b, (((1,), (1,)), ((), ())), preferred_element_type=F32)


def _rms(x, w):
    return x * lax.rsqrt(jnp.mean(x * x, axis=-1, keepdims=True) + EPS) * w


def _bias_lookup(dist, rb_ref, col):
    val = jnp.full(dist.shape, rb_ref[0, col], F32)
    for k in range(1, REL_BUCKETS):
        val = jnp.where(dist >= _THRESH[k], rb_ref[k, col], val)
    return val


def _bias_kernel(rb_ref, ba_ref, bc_ref):
    h = pl.program_id(0)
    qi = lax.broadcasted_iota(jnp.int32, (A_BLK, 2 * A_BLK), 0)
    kj = lax.broadcasted_iota(jnp.int32, (A_BLK, 2 * A_BLK), 1)
    steps = qi + A_BLK - kj
    band = (steps >= 0) & (steps <= A_BLK)
    for g, (_, dil) in enumerate(A_BRANCHES):
        ba_ref[0, g] = jnp.where(band, _bias_lookup(steps * dil, rb_ref, h), NEG_INF)
    kj = lax.broadcasted_iota(jnp.int32, (C_BLOCK, C_BLOCK), 0)
    qi = lax.broadcasted_iota(jnp.int32, (C_BLOCK, C_BLOCK), 1)
    nblk = bc_ref.shape[1]
    for delta in range(nblk):
        dist = delta * C_BLOCK + qi - kj
        bias = _bias_lookup(dist, rb_ref, A_HEADS + h)
        if delta == 0:
            bias = jnp.where(dist >= 0, bias, NEG_INF)
        bc_ref[0, nblk - 1 - delta] = bias


def _bias_tiles(rel_bias, nblk):
    return pl.pallas_call(
        _bias_kernel,
        grid=(A_HEADS,),
        in_specs=[pl.BlockSpec(memory_space=pltpu.SMEM)],
        out_specs=[
            pl.BlockSpec((1, len(A_BRANCHES), A_BLK, 2 * A_BLK), lambda h: (h // PAIR, 0, h % PAIR, 0)),
            pl.BlockSpec((1, nblk, C_BLOCK, C_BLOCK), lambda h: (h, 0, 0, 0)),
        ],
        out_shape=[
            jax.ShapeDtypeStruct((A_HEADS // PAIR, len(A_BRANCHES), PAIR * A_BLK, 2 * A_BLK), F32),
            jax.ShapeDtypeStruct((C_HEADS, nblk, C_BLOCK, C_BLOCK), F32),
        ],
        name="bias_tiles",
    )(rel_bias)


def _inproj_kernel(x_ref, nw_ref, wa_ref, wc_ref, wb_ref, pa_ref, pc_ref, pb_ref):
    h = _rms(x_ref[...], nw_ref[...]).astype(BF16)
    pa_ref[...] = jnp.dot(h, wa_ref[...], preferred_element_type=F32)
    pc_ref[...] = jnp.dot(h, wc_ref[...], preferred_element_type=F32)
    pb_ref[...] = jnp.dot(h, wb_ref[...], preferred_element_type=F32)


def _inproj(x2d, nw, wa, wc, wb, tm):
    m = x2d.shape[0]
    const = lambda i: (0, 0)
    row = lambda i: (i, 0)
    return pl.pallas_call(
        _inproj_kernel,
        grid=(m // tm,),
        in_specs=[
            pl.BlockSpec((tm, D_MODEL), row),
            pl.BlockSpec((1, D_MODEL), const),
            pl.BlockSpec(wa.shape, const),
            pl.BlockSpec(wc.shape, const),
            pl.BlockSpec(wb.shape, const),
        ],
        out_specs=[
            pl.BlockSpec((tm, wa.shape[1]), row),
            pl.BlockSpec((tm, wc.shape[1]), row),
            pl.BlockSpec((tm, wb.shape[1]), row),
        ],
        out_shape=[
            jax.ShapeDtypeStruct((m, wa.shape[1]), F32),
            jax.ShapeDtypeStruct((m, wc.shape[1]), F32),
            jax.ShapeDtypeStruct((m, wb.shape[1]), F32),
        ],
        compiler_params=pltpu.CompilerParams(vmem_limit_bytes=VMEM_LIMIT),
        name="inproj",
    )(x2d, nw, wa, wc, wb)


A_GROUP = 4


def _dilated_kernel(q_ref, k_ref, v_ref, bias_ref, o_ref, qs_s, ks_s, vs_s, m_s, l_s, acc_s):
    seq = q_ref.shape[1]
    lane = lax.broadcasted_iota(jnp.int32, (A_BLK, LANES), 1)
    head0 = lane < HEAD_DIM
    vs_s[:, LANES:] = jnp.ones((seq, LANES), BF16)

    def block(g, dil, sub_start, tok_start, first):
        qrows = pl.ds(sub_start, A_BLK)
        q2 = jnp.concatenate([qs_s[0, qrows, :], qs_s[1, qrows, :]], axis=0)
        if first:
            krows = qrows
            bias = bias_ref[0, g, :, A_BLK:]
        else:
            krows = pl.ds(sub_start - A_BLK, 2 * A_BLK)
            bias = bias_ref[0, g]
        s = _nt_dot(q2, ks_s[krows, :]) + bias
        m = jnp.max(s, axis=-1, keepdims=True)
        p = jnp.exp(s - m).astype(BF16)
        r = jnp.dot(p, vs_s[krows, :], preferred_element_type=F32)
        trows = pl.ds(tok_start, A_BLK) if dil == 1 else pl.ds(tok_start, A_BLK, stride=dil)
        m_s[g, trows, :] = jnp.where(head0, m[:A_BLK], m[A_BLK:])
        l_s[g, trows, :] = jnp.where(head0, r[:A_BLK, LANES:], r[A_BLK:, LANES:])
        acc_s[g, trows, :] = jnp.where(head0, r[:A_BLK, :LANES], r[A_BLK:, :LANES])

    for g, (window, dil) in enumerate(A_BRANCHES):
        assert window // dil == A_BLK
        sub_len = seq // dil
        nb = sub_len // A_BLK
        assert nb in (1, A_GROUP) or (dil == 1 and nb % A_GROUP == 0)

        def gather(r, carry, dil=dil, sub_len=sub_len):
            src = pl.ds(r, sub_len) if dil == 1 else pl.ds(r, sub_len, stride=dil)
            dst = pl.ds(pl.multiple_of(r * sub_len, sub_len), sub_len)
            q = q_ref[0, src, :] * (HEAD_DIM ** -0.5)
            lane_l = lax.broadcasted_iota(jnp.int32, q.shape, 1)
            qs_s[0, dst, :] = jnp.where(lane_l < HEAD_DIM, q, 0.0).astype(BF16)
            qs_s[1, dst, :] = jnp.where(lane_l < HEAD_DIM, 0.0, q).astype(BF16)
            ks_s[dst, :] = k_ref[0, src, :].astype(BF16)
            vs_s[dst, :LANES] = v_ref[0, src, :].astype(BF16)
            return carry

        if dil == 1:
            gather(0, 0)
        else:
            lax.fori_loop(0, dil, gather, 0)

        def group(gi, carry, g=g, dil=dil, nb=nb):
            for j in range(A_GROUP):
                bi = gi * A_GROUP + j
                if nb == 1:
                    r, n, first = bi, 0, True
                elif nb == A_GROUP:
                    r, n, first = gi, j, j == 0
                else:
                    r, n, first = 0, bi, isinstance(bi, int) and bi == 0
                sub_start = bi * A_BLK if isinstance(bi, int) else pl.multiple_of(bi * A_BLK, A_BLK)
                block(g, dil, sub_start, r + n * (A_BLK * dil), first)
            return carry

        ngroups = (dil * nb) // A_GROUP
        if dil == 1:
            group(0, 0)
            lax.fori_loop(1, ngroups, group, 0)
        else:
            lax.fori_loop(0, ngroups, group, 0)

    m1, m2, m3 = m_s[0], m_s[1], m_s[2]
    m = jnp.maximum(jnp.maximum(m1, m2), m3)
    w1, w2, w3 = jnp.exp(m1 - m), jnp.exp(m2 - m), jnp.exp(m3 - m)
    num = w1 * acc_s[0] + w2 * acc_s[1] + w3 * acc_s[2]
    den = w1 * l_s[0] + w2 * l_s[1] + w3 * l_s[2]
    o_ref[0] = num / den


def _dilated(pa, bias_a, batch, seq):
    npair = A_HEADS // PAIR
    nbr = len(A_BRANCHES)
    blk = lambda off: pl.BlockSpec((1, seq, LANES), lambda b, p: (b, 0, off + p))
    return pl.pallas_call(
        _dilated_kernel,
        grid=(batch, npair),
        in_specs=[
            blk(0), blk(npair), blk(2 * npair),
            pl.BlockSpec((1, nbr, PAIR * A_BLK, 2 * A_BLK), lambda b, p: (p, 0, 0, 0)),
        ],
        out_specs=pl.BlockSpec((1, seq, LANES), lambda b, p: (b, 0, p)),
        out_shape=jax.ShapeDtypeStruct((batch, seq, A_WIDTH), F32),
        scratch_shapes=[
            pltpu.VMEM((PAIR, seq, LANES), BF16),
            pltpu.VMEM((seq, LANES), BF16),
            pltpu.VMEM((seq, 2 * LANES), BF16),
            pltpu.VMEM((nbr, seq, LANES), F32),
            pltpu.VMEM((nbr, seq, LANES), F32),
            pltpu.VMEM((nbr, seq, LANES), F32),
        ],
        compiler_params=pltpu.CompilerParams(vmem_limit_bytes=VMEM_LIMIT),
        name="dilated_attn",
    )(pa, pa, pa, bias_a)


GLA_TILE = 128
PB_Q, PB_K, PB_V, PB_R, PB_A = 0, B_QK, 2 * B_QK, 2 * B_QK + B_WIDTH, 2 * B_QK + 2 * B_WIDTH
PB_WIDTH = PB_A + LANES


def _log_sigmoid(z):
    return jnp.minimum(z, 0.0) - jnp.log1p(jnp.exp(-jnp.abs(z)))


def _gla_kernel(pb_ref, w2_ref, ba_ref, nw_ref, y_ref, st_s):
    seq = pb_ref.shape[1]
    c = B_CHUNK
    ri = lax.broadcasted_iota(jnp.int32, (GLA_TILE, GLA_TILE), 0)
    ci = lax.broadcasted_iota(jnp.int32, (GLA_TILE, GLA_TILE), 1)
    same = (ri // c) == (ci // c)
    tri = jnp.where(same & (ci <= ri), 1.0, 0.0).astype(F32)
    tot = jnp.where(same, 1.0, 0.0).astype(F32)
    kl = lax.broadcasted_iota(jnp.int32, (B_QK, B_WIDTH), 0) // B_KEY_DIM
    vl = lax.broadcasted_iota(jnp.int32, (B_QK, B_WIDTH), 1) // B_VAL_DIM
    expand = jnp.where(kl == vl, 1.0, 0.0).astype(BF16)
    vlt = lax.broadcasted_iota(jnp.int32, (B_WIDTH, B_QK), 0) // B_VAL_DIM
    klt = lax.broadcasted_iota(jnp.int32, (B_WIDTH, B_QK), 1) // B_KEY_DIM
    head_t = vlt == klt
    va = lax.broadcasted_iota(jnp.int32, (B_WIDTH, B_WIDTH), 0) // B_VAL_DIM
    vb = lax.broadcasted_iota(jnp.int32, (B_WIDTH, B_WIDTH), 1) // B_VAL_DIM
    head_mean = jnp.where(va == vb, 1.0 / B_VAL_DIM, 0.0).astype(F32)
    ii = lax.broadcasted_iota(jnp.int32, (c, c, B_QK), 0)
    jj = lax.broadcasted_iota(jnp.int32, (c, c, B_QK), 1)
    causal = jj <= ii
    chunk_of_lane = lax.broadcasted_iota(jnp.int32, (B_WIDTH, GLA_TILE), 1) // c

    st_s[...] = jnp.zeros_like(st_s)

    def tile(t, carry):
        rows = pl.ds(pl.multiple_of(t * GLA_TILE, GLA_TILE), GLA_TILE)
        z = jnp.dot(pb_ref[0, rows, PB_A:PB_A + LANES], w2_ref[...], precision=HIGHEST,
                    preferred_element_type=F32) + ba_ref[...]
        log_a = _log_sigmoid(z) * (1.0 / B_GATE_TAU)
        b = jnp.dot(tri, log_a, precision=HIGHEST, preferred_element_type=F32)
        b_last = jnp.dot(tot, log_a, precision=HIGHEST, preferred_element_type=F32)
        q = pb_ref[0, rows, PB_Q:PB_Q + B_QK] * (B_KEY_DIM ** -0.5)
        k = pb_ref[0, rows, PB_K:PB_K + B_QK]
        v = pb_ref[0, rows, PB_V:PB_V + B_WIDTH]
        q_dec = (q * jnp.exp(b)).astype(BF16)
        k_dec = (k * jnp.exp(b_last - b)).astype(BF16)
        gate = jnp.exp(b_last)
        v_t = v.T.astype(BF16)
        state = st_s[...]
        outs = []
        for m in range(GLA_TILE // c):
            sl = slice(m * c, (m + 1) * c)
            qn, kn, bn, vn = q[sl], k[sl], b[sl], v[sl]
            diff = bn[:, None, :] - bn[None, :, :]
            decay = jnp.exp(jnp.where(causal, diff, NEG_INF))
            w = (qn[:, None, :] * kn[None, :, :]) * decay
            attn = jnp.dot(w.reshape(c * c, B_QK).astype(BF16), expand,
                           preferred_element_type=F32)
            o_intra = jnp.sum(attn.reshape(c, c, B_WIDTH) * vn[None, :, :], axis=1)
            o_inter = _nt_dot(q_dec[sl], state.astype(BF16))
            outs.append(o_intra + o_inter)
            v_m = jnp.where(chunk_of_lane == m, v_t, jnp.zeros_like(v_t))
            u_t = jnp.dot(v_m, k_dec, preferred_element_type=F32)
            state = state * gate[m * c:m * c + 1, :] + jnp.where(head_t, u_t, 0.0)
        st_s[...] = state
        o = jnp.concatenate(outs, axis=0)
        ms = jnp.dot(o * o, head_mean, precision=HIGHEST, preferred_element_type=F32)
        r = pb_ref[0, rows, PB_R:PB_R + B_WIDTH]
        silu = r / (1.0 + jnp.exp(-r))
        y_ref[0, rows, :] = o * lax.rsqrt(ms + EPS) * nw_ref[...] * silu
        return carry

    lax.fori_loop(0, seq // GLA_TILE, tile, 0)


def _gla(pb, w2p, b_a, norm_w, batch, seq):
    const = lambda b: (0, 0)
    return pl.pallas_call(
        _gla_kernel,
        grid=(batch,),
        in_specs=[
            pl.BlockSpec((1, seq, PB_WIDTH), lambda b: (b, 0, 0)),
            pl.BlockSpec((LANES, B_QK), const),
            pl.BlockSpec((1, B_QK), const),
            pl.BlockSpec((1, B_WIDTH), const),
        ],
        out_specs=pl.BlockSpec((1, seq, B_WIDTH), lambda b: (b, 0, 0)),
        out_shape=jax.ShapeDtypeStruct((batch, seq, B_WIDTH), F32),
        scratch_shapes=[pltpu.VMEM((B_WIDTH, B_QK), F32)],
        compiler_params=pltpu.CompilerParams(vmem_limit_bytes=VMEM_LIMIT),
        name="gla",
    )(pb, w2p, b_a, norm_w)


C_ONES = 16


def _moba_kernel(q_ref, k_ref, v_ref, bias_ref, o_ref, ka_s, qa_s, vt_s, s_s):
    seq = q_ref.shape[1]
    nblk = seq // C_BLOCK
    head0_rows = lax.broadcasted_iota(jnp.int32, (LANES, C_BLOCK), 0) < HEAD_DIM

    k = k_ref[0]
    q = q_ref[0]
    ka_s[:, :LANES] = k.astype(BF16)
    row_blk = lax.broadcasted_iota(jnp.int32, (seq, LANES), 0) // C_BLOCK
    lane_id = lax.broadcasted_iota(jnp.int32, (seq, LANES), 1)
    ka_s[:, LANES:] = jnp.where(lane_id == row_blk, 1.0, 0.0).astype(BF16)
    k_mean = jnp.mean(k.reshape(nblk, C_BLOCK, LANES), axis=1)
    q_t = (q * (HEAD_DIM ** -0.5)).T
    head0_t = lax.broadcasted_iota(jnp.int32, (LANES, seq), 0) < HEAD_DIM
    blk = lax.broadcasted_iota(jnp.int32, (nblk, seq), 0)
    q_blk = lax.broadcasted_iota(jnp.int32, (nblk, seq), 1) // C_BLOCK
    lane8 = lax.broadcasted_iota(jnp.int32, (nblk, LANES), 1)
    for hh in range(PAIR):
        km = jnp.where((lane8 < HEAD_DIM) == (hh == 0), k_mean, 0.0)
        gate = lax.dot_general(km, q, (((1,), (1,)), ((), ())), precision=HIGHEST,
                               preferred_element_type=F32)
        rank = jnp.zeros((nblk, seq), jnp.int32)
        for m in range(nblk):
            gm = gate[m:m + 1, :]
            beats = (gm > gate) | ((gm == gate) & (m < blk))
            rank = rank + jnp.where(beats & (m < q_blk), 1, 0)
        dropped = (blk < q_blk) & (rank >= C_TOPK)
        pen = jnp.where(dropped, NEG_INF, 0.0)
        assert nblk == SUBLANES
        aug = jnp.concatenate([pen, jnp.zeros((LANES - nblk, seq), F32)], axis=0)
        qa_s[hh, :LANES, :] = jnp.where(head0_t == (hh == 0), q_t, 0.0).astype(BF16)
        qa_s[hh, LANES:, :] = aug.astype(BF16)
    vt_s[:LANES, :] = v_ref[0].T.astype(BF16)
    vt_s[LANES:, :] = jnp.ones((C_ONES, seq), BF16)

    for qb in range(nblk):
        nkeys = (qb + 1) * C_BLOCK
        qcols = slice(qb * C_BLOCK, (qb + 1) * C_BLOCK)
        outs = []
        for hh in range(PAIR):
            s = jnp.dot(ka_s[:nkeys, :], qa_s[hh, :, qcols], preferred_element_type=F32)
            s = s + bias_ref[hh, (nblk - 1 - qb) * C_BLOCK:, :]
            s_s[hh, :nkeys, :] = s
            m = jnp.max(s, axis=0, keepdims=True)
            p = jnp.exp(s_s[hh, :nkeys, :] - m).astype(BF16)
            r = jnp.dot(vt_s[:, :nkeys], p, preferred_element_type=F32)
            outs.append(r[:LANES] / r[LANES:LANES + 1])
        out_t = jnp.where(head0_rows, outs[0], outs[1])
        o_ref[0, qcols, :] = out_t.T


def _moba(pc, bias_c, batch, seq):
    npair = C_HEADS // PAIR
    nblk = seq // C_BLOCK
    blk = lambda off: pl.BlockSpec((1, seq, LANES), lambda b, p: (b, 0, off + p))
    return pl.pallas_call(
        _moba_kernel,
        grid=(batch, npair),
        in_specs=[
            blk(0), blk(npair), blk(2 * npair),
            pl.BlockSpec((PAIR, seq, C_BLOCK), lambda b, p: (p, 0, 0)),
        ],
        out_specs=pl.BlockSpec((1, seq, LANES), lambda b, p: (b, 0, p)),
        out_shape=jax.ShapeDtypeStruct((batch, seq, C_WIDTH), F32),
        scratch_shapes=[
            pltpu.VMEM((seq, 2 * LANES), BF16),
            pltpu.VMEM((PAIR, 2 * LANES, seq), BF16),
            pltpu.VMEM((LANES + C_ONES, seq), BF16),
            pltpu.VMEM((PAIR, seq, C_BLOCK), F32),
        ],
        compiler_params=pltpu.CompilerParams(vmem_limit_bytes=VMEM_LIMIT),
        name="moba_attn",
    )(pc, pc, pc, bias_c)


FF_CHUNK = 1024


def _outffn_kernel(x_ref, ya_ref, yb_ref, yc_ref, woa_ref, wob_ref, woc_ref, n2_ref,
                   w1_ref, w2_ref, fn_ref, o_ref, *, final):
    x = x_ref[...]
    x = x + jnp.dot(ya_ref[...].astype(BF16), woa_ref[...], preferred_element_type=F32)
    x = x + jnp.dot(yb_ref[...].astype(BF16), wob_ref[...], preferred_element_type=F32)
    x = x + jnp.dot(yc_ref[...].astype(BF16), woc_ref[...], preferred_element_type=F32)
    h = _rms(x, n2_ref[...]).astype(BF16)
    ffn = None
    for j in range(D_FF // FF_CHUNK):
        cols = slice(j * FF_CHUNK, (j + 1) * FF_CHUNK)
        a = jnp.dot(h, w1_ref[:, cols], preferred_element_type=F32)
        a = jnp.square(jnp.maximum(a, 0.0)).astype(BF16)
        part = jnp.dot(a, w2_ref[cols, :], preferred_element_type=F32)
        ffn = part if ffn is None else ffn + part
    x = x + ffn
    if final:
        x = _rms(x, fn_ref[...])
    o_ref[...] = x


def _outffn(x2d, ya, yb, yc, woa, wob, woc, n2, w1, w2, fn, tm, final):
    m = x2d.shape[0]
    const = lambda i: (0, 0)
    row = lambda i: (i, 0)
    full = lambda a: pl.BlockSpec(a.shape, const)
    return pl.pallas_call(
        functools.partial(_outffn_kernel, final=final),
        grid=(m // tm,),
        in_specs=[
            pl.BlockSpec((tm, D_MODEL), row),
            pl.BlockSpec((tm, ya.shape[1]), row),
            pl.BlockSpec((tm, yb.shape[1]), row),
            pl.BlockSpec((tm, yc.shape[1]), row),
            full(woa), full(wob), full(woc), full(n2), full(w1), full(w2), full(fn),
        ],
        out_specs=pl.BlockSpec((tm, D_MODEL), row),
        out_shape=jax.ShapeDtypeStruct((m, D_MODEL), F32),
        compiler_params=pltpu.CompilerParams(vmem_limit_bytes=VMEM_LIMIT),
        name="outproj_ffn",
    )(x2d, ya, yb, yc, woa, wob, woc, n2, w1, w2, fn)


def _split_w_in(w):
    sizes = (A_WIDTH, A_WIDTH, A_WIDTH, B_QK, B_QK, B_WIDTH, B_WIDTH, B_GATE_RANK, C_WIDTH, C_WIDTH, C_WIDTH)
    offs = np.concatenate([[0], np.cumsum(sizes)])
    aq, ak, av, bq, bk, bv, br, ba, cq, ck, cv = [w[:, offs[i]:offs[i + 1]] for i in range(len(sizes))]
    ba = jnp.pad(ba, ((0, 0), (0, LANES - B_GATE_RANK)))
    cat = lambda parts: jnp.concatenate(parts, axis=1).astype(BF16)
    return cat([aq, ak, av]), cat([cq, ck, cv]), cat([bq, bk, bv, br, ba])


def kernel(x, norm1_w, w_in, gla_w_a2, gla_b_a, gla_norm_w, w_out, norm2_w, w_ff1, w_ff2, rel_bias, final_norm_w):
    batch, seq, d = x.shape
    depth = w_in.shape[0]
    assert d == D_MODEL and seq % C_BLOCK == 0 and seq % (A_BLK * A_BRANCHES[-1][1]) == 0
    tokens = batch * seq
    tm = 512 if tokens % 512 == 0 else seq
    bias_a, bias_c = _bias_tiles(rel_bias, seq // C_BLOCK)
    bias_c = bias_c.reshape(C_HEADS, seq, C_BLOCK)
    x2d = x.reshape(tokens, d)
    fn = final_norm_w.reshape(1, d)
    for i in range(depth):
        wa, wc, wb = _split_w_in(w_in[i])
        pa, pc, pb = _inproj(x2d, norm1_w[i].reshape(1, d), wa, wc, wb, tm)
        ya = _dilated(pa.reshape(batch, seq, -1), bias_a, batch, seq)
        w2p = jnp.pad(gla_w_a2[i], ((0, LANES - B_GATE_RANK), (0, 0)))
        yb = _gla(pb.reshape(batch, seq, -1), w2p, gla_b_a[i].reshape(1, B_QK),
                  gla_norm_w[i].reshape(1, B_WIDTH), batch, seq)
        yc = _moba(pc.reshape(batch, seq, -1), bias_c, batch, seq)
        wo = w_out[i].astype(BF16)
        x2d = _outffn(
            x2d, ya.reshape(tokens, -1), yb.reshape(tokens, -1), yc.reshape(tokens, -1),
            wo[:A_WIDTH], wo[A_WIDTH:A_WIDTH + B_WIDTH], wo[A_WIDTH + B_WIDTH:],
            norm2_w[i].reshape(1, d), w_ff1[i].astype(BF16), w_ff2[i].astype(BF16), fn,
            tm, final=(i == depth - 1))
    return x2d.reshape(batch, seq, d)
```

```python
import functools
import math

import numpy as np
import jax
import jax.numpy as jnp
from jax import lax
from jax.experimental import pallas as pl
from jax.experimental.pallas import tpu as pltpu

D_MODEL = 1024
HEAD_DIM = 64
A_HEADS = 6
A_BRANCHES = ((128, 1), (512, 4), (2048, 16))
B_HEADS = 4
B_KEY_DIM = 32
B_VAL_DIM = 64
B_GATE_RANK = 16
B_GATE_TAU = 16.0
B_CHUNK = 16
C_HEADS = 6
C_BLOCK = 256
C_TOPK = 3
REL_BUCKETS = 32
REL_MAX_DIST = 2048
D_FF = 4 * D_MODEL
EPS = 1e-6
NEG_INF = -1e30

A_WIDTH = A_HEADS * HEAD_DIM
B_QK = B_HEADS * B_KEY_DIM
B_WIDTH = B_HEADS * B_VAL_DIM
C_WIDTH = C_HEADS * HEAD_DIM

LANES = 128
SUBLANES = 8
A_BLK = 128
PAIR = LANES // HEAD_DIM
VMEM_LIMIT = 56 * 1024 * 1024

F32 = jnp.float32
BF16 = jnp.bfloat16
HIGHEST = lax.Precision.HIGHEST


def _bucket_thresholds():
    n = np.arange(0, REL_MAX_DIST + 1)
    exact = REL_BUCKETS // 2
    logv = (np.log(np.maximum(n, 1).astype(np.float32) / np.float32(exact))
            / np.float32(math.log(REL_MAX_DIST / exact))).astype(np.float32)
    large = np.minimum(exact + (logv * np.float32(REL_BUCKETS - exact)).astype(np.int32), REL_BUCKETS - 1)
    bucket = np.where(n < exact, n, large)
    assert np.all(np.diff(bucket) >= 0)
    return [int(np.argmax(bucket >= k)) for k in range(REL_BUCKETS)]


_THRESH = _bucket_thresholds()


def _nt_dot(a, b):
    return lax.dot_general(a, b, (((1,), (1,)), ((), ())), preferred_element_type=F32)


def _rms(x, w):
    return x * lax.rsqrt(jnp.mean(x * x, axis=-1, keepdims=True) + EPS) * w


def _bias_lookup(dist, rb_ref, col):
    val = jnp.full(dist.shape, rb_ref[0, col], F32)
    for k in range(1, REL_BUCKETS):
        val = jnp.where(dist >= _THRESH[k], rb_ref[k, col], val)
    return val


def _bias_kernel(rb_ref, ba_ref, bc_ref):
    h = pl.program_id(0)
    qi = lax.broadcasted_iota(jnp.int32, (A_BLK, 2 * A_BLK), 0)
    kj = lax.broadcasted_iota(jnp.int32, (A_BLK, 2 * A_BLK), 1)
    steps = qi + A_BLK - kj
    band = (steps >= 0) & (steps <= A_BLK)
    for g, (_, dil) in enumerate(A_BRANCHES):
        ba_ref[0, g] = jnp.where(band, _bias_lookup(steps * dil, rb_ref, h), NEG_INF)
    qi = lax.broadcasted_iota(jnp.int32, (C_BLOCK, C_BLOCK), 0)
    kj = lax.broadcasted_iota(jnp.int32, (C_BLOCK, C_BLOCK), 1)
    nblk = bc_ref.shape[2] // C_BLOCK
    for delta in range(nblk):
        dist = delta * C_BLOCK + qi - kj
        bias = _bias_lookup(dist, rb_ref, A_HEADS + h)
        if delta == 0:
            bias = jnp.where(dist >= 0, bias, NEG_INF)
        bc_ref[0, :, (nblk - 1 - delta) * C_BLOCK:(nblk - delta) * C_BLOCK] = bias


def _bias_tiles(rel_bias, seq):
    return pl.pallas_call(
        _bias_kernel,
        grid=(A_HEADS,),
        in_specs=[pl.BlockSpec(memory_space=pltpu.SMEM)],
        out_specs=[
            pl.BlockSpec((1, len(A_BRANCHES), A_BLK, 2 * A_BLK), lambda h: (h // PAIR, 0, h % PAIR, 0)),
            pl.BlockSpec((1, C_BLOCK, seq), lambda h: (h // PAIR, h % PAIR, 0)),
        ],
        out_shape=[
            jax.ShapeDtypeStruct((A_HEADS // PAIR, len(A_BRANCHES), PAIR * A_BLK, 2 * A_BLK), F32),
            jax.ShapeDtypeStruct((C_HEADS // PAIR, PAIR * C_BLOCK, seq), F32),
        ],
        name="bias_tiles",
    )(rel_bias)


def _inproj_kernel(x_ref, nw_ref, wa_ref, wc_ref, wb_ref, pa_ref, pc_ref, pb_ref):
    h = _rms(x_ref[...], nw_ref[...]).astype(BF16)
    pa_ref[...] = jnp.dot(h, wa_ref[...], preferred_element_type=F32)
    pc_ref[...] = jnp.dot(h, wc_ref[...], preferred_element_type=F32)
    pb_ref[...] = jnp.dot(h, wb_ref[...], preferred_element_type=F32)


def _inproj(x2d, nw, wa, wc, wb, tm):
    m = x2d.shape[0]
    const = lambda i: (0, 0)
    row = lambda i: (i, 0)
    return pl.pallas_call(
        _inproj_kernel,
        grid=(m // tm,),
        in_specs=[
            pl.BlockSpec((tm, D_MODEL), row),
            pl.BlockSpec((1, D_MODEL), const),
            pl.BlockSpec(wa.shape, const),
            pl.BlockSpec(wc.shape, const),
            pl.BlockSpec(wb.shape, const),
        ],
        out_specs=[
            pl.BlockSpec((tm, wa.shape[1]), row),
            pl.BlockSpec((tm, wc.shape[1]), row),
            pl.BlockSpec((tm, wb.shape[1]), row),
        ],
        out_shape=[
            jax.ShapeDtypeStruct((m, wa.shape[1]), F32),
            jax.ShapeDtypeStruct((m, wc.shape[1]), F32),
            jax.ShapeDtypeStruct((m, wb.shape[1]), F32),
        ],
        compiler_params=pltpu.CompilerParams(vmem_limit_bytes=VMEM_LIMIT),
        name="inproj",
    )(x2d, nw, wa, wc, wb)


A_GROUP = 16


def _dilated_kernel(q_ref, k_ref, v_ref, bias_ref, o_ref, qs_s, ks_s, vs_s, m_s, l_s, acc_s):
    seq = q_ref.shape[1]
    lane = lax.broadcasted_iota(jnp.int32, (A_BLK, LANES), 1)
    head0 = lane < HEAD_DIM
    vs_s[:, LANES:] = jnp.ones((seq, LANES), BF16)

    def block(g, dil, sub_start, tok_start, first):
        qrows = pl.ds(sub_start, A_BLK)
        q2 = jnp.concatenate([qs_s[0, qrows, :], qs_s[1, qrows, :]], axis=0)
        if first:
            krows = qrows
            bias = bias_ref[0, g, :, A_BLK:]
        else:
            krows = pl.ds(sub_start - A_BLK, 2 * A_BLK)
            bias = bias_ref[0, g]
        s = _nt_dot(q2, ks_s[krows, :]) + bias
        m = jnp.max(s, axis=-1, keepdims=True)
        p = jnp.exp(s - m).astype(BF16)
        r = jnp.dot(p, vs_s[krows, :], preferred_element_type=F32)
        trows = pl.ds(tok_start, A_BLK) if dil == 1 else pl.ds(tok_start, A_BLK, stride=dil)
        m_s[g, trows, :] = jnp.where(head0, m[:A_BLK], m[A_BLK:])
        l_s[g, trows, :] = jnp.where(head0, r[:A_BLK, LANES:], r[A_BLK:, LANES:])
        acc_s[g, trows, :] = jnp.where(head0, r[:A_BLK, :LANES], r[A_BLK:, :LANES])

    for g, (window, dil) in enumerate(A_BRANCHES):
        assert window // dil == A_BLK
        sub_len = seq // dil
        nb = sub_len // A_BLK
        assert (dil * nb) % A_GROUP == 0 and (dil * nb == A_GROUP or nb in (1, A_GROUP) or dil == 1)

        def gather(r, carry, dil=dil, sub_len=sub_len):
            src = pl.ds(r, sub_len) if dil == 1 else pl.ds(r, sub_len, stride=dil)
            dst = pl.ds(pl.multiple_of(r * sub_len, sub_len), sub_len)
            q = q_ref[0, src, :] * (HEAD_DIM ** -0.5)
            lane_l = lax.broadcasted_iota(jnp.int32, q.shape, 1)
            qs_s[0, dst, :] = jnp.where(lane_l < HEAD_DIM, q, 0.0).astype(BF16)
            qs_s[1, dst, :] = jnp.where(lane_l < HEAD_DIM, 0.0, q).astype(BF16)
            ks_s[dst, :] = k_ref[0, src, :].astype(BF16)
            vs_s[dst, :LANES] = v_ref[0, src, :].astype(BF16)
            return carry

        if dil == 1:
            gather(0, 0)
        else:
            lax.fori_loop(0, dil, gather, 0)

        def group(gi, carry, g=g, dil=dil, nb=nb):
            for j in range(A_GROUP):
                bi = gi * A_GROUP + j
                if isinstance(bi, int):
                    r, n = divmod(bi, nb)
                    first = n == 0
                elif nb == 1:
                    r, n, first = bi, 0, True
                elif nb == A_GROUP:
                    r, n, first = gi, j, j == 0
                else:
                    r, n, first = 0, bi, False
                sub_start = bi * A_BLK if isinstance(bi, int) else pl.multiple_of(bi * A_BLK, A_BLK)
                block(g, dil, sub_start, r + n * (A_BLK * dil), first)
            return carry

        ngroups = (dil * nb) // A_GROUP
        if dil == 1 or ngroups == 1:
            group(0, 0)
            lax.fori_loop(1, ngroups, group, 0)
        else:
            lax.fori_loop(0, ngroups, group, 0)

    m1, m2, m3 = m_s[0], m_s[1], m_s[2]
    m = jnp.maximum(jnp.maximum(m1, m2), m3)
    w1, w2, w3 = jnp.exp(m1 - m), jnp.exp(m2 - m), jnp.exp(m3 - m)
    num = w1 * acc_s[0] + w2 * acc_s[1] + w3 * acc_s[2]
    den = w1 * l_s[0] + w2 * l_s[1] + w3 * l_s[2]
    o_ref[0] = num / den


def _dilated(pa, bias_a, batch, seq):
    npair = A_HEADS // PAIR
    nbr = len(A_BRANCHES)
    blk = lambda off: pl.BlockSpec((1, seq, LANES), lambda b, p: (b, 0, off + p))
    return pl.pallas_call(
        _dilated_kernel,
        grid=(batch, npair),
        in_specs=[
            blk(0), blk(npair), blk(2 * npair),
            pl.BlockSpec((1, nbr, PAIR * A_BLK, 2 * A_BLK), lambda b, p: (p, 0, 0, 0)),
        ],
        out_specs=pl.BlockSpec((1, seq, LANES), lambda b, p: (b, 0, p)),
        out_shape=jax.ShapeDtypeStruct((batch, seq, A_WIDTH), F32),
        scratch_shapes=[
            pltpu.VMEM((PAIR, seq, LANES), BF16),
            pltpu.VMEM((seq, LANES), BF16),
            pltpu.VMEM((seq, 2 * LANES), BF16),
            pltpu.VMEM((nbr, seq, LANES), F32),
            pltpu.VMEM((nbr, seq, LANES), F32),
            pltpu.VMEM((nbr, seq, LANES), F32),
        ],
        compiler_params=pltpu.CompilerParams(vmem_limit_bytes=VMEM_LIMIT),
        name="dilated_attn",
    )(pa, pa, pa, bias_a)


GLA_TILE = 128
PB_Q, PB_K, PB_V, PB_R, PB_A = 0, B_QK, 2 * B_QK, 2 * B_QK + B_WIDTH, 2 * B_QK + 2 * B_WIDTH
PB_WIDTH = PB_A + LANES


def _log_sigmoid(z):
    return jnp.minimum(z, 0.0) - jnp.log1p(jnp.exp(-jnp.abs(z)))


def _gla_kernel(pb_ref, w2_ref, ba_ref, nw_ref, y_ref, st_s):
    seq = pb_ref.shape[1]
    c = B_CHUNK
    ri = lax.broadcasted_iota(jnp.int32, (GLA_TILE, GLA_TILE), 0)
    ci = lax.broadcasted_iota(jnp.int32, (GLA_TILE, GLA_TILE), 1)
    same = (ri // c) == (ci // c)
    tri = jnp.where(same & (ci <= ri), 1.0, 0.0).astype(F32)
    tot = jnp.where(same, 1.0, 0.0).astype(F32)
    kl = lax.broadcasted_iota(jnp.int32, (B_QK, B_WIDTH), 0) // B_KEY_DIM
    vl = lax.broadcasted_iota(jnp.int32, (B_QK, B_WIDTH), 1) // B_VAL_DIM
    expand = jnp.where(kl == vl, 1.0, 0.0).astype(BF16)
    vlt = lax.broadcasted_iota(jnp.int32, (B_WIDTH, B_QK), 0) // B_VAL_DIM
    klt = lax.broadcasted_iota(jnp.int32, (B_WIDTH, B_QK), 1) // B_KEY_DIM
    head_t = vlt == klt
    va = lax.broadcasted_iota(jnp.int32, (B_WIDTH, B_WIDTH), 0) // B_VAL_DIM
    vb = lax.broadcasted_iota(jnp.int32, (B_WIDTH, B_WIDTH), 1) // B_VAL_DIM
    head_mean = jnp.where(va == vb, 1.0 / B_VAL_DIM, 0.0).astype(F32)
    ii = lax.broadcasted_iota(jnp.int32, (c, c, B_QK), 0)
    jj = lax.broadcasted_iota(jnp.int32, (c, c, B_QK), 1)
    causal = jj <= ii
    chunk_of_lane = lax.broadcasted_iota(jnp.int32, (B_WIDTH, GLA_TILE), 1) // c

    st_s[...] = jnp.zeros_like(st_s)

    def tile(t, carry):
        rows = pl.ds(pl.multiple_of(t * GLA_TILE, GLA_TILE), GLA_TILE)
        z = jnp.dot(pb_ref[0, rows, PB_A:PB_A + LANES], w2_ref[...], precision=HIGHEST,
                    preferred_element_type=F32) + ba_ref[...]
        log_a = _log_sigmoid(z) * (1.0 / B_GATE_TAU)
        b = jnp.dot(tri, log_a, precision=HIGHEST, preferred_element_type=F32)
        b_last = jnp.dot(tot, log_a, precision=HIGHEST, preferred_element_type=F32)
        q = pb_ref[0, rows, PB_Q:PB_Q + B_QK] * (B_KEY_DIM ** -0.5)
        k = pb_ref[0, rows, PB_K:PB_K + B_QK]
        v = pb_ref[0, rows, PB_V:PB_V + B_WIDTH]
        q_dec = (q * jnp.exp(b)).astype(BF16)
        k_dec = (k * jnp.exp(b_last - b)).astype(BF16)
        gate = jnp.exp(b_last)
        v_t = v.T.astype(BF16)
        state = st_s[...]
        outs = []
        for m in range(GLA_TILE // c):
            sl = slice(m * c, (m + 1) * c)
            qn, kn, bn, vn = q[sl], k[sl], b[sl], v[sl]
            diff = bn[:, None, :] - bn[None, :, :]
            decay = jnp.exp(jnp.where(causal, diff, NEG_INF))
            w = (qn[:, None, :] * kn[None, :, :]) * decay
            attn = jnp.dot(w.reshape(c * c, B_QK).astype(BF16), expand,
                           preferred_element_type=F32)
            o_intra = jnp.sum(attn.reshape(c, c, B_WIDTH) * vn[None, :, :], axis=1)
            o_inter = _nt_dot(q_dec[sl], state.astype(BF16))
            outs.append(o_intra + o_inter)
            v_m = jnp.where(chunk_of_lane == m, v_t, jnp.zeros_like(v_t))
            u_t = jnp.dot(v_m, k_dec, preferred_element_type=F32)
            state = state * gate[m * c:m * c + 1, :] + jnp.where(head_t, u_t, 0.0)
        st_s[...] = state
        o = jnp.concatenate(outs, axis=0)
        ms = jnp.dot(o * o, head_mean, precision=HIGHEST, preferred_element_type=F32)
        r = pb_ref[0, rows, PB_R:PB_R + B_WIDTH]
        silu = r / (1.0 + jnp.exp(-r))
        y_ref[0, rows, :] = o * lax.rsqrt(ms + EPS) * nw_ref[...] * silu
        return carry

    lax.fori_loop(0, seq // GLA_TILE, tile, 0)


def _gla(pb, w2p, b_a, norm_w, batch, seq):
    const = lambda b: (0, 0)
    return pl.pallas_call(
        _gla_kernel,
        grid=(batch,),
        in_specs=[
            pl.BlockSpec((1, seq, PB_WIDTH), lambda b: (b, 0, 0)),
            pl.BlockSpec((LANES, B_QK), const),
            pl.BlockSpec((1, B_QK), const),
            pl.BlockSpec((1, B_WIDTH), const),
        ],
        out_specs=pl.BlockSpec((1, seq, B_WIDTH), lambda b: (b, 0, 0)),
        out_shape=jax.ShapeDtypeStruct((batch, seq, B_WIDTH), F32),
        scratch_shapes=[pltpu.VMEM((B_WIDTH, B_QK), F32)],
        compiler_params=pltpu.CompilerParams(vmem_limit_bytes=VMEM_LIMIT),
        name="gla",
    )(pb, w2p, b_a, norm_w)


def _moba_kernel(q_ref, k_ref, v_ref, bias_ref, o_ref, ka_s, qa_s, va_s, s_s):
    seq = q_ref.shape[1]
    nblk = seq // C_BLOCK
    head0 = lax.broadcasted_iota(jnp.int32, (C_BLOCK, LANES), 1) < HEAD_DIM

    k = k_ref[0]
    q = q_ref[0]
    ka_s[:, :LANES] = k.astype(BF16)
    row_blk = lax.broadcasted_iota(jnp.int32, (seq, LANES), 0) // C_BLOCK
    lane_id = lax.broadcasted_iota(jnp.int32, (seq, LANES), 1)
    ka_s[:, LANES:] = jnp.where(lane_id == row_blk, 1.0, 0.0).astype(BF16)
    va_s[:, :LANES] = v_ref[0].astype(BF16)
    va_s[:, LANES:] = jnp.ones((seq, LANES), BF16)
    k_mean = jnp.mean(k.reshape(nblk, C_BLOCK, LANES), axis=1)
    q_scaled = q * (HEAD_DIM ** -0.5)
    blk = lax.broadcasted_iota(jnp.int32, (nblk, seq), 0)
    q_blk = lax.broadcasted_iota(jnp.int32, (nblk, seq), 1) // C_BLOCK
    for hh in range(PAIR):
        qa_s[hh, :, :LANES] = jnp.where((lane_id < HEAD_DIM) == (hh == 0), q_scaled, 0.0).astype(BF16)
        gate = _nt_dot(k_mean.astype(BF16), qa_s[hh, :, :LANES])
        rank = jnp.zeros((nblk, seq), jnp.int32)
        for m in range(nblk):
            gm = gate[m:m + 1, :]
            beats = (gm > gate) | ((gm == gate) & (m < blk))
            rank = rank + jnp.where(beats & (m < q_blk), 1, 0)
        dropped = (blk < q_blk) & (rank >= C_TOPK)
        pen = jnp.where(dropped, NEG_INF, 0.0)
        assert nblk == SUBLANES
        pen_t = jnp.concatenate([pen, jnp.zeros((LANES - nblk, seq), F32)], axis=0).T
        qa_s[hh, :, LANES:] = pen_t.astype(BF16)

    for qb in range(nblk):
        nkeys = (qb + 1) * C_BLOCK
        qrows = slice(qb * C_BLOCK, (qb + 1) * C_BLOCK)
        q2 = jnp.concatenate([qa_s[0, qrows, :], qa_s[1, qrows, :]], axis=0)
        s = _nt_dot(q2, ka_s[:nkeys, :]) + bias_ref[0, :, (nblk - 1 - qb) * C_BLOCK:]
        s_s[:, :nkeys] = s
        m = jnp.max(s, axis=-1, keepdims=True)
        p = jnp.exp(s_s[:, :nkeys] - m).astype(BF16)
        r = jnp.dot(p, va_s[:nkeys, :], preferred_element_type=F32)
        o_ref[0, qrows, :] = jnp.where(head0, r[:C_BLOCK, :LANES] / r[:C_BLOCK, LANES:],
                                       r[C_BLOCK:, :LANES] / r[C_BLOCK:, LANES:])


def _moba(pc, bias_c, batch, seq):
    npair = C_HEADS // PAIR
    blk = lambda off: pl.BlockSpec((1, seq, LANES), lambda b, p: (b, 0, off + p))
    return pl.pallas_call(
        _moba_kernel,
        grid=(batch, npair),
        in_specs=[
            blk(0), blk(npair), blk(2 * npair),
            pl.BlockSpec((1, PAIR * C_BLOCK, seq), lambda b, p: (p, 0, 0)),
        ],
        out_specs=pl.BlockSpec((1, seq, LANES), lambda b, p: (b, 0, p)),
        out_shape=jax.ShapeDtypeStruct((batch, seq, C_WIDTH), F32),
        scratch_shapes=[
            pltpu.VMEM((seq, 2 * LANES), BF16),
            pltpu.VMEM((PAIR, seq, 2 * LANES), BF16),
            pltpu.VMEM((seq, 2 * LANES), BF16),
            pltpu.VMEM((PAIR * C_BLOCK, seq), F32),
        ],
        compiler_params=pltpu.CompilerParams(vmem_limit_bytes=VMEM_LIMIT),
        name="moba_attn",
    )(pc, pc, pc, bias_c)


FF_CHUNK = 1024


def _outffn_kernel(x_ref, ya_ref, yb_ref, yc_ref, woa_ref, wob_ref, woc_ref, n2_ref,
                   w1_ref, w2_ref, fn_ref, o_ref, *, final):
    x = x_ref[...]
    x = x + jnp.dot(ya_ref[...].astype(BF16), woa_ref[...], preferred_element_type=F32)
    x = x + jnp.dot(yb_ref[...].astype(BF16), wob_ref[...], preferred_element_type=F32)
    x = x + jnp.dot(yc_ref[...].astype(BF16), woc_ref[...], preferred_element_type=F32)
    h = _rms(x, n2_ref[...]).astype(BF16)
    ffn = None
    for j in range(D_FF // FF_CHUNK):
        cols = slice(j * FF_CHUNK, (j + 1) * FF_CHUNK)
        a = jnp.dot(h, w1_ref[:, cols], preferred_element_type=F32)
        a = jnp.square(jnp.maximum(a, 0.0)).astype(BF16)
        part = jnp.dot(a, w2_ref[cols, :], preferred_element_type=F32)
        ffn = part if ffn is None else ffn + part
    x = x + ffn
    if final:
        x = _rms(x, fn_ref[...])
    o_ref[...] = x


def _outffn(x2d, ya, yb, yc, woa, wob, woc, n2, w1, w2, fn, tm, final):
    m = x2d.shape[0]
    const = lambda i: (0, 0)
    row = lambda i: (i, 0)
    full = lambda a: pl.BlockSpec(a.shape, const)
    return pl.pallas_call(
        functools.partial(_outffn_kernel, final=final),
        grid=(m // tm,),
        in_specs=[
            pl.BlockSpec((tm, D_MODEL), row),
            pl.BlockSpec((tm, ya.shape[1]), row),
            pl.BlockSpec((tm, yb.shape[1]), row),
            pl.BlockSpec((tm, yc.shape[1]), row),
            full(woa), full(wob), full(woc), full(n2), full(w1), full(w2), full(fn),
        ],
        out_specs=pl.BlockSpec((tm, D_MODEL), row),
        out_shape=jax.ShapeDtypeStruct((m, D_MODEL), F32),
        compiler_params=pltpu.CompilerParams(vmem_limit_bytes=VMEM_LIMIT),
        name="outproj_ffn",
    )(x2d, ya, yb, yc, woa, wob, woc, n2, w1, w2, fn)


def _split_w_in(w):
    sizes = (A_WIDTH, A_WIDTH, A_WIDTH, B_QK, B_QK, B_WIDTH, B_WIDTH, B_GATE_RANK, C_WIDTH, C_WIDTH, C_WIDTH)
    offs = np.concatenate([[0], np.cumsum(sizes)])
    aq, ak, av, bq, bk, bv, br, ba, cq, ck, cv = [w[:, offs[i]:offs[i + 1]] for i in range(len(sizes))]
    ba = jnp.pad(ba, ((0, 0), (0, LANES - B_GATE_RANK)))
    cat = lambda parts: jnp.concatenate(parts, axis=1).astype(BF16)
    return cat([aq, ak, av]), cat([cq, ck, cv]), cat([bq, bk, bv, br, ba])


def kernel(x, norm1_w, w_in, gla_w_a2, gla_b_a, gla_norm_w, w_out, norm2_w, w_ff1, w_ff2, rel_bias, final_norm_w):
    batch, seq, d = x.shape
    depth = w_in.shape[0]
    assert d == D_MODEL and seq % C_BLOCK == 0 and seq % (A_BLK * A_BRANCHES[-1][1]) == 0
    tokens = batch * seq
    tm = 512 if tokens % 512 == 0 else seq
    bias_a, bias_c = _bias_tiles(rel_bias, seq)
    x2d = x.reshape(tokens, d)
    fn = final_norm_w.reshape(1, d)
    for i in range(depth):
        wa, wc, wb = _split_w_in(w_in[i])
        pa, pc, pb = _inproj(x2d, norm1_w[i].reshape(1, d), wa, wc, wb, tm)
        ya = _dilated(pa.reshape(batch, seq, -1), bias_a, batch, seq)
        w2p = jnp.pad(gla_w_a2[i], ((0, LANES - B_GATE_RANK), (0, 0)))
        yb = _gla(pb.reshape(batch, seq, -1), w2p, gla_b_a[i].reshape(1, B_QK),
                  gla_norm_w[i].reshape(1, B_WIDTH), batch, seq)
        yc = _moba(pc.reshape(batch, seq, -1), bias_c, batch, seq)
        wo = w_out[i].astype(BF16)
        x2d = _outffn(
            x2d, ya.reshape(tokens, -1), yb.reshape(tokens, -1), yc.reshape(tokens, -1),
            wo[:A_WIDTH], wo[A_WIDTH:A_WIDTH + B_WIDTH], wo[A_WIDTH + B_WIDTH:],
            norm2_w[i].reshape(1, d), w_ff1[i].astype(BF16), w_ff2[i].astype(BF16), fn,
            tm, final=(i == depth - 1))
    return x2d.reshape(batch, seq, d)
```

```python
import functools
import math

import numpy as np
import jax
import jax.numpy as jnp
from jax import lax
from jax.experimental import pallas as pl
from jax.experimental.pallas import tpu as pltpu

D_MODEL = 1024
HEAD_DIM = 64
A_HEADS = 6
A_BRANCHES = ((128, 1), (512, 4), (2048, 16))
B_HEADS = 4
B_KEY_DIM = 32
B_VAL_DIM = 64
B_GATE_RANK = 16
B_GATE_TAU = 16.0
B_CHUNK = 16
C_HEADS = 6
C_BLOCK = 256
C_TOPK = 3
REL_BUCKETS = 32
REL_MAX_DIST = 2048
D_FF = 4 * D_MODEL
EPS = 1e-6
NEG_INF = -1e30

A_WIDTH = A_HEADS * HEAD_DIM
B_QK = B_HEADS * B_KEY_DIM
B_WIDTH = B_HEADS * B_VAL_DIM
C_WIDTH = C_HEADS * HEAD_DIM

LANES = 128
SUBLANES = 8
A_BLK = 128
PAIR = LANES // HEAD_DIM
VMEM_LIMIT = 56 * 1024 * 1024

F32 = jnp.float32
BF16 = jnp.bfloat16
HIGHEST = lax.Precision.HIGHEST


def _bucket_thresholds():
    n = np.arange(0, REL_MAX_DIST + 1)
    exact = REL_BUCKETS // 2
    logv = (np.log(np.maximum(n, 1).astype(np.float32) / np.float32(exact))
            / np.float32(math.log(REL_MAX_DIST / exact))).astype(np.float32)
    large = np.minimum(exact + (logv * np.float32(REL_BUCKETS - exact)).astype(np.int32), REL_BUCKETS - 1)
    bucket = np.where(n < exact, n, large)
    assert np.all(np.diff(bucket) >= 0)
    return [int(np.argmax(bucket >= k)) for k in range(REL_BUCKETS)]


_THRESH = _bucket_thresholds()


def _nt_dot(a, b):
    return lax.dot_general(a, b, (((1,), (1,)), ((), ())), preferred_element_type=F32)


def _rms(x, w):
    return x * lax.rsqrt(jnp.mean(x * x, axis=-1, keepdims=True) + EPS) * w


def _bias_lookup(dist, rb_ref, col):
    val = jnp.full(dist.shape, rb_ref[0, col], F32)
    for k in range(1, REL_BUCKETS):
        val = jnp.where(dist >= _THRESH[k], rb_ref[k, col], val)
    return val


def _bias_kernel(rb_ref, ba_ref, bc_ref):
    h = pl.program_id(0)
    qi = lax.broadcasted_iota(jnp.int32, (A_BLK, 2 * A_BLK), 0)
    kj = lax.broadcasted_iota(jnp.int32, (A_BLK, 2 * A_BLK), 1)
    steps = qi + A_BLK - kj
    band = (steps >= 0) & (steps <= A_BLK)
    for g, (_, dil) in enumerate(A_BRANCHES):
        ba_ref[0, g] = jnp.where(band, _bias_lookup(steps * dil, rb_ref, h), NEG_INF)
    qi = lax.broadcasted_iota(jnp.int32, (C_BLOCK, C_BLOCK), 0)
    kj = lax.broadcasted_iota(jnp.int32, (C_BLOCK, C_BLOCK), 1)
    nblk = bc_ref.shape[2] // C_BLOCK
    for delta in range(nblk):
        dist = delta * C_BLOCK + qi - kj
        bias = _bias_lookup(dist, rb_ref, A_HEADS + h)
        if delta == 0:
            bias = jnp.where(dist >= 0, bias, NEG_INF)
        bc_ref[0, :, (nblk - 1 - delta) * C_BLOCK:(nblk - delta) * C_BLOCK] = bias


def _bias_tiles(rel_bias, seq):
    return pl.pallas_call(
        _bias_kernel,
        grid=(A_HEADS,),
        in_specs=[pl.BlockSpec(memory_space=pltpu.SMEM)],
        out_specs=[
            pl.BlockSpec((1, len(A_BRANCHES), A_BLK, 2 * A_BLK), lambda h: (h // PAIR, 0, h % PAIR, 0)),
            pl.BlockSpec((1, C_BLOCK, seq), lambda h: (h // PAIR, h % PAIR, 0)),
        ],
        out_shape=[
            jax.ShapeDtypeStruct((A_HEADS // PAIR, len(A_BRANCHES), PAIR * A_BLK, 2 * A_BLK), F32),
            jax.ShapeDtypeStruct((C_HEADS // PAIR, PAIR * C_BLOCK, seq), F32),
        ],
        name="bias_tiles",
    )(rel_bias)


def _inproj_kernel(x_ref, nw_ref, wa_ref, wc_ref, wb_ref, pa_ref, pc_ref, pb_ref):
    h = _rms(x_ref[...], nw_ref[...]).astype(BF16)
    pa_ref[...] = jnp.dot(h, wa_ref[...], preferred_element_type=F32)
    pc_ref[...] = jnp.dot(h, wc_ref[...], preferred_element_type=F32)
    pb_ref[...] = jnp.dot(h, wb_ref[...], preferred_element_type=F32)


def _inproj(x2d, nw, wa, wc, wb, tm):
    m = x2d.shape[0]
    const = lambda i: (0, 0)
    row = lambda i: (i, 0)
    return pl.pallas_call(
        _inproj_kernel,
        grid=(m // tm,),
        in_specs=[
            pl.BlockSpec((tm, D_MODEL), row),
            pl.BlockSpec((1, D_MODEL), const),
            pl.BlockSpec(wa.shape, const),
            pl.BlockSpec(wc.shape, const),
            pl.BlockSpec(wb.shape, const),
        ],
        out_specs=[
            pl.BlockSpec((tm, wa.shape[1]), row),
            pl.BlockSpec((tm, wc.shape[1]), row),
            pl.BlockSpec((tm, wb.shape[1]), row),
        ],
        out_shape=[
            jax.ShapeDtypeStruct((m, wa.shape[1]), F32),
            jax.ShapeDtypeStruct((m, wc.shape[1]), F32),
            jax.ShapeDtypeStruct((m, wb.shape[1]), F32),
        ],
        compiler_params=pltpu.CompilerParams(vmem_limit_bytes=VMEM_LIMIT),
        name="inproj",
    )(x2d, nw, wa, wc, wb)


def _dilated_kernel(q_ref, k_ref, v_ref, bias_ref, o_ref, qs_s, ks_s, vs_s, x_s, m_s, l_s, acc_s):
    seq = q_ref.shape[1]
    (w1, d1), (w2, d2), (w3, d3) = A_BRANCHES
    assert d1 == 1 and d3 == d2 * d2 and all(w // d == A_BLK for w, d in A_BRANCHES)
    coarse = seq // d2
    lane = lax.broadcasted_iota(jnp.int32, (A_BLK, LANES), 1)
    head0 = lane < HEAD_DIM
    vs_s[:, :, LANES:] = jnp.ones((len(A_BRANCHES), seq, LANES), BF16)

    def put(g, dst, q, k, v):
        lane_l = lax.broadcasted_iota(jnp.int32, q.shape, 1)
        qs_s[g, 0, dst, :] = jnp.where(lane_l < HEAD_DIM, q, 0.0).astype(BF16)
        qs_s[g, 1, dst, :] = jnp.where(lane_l < HEAD_DIM, 0.0, q).astype(BF16)
        ks_s[g, dst, :] = k.astype(BF16)
        vs_s[g, dst, :LANES] = v.astype(BF16)

    def block(g, bi, first, out_rows):
        qrows = pl.ds(bi * A_BLK, A_BLK)
        q2 = jnp.concatenate([qs_s[g, 0, qrows, :], qs_s[g, 1, qrows, :]], axis=0)
        if first:
            krows = qrows
            bias = bias_ref[0, g, :, A_BLK:]
        else:
            krows = pl.ds((bi - 1) * A_BLK, 2 * A_BLK)
            bias = bias_ref[0, g]
        s = _nt_dot(q2, ks_s[g, krows, :]) + bias
        m = jnp.max(s, axis=-1, keepdims=True)
        p = jnp.exp(s - m).astype(BF16)
        r = jnp.dot(p, vs_s[g, krows, :], preferred_element_type=F32)
        m_s[g, out_rows, :] = jnp.where(head0, m[:A_BLK], m[A_BLK:])
        l_s[g, out_rows, :] = jnp.where(head0, r[:A_BLK, LANES:], r[A_BLK:, LANES:])
        acc_s[g, out_rows, :] = jnp.where(head0, r[:A_BLK, :LANES], r[A_BLK:, :LANES])

    put(0, pl.ds(0, seq), q_ref[0] * (HEAD_DIM ** -0.5), k_ref[0], v_ref[0])
    for bi in range(seq // A_BLK):
        block(0, bi, bi == 0, pl.ds(bi * A_BLK, A_BLK))

    for r in range(d2):
        src = pl.ds(r, coarse, stride=d2)
        dst = pl.ds(r * coarse, coarse)
        q, k, v = q_ref[0, src, :] * (HEAD_DIM ** -0.5), k_ref[0, src, :], v_ref[0, src, :]
        x_s[0, dst, :], x_s[1, dst, :], x_s[2, dst, :] = q, k, v
        put(1, dst, q, k, v)
    nb = coarse // A_BLK
    for bi in range(d2 * nb):
        block(1, bi, bi % nb == 0, pl.ds(bi * A_BLK, A_BLK))

    fine = seq // d3
    nb = fine // A_BLK
    for r in range(d3):
        lo, hi = r % d2, r // d2
        src = pl.ds(lo * coarse + hi, fine, stride=d2)
        put(2, pl.ds(r * fine, fine), x_s[0, src, :], x_s[1, src, :], x_s[2, src, :])
    for bi in range(d3 * nb):
        r, n = divmod(bi, nb)
        lo, hi = r % d2, r // d2
        block(2, bi, n == 0, pl.ds(lo * coarse + hi + n * A_BLK * d2, A_BLK, stride=d2))

    for r in range(d2):
        rows = pl.ds(r * coarse, coarse)
        tok = pl.ds(r, coarse, stride=d2)
        m1, m2, m3 = m_s[0, tok, :], m_s[1, rows, :], m_s[2, rows, :]
        m = jnp.maximum(jnp.maximum(m1, m2), m3)
        e1, e2, e3 = jnp.exp(m1 - m), jnp.exp(m2 - m), jnp.exp(m3 - m)
        num = e1 * acc_s[0, tok, :] + e2 * acc_s[1, rows, :] + e3 * acc_s[2, rows, :]
        den = e1 * l_s[0, tok, :] + e2 * l_s[1, rows, :] + e3 * l_s[2, rows, :]
        o_ref[0, tok, :] = num / den


def _dilated(pa, bias_a, batch, seq):
    npair = A_HEADS // PAIR
    nbr = len(A_BRANCHES)
    blk = lambda off: pl.BlockSpec((1, seq, LANES), lambda b, p: (b, 0, off + p))
    return pl.pallas_call(
        _dilated_kernel,
        grid=(batch, npair),
        in_specs=[
            blk(0), blk(npair), blk(2 * npair),
            pl.BlockSpec((1, nbr, PAIR * A_BLK, 2 * A_BLK), lambda b, p: (p, 0, 0, 0)),
        ],
        out_specs=pl.BlockSpec((1, seq, LANES), lambda b, p: (b, 0, p)),
        out_shape=jax.ShapeDtypeStruct((batch, seq, A_WIDTH), F32),
        scratch_shapes=[
            pltpu.VMEM((nbr, PAIR, seq, LANES), BF16),
            pltpu.VMEM((nbr, seq, LANES), BF16),
            pltpu.VMEM((nbr, seq, 2 * LANES), BF16),
            pltpu.VMEM((3, seq, LANES), F32),
            pltpu.VMEM((nbr, seq, LANES), F32),
            pltpu.VMEM((nbr, seq, LANES), F32),
            pltpu.VMEM((nbr, seq, LANES), F32),
        ],
        compiler_params=pltpu.CompilerParams(vmem_limit_bytes=VMEM_LIMIT),
        name="dilated_attn",
    )(pa, pa, pa, bias_a)


GLA_TILE = 128
GLA_LEVELS = 7
GLA_UNROLL = 4
PB_Q, PB_K, PB_V, PB_R, PB_A = 0, B_QK, 2 * B_QK, 2 * B_QK + B_WIDTH, 2 * B_QK + 2 * B_WIDTH
PB_WIDTH = PB_A + LANES


def _log_sigmoid(z):
    return jnp.minimum(z, 0.0) - jnp.log1p(jnp.exp(-jnp.abs(z)))


def _gla_constants():
    t = GLA_TILE
    i = np.arange(t)[:, None]
    c = np.arange(t)[None, :]
    blocks = [(c <= i).astype(np.float32)]
    for lvl in range(1, GLA_LEVELS + 1):
        w = t >> lvl
        split = (i // (2 * w)) * (2 * w) + w - 1
        blocks.append((c <= i).astype(np.float32) - (c <= split).astype(np.float32))
    x = np.arange(t)[:, None] ^ np.arange(t)[None, :]
    level_of_pair = np.where(np.arange(t)[None, :] < np.arange(t)[:, None],
                             np.floor(np.log2(np.maximum(x, 1))).astype(np.int32), -1)
    return np.concatenate(blocks, axis=0), np.tile(level_of_pair, (1, B_HEADS)).astype(np.int32)


def _gla_kernel(pb_ref, w2_ref, ba_ref, nw_ref, g_ref, lvl_ref, y_ref, st_s):
    seq = pb_ref.shape[1]
    t = GLA_TILE
    kl = lax.broadcasted_iota(jnp.int32, (B_QK, B_WIDTH), 0) // B_KEY_DIM
    vl = lax.broadcasted_iota(jnp.int32, (B_QK, B_WIDTH), 1) // B_VAL_DIM
    expand = jnp.where(kl == vl, 1.0, 0.0).astype(BF16)
    vlt = lax.broadcasted_iota(jnp.int32, (B_WIDTH, B_QK), 0) // B_VAL_DIM
    klt = lax.broadcasted_iota(jnp.int32, (B_WIDTH, B_QK), 1) // B_KEY_DIM
    head_t = vlt == klt
    va = (lax.broadcasted_iota(jnp.int32, (2 * B_WIDTH, B_WIDTH), 0) % B_WIDTH) // B_VAL_DIM
    vb = lax.broadcasted_iota(jnp.int32, (2 * B_WIDTH, B_WIDTH), 1) // B_VAL_DIM
    head_mean2 = jnp.where(va == vb, 1.0 / B_VAL_DIM, 0.0).astype(BF16)
    row = lax.broadcasted_iota(jnp.int32, (t, B_QK), 0)
    rep_head_k = lax.broadcasted_iota(jnp.int32, (B_HEADS * t, B_QK), 0) // t
    own_key_lanes = rep_head_k == lax.broadcasted_iota(jnp.int32, (B_HEADS * t, B_QK), 1) // B_KEY_DIM
    rep_head_v = lax.broadcasted_iota(jnp.int32, (B_HEADS * t, B_WIDTH), 0) // t
    own_val_lanes = rep_head_v == lax.broadcasted_iota(jnp.int32, (B_HEADS * t, B_WIDTH), 1) // B_VAL_DIM
    zero_k4 = jnp.zeros((B_HEADS * t, B_QK), BF16)
    zero_v4 = jnp.zeros((B_HEADS * t, B_WIDTH), BF16)

    st_s[...] = jnp.zeros_like(st_s)

    def tile(ti, carry):
        rows = pl.ds(pl.multiple_of(ti * t, t), t)
        z = jnp.dot(pb_ref[0, rows, PB_A:PB_A + LANES], w2_ref[...], precision=HIGHEST,
                    preferred_element_type=F32) + ba_ref[...]
        log_a = _log_sigmoid(z) * (1.0 / B_GATE_TAU)
        a1 = log_a.astype(BF16)
        a2 = (log_a - a1.astype(F32)).astype(BF16)
        dd = jnp.dot(g_ref[...], jnp.concatenate([a1, a2], axis=1), preferred_element_type=F32)
        dist = dd[:, :B_QK] + dd[:, B_QK:]
        b = dist[:t]
        b_last = b[t - 1:t, :]
        q = pb_ref[0, rows, PB_Q:PB_Q + B_QK] * (B_KEY_DIM ** -0.5)
        k = pb_ref[0, rows, PB_K:PB_K + B_QK]
        v = pb_ref[0, rows, PB_V:PB_V + B_WIDTH]
        v_bf = v.astype(BF16)
        state = st_s[...]
        o = _nt_dot((q * jnp.exp(b)).astype(BF16), state.astype(BF16))
        k_dec = (k * jnp.exp(b_last - b)).astype(BF16)
        u_t = jnp.dot(v.T.astype(BF16), k_dec, preferred_element_type=F32)
        st_s[...] = state * jnp.exp(b_last) + jnp.where(head_t, u_t, 0.0)
        o = o + jnp.dot((q * k).astype(BF16), expand, preferred_element_type=F32) * v
        attn = jnp.zeros((t, B_HEADS * t), F32)
        for lvl in range(1, GLA_LEVELS + 1):
            w = t >> lvl
            decay = jnp.exp(-jnp.abs(dist[lvl * t:(lvl + 1) * t]))
            after = (row & w) != 0
            q_l = (jnp.where(after, q * decay, 0.0)).astype(BF16)
            k_l = jnp.where(after, 0.0, k * decay).astype(BF16)
            k4 = jnp.where(own_key_lanes, jnp.concatenate([k_l] * B_HEADS, axis=0), zero_k4)
            attn = jnp.where(lvl_ref[...] == GLA_LEVELS - lvl, _nt_dot(q_l, k4), attn)
        v4 = jnp.where(own_val_lanes, jnp.concatenate([v_bf] * B_HEADS, axis=0), zero_v4)
        o = o + jnp.dot(attn.astype(BF16), v4, preferred_element_type=F32)
        sq = o * o
        sq_hi = sq.astype(BF16)
        sq_lo = (sq - sq_hi.astype(F32)).astype(BF16)
        ms = jnp.dot(jnp.concatenate([sq_hi, sq_lo], axis=1), head_mean2, preferred_element_type=F32)
        r = pb_ref[0, rows, PB_R:PB_R + B_WIDTH]
        silu = r / (1.0 + jnp.exp(-r))
        y_ref[0, rows, :] = o * lax.rsqrt(ms + EPS) * nw_ref[...] * silu
        return carry

    lax.fori_loop(0, seq // t, tile, 0, unroll=GLA_UNROLL)


def _gla(pb, w2p, b_a, norm_w, batch, seq):
    const = lambda b: (0, 0)
    g_all, lvl = _gla_constants()
    g_all = jnp.asarray(g_all, BF16)
    lvl = jnp.asarray(lvl)
    return pl.pallas_call(
        _gla_kernel,
        grid=(batch,),
        in_specs=[
            pl.BlockSpec((1, seq, PB_WIDTH), lambda b: (b, 0, 0)),
            pl.BlockSpec((LANES, B_QK), const),
            pl.BlockSpec((1, B_QK), const),
            pl.BlockSpec((1, B_WIDTH), const),
            pl.BlockSpec(g_all.shape, const),
            pl.BlockSpec(lvl.shape, const),
        ],
        out_specs=pl.BlockSpec((1, seq, B_WIDTH), lambda b: (b, 0, 0)),
        out_shape=jax.ShapeDtypeStruct((batch, seq, B_WIDTH), F32),
        scratch_shapes=[pltpu.VMEM((B_WIDTH, B_QK), F32)],
        compiler_params=pltpu.CompilerParams(vmem_limit_bytes=VMEM_LIMIT),
        name="gla",
    )(pb, w2p, b_a, norm_w, g_all, lvl)


def _moba_kernel(q_ref, k_ref, v_ref, bias_ref, o_ref, ka_s, qa_s, va_s, s_s):
    seq = q_ref.shape[1]
    nblk = seq // C_BLOCK
    head0 = lax.broadcasted_iota(jnp.int32, (C_BLOCK, LANES), 1) < HEAD_DIM

    k = k_ref[0]
    q = q_ref[0]
    ka_s[:, :LANES] = k.astype(BF16)
    row_blk = lax.broadcasted_iota(jnp.int32, (seq, LANES), 0) // C_BLOCK
    lane_id = lax.broadcasted_iota(jnp.int32, (seq, LANES), 1)
    ka_s[:, LANES:] = jnp.where(lane_id == row_blk, 1.0, 0.0).astype(BF16)
    va_s[:, :LANES] = v_ref[0].astype(BF16)
    va_s[:, LANES:] = jnp.ones((seq, LANES), BF16)
    k_mean = jnp.mean(k.reshape(nblk, C_BLOCK, LANES), axis=1)
    q_scaled = q * (HEAD_DIM ** -0.5)
    blk = lax.broadcasted_iota(jnp.int32, (nblk, seq), 0)
    q_blk = lax.broadcasted_iota(jnp.int32, (nblk, seq), 1) // C_BLOCK
    for hh in range(PAIR):
        qa_s[hh, :, :LANES] = jnp.where((lane_id < HEAD_DIM) == (hh == 0), q_scaled, 0.0).astype(BF16)
        gate = _nt_dot(k_mean.astype(BF16), qa_s[hh, :, :LANES])
        rank = jnp.zeros((nblk, seq), jnp.int32)
        for m in range(nblk):
            gm = gate[m:m + 1, :]
            beats = (gm > gate) | ((gm == gate) & (m < blk))
            rank = rank + jnp.where(beats & (m < q_blk), 1, 0)
        dropped = (blk < q_blk) & (rank >= C_TOPK)
        pen = jnp.where(dropped, NEG_INF, 0.0)
        assert nblk == SUBLANES
        pen_t = jnp.concatenate([pen, jnp.zeros((LANES - nblk, seq), F32)], axis=0).T
        qa_s[hh, :, LANES:] = pen_t.astype(BF16)

    for qb in range(nblk):
        nkeys = (qb + 1) * C_BLOCK
        qrows = slice(qb * C_BLOCK, (qb + 1) * C_BLOCK)
        q2 = jnp.concatenate([qa_s[0, qrows, :], qa_s[1, qrows, :]], axis=0)
        s = _nt_dot(q2, ka_s[:nkeys, :]) + bias_ref[0, :, (nblk - 1 - qb) * C_BLOCK:]
        s_s[:, :nkeys] = s
        m = jnp.max(s, axis=-1, keepdims=True)
        p = jnp.exp(s_s[:, :nkeys] - m).astype(BF16)
        r = jnp.dot(p, va_s[:nkeys, :], preferred_element_type=F32)
        o_ref[0, qrows, :] = jnp.where(head0, r[:C_BLOCK, :LANES] / r[:C_BLOCK, LANES:],
                                       r[C_BLOCK:, :LANES] / r[C_BLOCK:, LANES:])


def _moba(pc, bias_c, batch, seq):
    npair = C_HEADS // PAIR
    blk = lambda off: pl.BlockSpec((1, seq, LANES), lambda b, p: (b, 0, off + p))
    return pl.pallas_call(
        _moba_kernel,
        grid=(batch, npair),
        in_specs=[
            blk(0), blk(npair), blk(2 * npair),
            pl.BlockSpec((1, PAIR * C_BLOCK, seq), lambda b, p: (p, 0, 0)),
        ],
        out_specs=pl.BlockSpec((1, seq, LANES), lambda b, p: (b, 0, p)),
        out_shape=jax.ShapeDtypeStruct((batch, seq, C_WIDTH), F32),
        scratch_shapes=[
            pltpu.VMEM((seq, 2 * LANES), BF16),
            pltpu.VMEM((PAIR, seq, 2 * LANES), BF16),
            pltpu.VMEM((seq, 2 * LANES), BF16),
            pltpu.VMEM((PAIR * C_BLOCK, seq), F32),
        ],
        compiler_params=pltpu.CompilerParams(vmem_limit_bytes=VMEM_LIMIT),
        name="moba_attn",
    )(pc, pc, pc, bias_c)


FF_CHUNK = 1024


def _outffn_kernel(x_ref, ya_ref, yb_ref, yc_ref, woa_ref, wob_ref, woc_ref, n2_ref,
                   w1_ref, w2_ref, fn_ref, o_ref, *, final):
    x = x_ref[...]
    x = x + jnp.dot(ya_ref[...].astype(BF16), woa_ref[...], preferred_element_type=F32)
    x = x + jnp.dot(yb_ref[...].astype(BF16), wob_ref[...], preferred_element_type=F32)
    x = x + jnp.dot(yc_ref[...].astype(BF16), woc_ref[...], preferred_element_type=F32)
    h = _rms(x, n2_ref[...]).astype(BF16)
    ffn = None
    for j in range(D_FF // FF_CHUNK):
        cols = slice(j * FF_CHUNK, (j + 1) * FF_CHUNK)
        a = jnp.dot(h, w1_ref[:, cols], preferred_element_type=F32)
        a = jnp.square(jnp.maximum(a, 0.0)).astype(BF16)
        part = jnp.dot(a, w2_ref[cols, :], preferred_element_type=F32)
        ffn = part if ffn is None else ffn + part
    x = x + ffn
    if final:
        x = _rms(x, fn_ref[...])
    o_ref[...] = x


def _outffn(x2d, ya, yb, yc, woa, wob, woc, n2, w1, w2, fn, tm, final):
    m = x2d.shape[0]
    const = lambda i: (0, 0)
    row = lambda i: (i, 0)
    full = lambda a: pl.BlockSpec(a.shape, const)
    return pl.pallas_call(
        functools.partial(_outffn_kernel, final=final),
        grid=(m // tm,),
        in_specs=[
            pl.BlockSpec((tm, D_MODEL), row),
            pl.BlockSpec((tm, ya.shape[1]), row),
            pl.BlockSpec((tm, yb.shape[1]), row),
            pl.BlockSpec((tm, yc.shape[1]), row),
            full(woa), full(wob), full(woc), full(n2), full(w1), full(w2), full(fn),
        ],
        out_specs=pl.BlockSpec((tm, D_MODEL), row),
        out_shape=jax.ShapeDtypeStruct((m, D_MODEL), F32),
        compiler_params=pltpu.CompilerParams(vmem_limit_bytes=VMEM_LIMIT),
        name="outproj_ffn",
    )(x2d, ya, yb, yc, woa, wob, woc, n2, w1, w2, fn)


def _split_w_in(w):
    sizes = (A_WIDTH, A_WIDTH, A_WIDTH, B_QK, B_QK, B_WIDTH, B_WIDTH, B_GATE_RANK, C_WIDTH, C_WIDTH, C_WIDTH)
    offs = np.concatenate([[0], np.cumsum(sizes)])
    aq, ak, av, bq, bk, bv, br, ba, cq, ck, cv = [w[:, offs[i]:offs[i + 1]] for i in range(len(sizes))]
    ba = jnp.pad(ba, ((0, 0), (0, LANES - B_GATE_RANK)))
    cat = lambda parts: jnp.concatenate(parts, axis=1).astype(BF16)
    return cat([aq, ak, av]), cat([cq, ck, cv]), cat([bq, bk, bv, br, ba])


def kernel(x, norm1_w, w_in, gla_w_a2, gla_b_a, gla_norm_w, w_out, norm2_w, w_ff1, w_ff2, rel_bias, final_norm_w):
    batch, seq, d = x.shape
    depth = w_in.shape[0]
    assert d == D_MODEL and seq % C_BLOCK == 0 and seq % (A_BLK * A_BRANCHES[-1][1]) == 0
    tokens = batch * seq
    tm = 512 if tokens % 512 == 0 else seq
    bias_a, bias_c = _bias_tiles(rel_bias, seq)
    x2d = x.reshape(tokens, d)
    fn = final_norm_w.reshape(1, d)
    for i in range(depth):
        wa, wc, wb = _split_w_in(w_in[i])
        pa, pc, pb = _inproj(x2d, norm1_w[i].reshape(1, d), wa, wc, wb, tm)
        ya = _dilated(pa.reshape(batch, seq, -1), bias_a, batch, seq)
        w2p = jnp.pad(gla_w_a2[i], ((0, LANES - B_GATE_RANK), (0, 0)))
        yb = _gla(pb.reshape(batch, seq, -1), w2p, gla_b_a[i].reshape(1, B_QK),
                  gla_norm_w[i].reshape(1, B_WIDTH), batch, seq)
        yc = _moba(pc.reshape(batch, seq, -1), bias_c, batch, seq)
        wo = w_out[i].astype(BF16)
        x2d = _outffn(
            x2d, ya.reshape(tokens, -1), yb.reshape(tokens, -1), yc.reshape(tokens, -1),
            wo[:A_WIDTH], wo[A_WIDTH:A_WIDTH + B_WIDTH], wo[A_WIDTH + B_WIDTH:],
            norm2_w[i].reshape(1, d), w_ff1[i].astype(BF16), w_ff2[i].astype(BF16), fn,
            tm, final=(i == depth - 1))
    return x2d.reshape(batch, seq, d)
```

```python
import functools
import math

import numpy as np
import jax
import jax.numpy as jnp
from jax import lax
from jax.experimental import pallas as pl
from jax.experimental.pallas import tpu as pltpu

D_MODEL = 1024
HEAD_DIM = 64
A_HEADS = 6
A_BRANCHES = ((128, 1), (512, 4), (2048, 16))
B_HEADS = 4
B_KEY_DIM = 32
B_VAL_DIM = 64
B_GATE_RANK = 16
B_GATE_TAU = 16.0
B_CHUNK = 16
C_HEADS = 6
C_BLOCK = 256
C_TOPK = 3
REL_BUCKETS = 32
REL_MAX_DIST = 2048
D_FF = 4 * D_MODEL
EPS = 1e-6
NEG_INF = -1e30
LOG2E = math.log2(math.e)

A_WIDTH = A_HEADS * HEAD_DIM
B_QK = B_HEADS * B_KEY_DIM
B_WIDTH = B_HEADS * B_VAL_DIM
C_WIDTH = C_HEADS * HEAD_DIM

LANES = 128
SUBLANES = 8
A_BLK = 128
PAIR = LANES // HEAD_DIM
VMEM_LIMIT = 56 * 1024 * 1024
F32 = jnp.float32
BF16 = jnp.bfloat16
HIGHEST = lax.Precision.HIGHEST


def _bucket_thresholds():
    n = np.arange(0, REL_MAX_DIST + 1)
    exact = REL_BUCKETS // 2
    logv = (np.log(np.maximum(n, 1).astype(np.float32) / np.float32(exact))
            / np.float32(math.log(REL_MAX_DIST / exact))).astype(np.float32)
    large = np.minimum(exact + (logv * np.float32(REL_BUCKETS - exact)).astype(np.int32), REL_BUCKETS - 1)
    bucket = np.where(n < exact, n, large)
    assert np.all(np.diff(bucket) >= 0)
    return [int(np.argmax(bucket >= k)) for k in range(REL_BUCKETS)]


_THRESH = _bucket_thresholds()


def _nt_dot(a, b):
    return lax.dot_general(a, b, (((1,), (1,)), ((), ())), preferred_element_type=F32)


def _rms(x, w):
    return x * lax.rsqrt(jnp.mean(x * x, axis=-1, keepdims=True) + EPS) * w


def _bias_lookup(dist, rb_ref, col):
    val = jnp.full(dist.shape, rb_ref[0, col], F32)
    for k in range(1, REL_BUCKETS):
        val = jnp.where(dist >= _THRESH[k], rb_ref[k, col], val)
    return val


def _bias_kernel(rb_ref, ba_ref, bc_ref):
    h = pl.program_id(0)
    qi = lax.broadcasted_iota(jnp.int32, (A_BLK, 2 * A_BLK), 0)
    kj = lax.broadcasted_iota(jnp.int32, (A_BLK, 2 * A_BLK), 1)
    steps = qi + A_BLK - kj
    band = (steps >= 0) & (steps <= A_BLK)
    for g, (_, dil) in enumerate(A_BRANCHES):
        ba_ref[0, g] = jnp.where(band, _bias_lookup(steps * dil, rb_ref, h) * LOG2E, NEG_INF)
    qi = lax.broadcasted_iota(jnp.int32, (C_BLOCK, C_BLOCK), 0)
    kj = lax.broadcasted_iota(jnp.int32, (C_BLOCK, C_BLOCK), 1)
    nblk = bc_ref.shape[2] // C_BLOCK
    for delta in range(nblk):
        dist = delta * C_BLOCK + qi - kj
        bias = _bias_lookup(dist, rb_ref, A_HEADS + h) * LOG2E
        if delta == 0:
            bias = jnp.where(dist >= 0, bias, NEG_INF)
        bc_ref[0, :, (nblk - 1 - delta) * C_BLOCK:(nblk - delta) * C_BLOCK] = bias


def _bias_tiles(rel_bias, seq):
    return pl.pallas_call(
        _bias_kernel,
        grid=(A_HEADS,),
        in_specs=[pl.BlockSpec(memory_space=pltpu.SMEM)],
        out_specs=[
            pl.BlockSpec((1, len(A_BRANCHES), A_BLK, 2 * A_BLK), lambda h: (h // PAIR, 0, h % PAIR, 0)),
            pl.BlockSpec((1, C_BLOCK, seq), lambda h: (h // PAIR, h % PAIR, 0)),
        ],
        out_shape=[
            jax.ShapeDtypeStruct((A_HEADS // PAIR, len(A_BRANCHES), PAIR * A_BLK, 2 * A_BLK), F32),
            jax.ShapeDtypeStruct((C_HEADS // PAIR, PAIR * C_BLOCK, seq), F32),
        ],
        name="bias_tiles",
    )(rel_bias)


def _inproj_kernel(x_ref, nw_ref, wa_ref, wc_ref, wb_ref, pa_ref, pc_ref, pb_ref):
    h = _rms(x_ref[...], nw_ref[...]).astype(BF16)
    pa_ref[...] = jnp.dot(h, wa_ref[...], preferred_element_type=F32)
    pc_ref[...] = jnp.dot(h, wc_ref[...], preferred_element_type=F32)
    pb_ref[...] = jnp.dot(h, wb_ref[...], preferred_element_type=F32)


def _inproj(x2d, nw, wa, wc, wb, tm):
    m = x2d.shape[0]
    const = lambda i: (0, 0)
    row = lambda i: (i, 0)
    return pl.pallas_call(
        _inproj_kernel,
        grid=(m // tm,),
        in_specs=[
            pl.BlockSpec((tm, D_MODEL), row),
            pl.BlockSpec((1, D_MODEL), const),
            pl.BlockSpec(wa.shape, const),
            pl.BlockSpec(wc.shape, const),
            pl.BlockSpec(wb.shape, const),
        ],
        out_specs=[
            pl.BlockSpec((tm, wa.shape[1]), row),
            pl.BlockSpec((tm, wc.shape[1]), row),
            pl.BlockSpec((tm, wb.shape[1]), row),
        ],
        out_shape=[
            jax.ShapeDtypeStruct((m, wa.shape[1]), F32),
            jax.ShapeDtypeStruct((m, wc.shape[1]), F32),
            jax.ShapeDtypeStruct((m, wb.shape[1]), F32),
        ],
        compiler_params=pltpu.CompilerParams(vmem_limit_bytes=VMEM_LIMIT),
        name="inproj",
    )(x2d, nw, wa, wc, wb)


A_QSCALE = HEAD_DIM ** -0.5 * LOG2E


def _dilated_kernel(q_ref, k_ref, v_ref, bias_ref, o_ref, qs_s, ks_s, vs_s, x_s, m_s, l_s, acc_s):
    seq = q_ref.shape[1]
    (w1, d1), (w2, d2), (w3, d3) = A_BRANCHES
    assert d1 == 1 and d3 == d2 * d2 and all(w // d == A_BLK for w, d in A_BRANCHES)
    coarse = seq // d2
    lane = lax.broadcasted_iota(jnp.int32, (A_BLK, LANES), 1)
    head0 = lane < HEAD_DIM
    vs_s[:, :, LANES:] = jnp.ones((len(A_BRANCHES), seq, LANES), BF16)

    def put(g, dst, q, k, v):
        lane_l = lax.broadcasted_iota(jnp.int32, q.shape, 1)
        qs_s[g, 0, dst, :] = jnp.where(lane_l < HEAD_DIM, q, 0.0).astype(BF16)
        qs_s[g, 1, dst, :] = jnp.where(lane_l < HEAD_DIM, 0.0, q).astype(BF16)
        ks_s[g, dst, :] = k.astype(BF16)
        vs_s[g, dst, :LANES] = v.astype(BF16)

    def block(g, bi, first, out_rows):
        qrows = pl.ds(bi * A_BLK, A_BLK)
        q2 = jnp.concatenate([qs_s[g, 0, qrows, :], qs_s[g, 1, qrows, :]], axis=0)
        if first:
            krows = qrows
            bias = bias_ref[0, g, :, A_BLK:]
        else:
            krows = pl.ds((bi - 1) * A_BLK, 2 * A_BLK)
            bias = bias_ref[0, g]
        s = _nt_dot(q2, ks_s[g, krows, :]) + bias
        m = jnp.max(s, axis=-1, keepdims=True)
        p = jnp.exp2(s - m).astype(BF16)
        r = jnp.dot(p, vs_s[g, krows, :], preferred_element_type=F32)
        m_s[g, out_rows, :] = jnp.where(head0, m[:A_BLK], m[A_BLK:])
        l_s[g, out_rows, :] = jnp.where(head0, r[:A_BLK, LANES:], r[A_BLK:, LANES:])
        acc_s[g, out_rows, :] = jnp.where(head0, r[:A_BLK, :LANES], r[A_BLK:, :LANES])

    put(0, pl.ds(0, seq), q_ref[0] * A_QSCALE, k_ref[0], v_ref[0])
    for bi in range(seq // A_BLK):
        block(0, bi, bi == 0, pl.ds(bi * A_BLK, A_BLK))

    for r in range(d2):
        src = pl.ds(r, coarse, stride=d2)
        dst = pl.ds(r * coarse, coarse)
        q, k, v = q_ref[0, src, :] * A_QSCALE, k_ref[0, src, :], v_ref[0, src, :]
        x_s[0, dst, :], x_s[1, dst, :], x_s[2, dst, :] = q, k, v
        put(1, dst, q, k, v)
    nb = coarse // A_BLK
    for bi in range(d2 * nb):
        block(1, bi, bi % nb == 0, pl.ds(bi * A_BLK, A_BLK))

    fine = seq // d3
    nb = fine // A_BLK
    for r in range(d3):
        lo, hi = r % d2, r // d2
        src = pl.ds(lo * coarse + hi, fine, stride=d2)
        put(2, pl.ds(r * fine, fine), x_s[0, src, :], x_s[1, src, :], x_s[2, src, :])
    for bi in range(d3 * nb):
        r, n = divmod(bi, nb)
        lo, hi = r % d2, r // d2
        block(2, bi, n == 0, pl.ds(lo * coarse + hi + n * A_BLK * d2, A_BLK, stride=d2))

    for r in range(d2):
        rows = pl.ds(r * coarse, coarse)
        tok = pl.ds(r, coarse, stride=d2)
        m1, m2, m3 = m_s[0, tok, :], m_s[1, rows, :], m_s[2, rows, :]
        m = jnp.maximum(jnp.maximum(m1, m2), m3)
        e1, e2, e3 = jnp.exp2(m1 - m), jnp.exp2(m2 - m), jnp.exp2(m3 - m)
        num = e1 * acc_s[0, tok, :] + e2 * acc_s[1, rows, :] + e3 * acc_s[2, rows, :]
        den = e1 * l_s[0, tok, :] + e2 * l_s[1, rows, :] + e3 * l_s[2, rows, :]
        o_ref[0, tok, :] = num / den


def _dilated(pa, bias_a, batch, seq):
    npair = A_HEADS // PAIR
    nbr = len(A_BRANCHES)
    blk = lambda off: pl.BlockSpec((1, seq, LANES), lambda b, p: (b, 0, off + p))
    return pl.pallas_call(
        _dilated_kernel,
        grid=(batch, npair),
        in_specs=[
            blk(0), blk(npair), blk(2 * npair),
            pl.BlockSpec((1, nbr, PAIR * A_BLK, 2 * A_BLK), lambda b, p: (p, 0, 0, 0)),
        ],
        out_specs=pl.BlockSpec((1, seq, LANES), lambda b, p: (b, 0, p)),
        out_shape=jax.ShapeDtypeStruct((batch, seq, A_WIDTH), F32),
        scratch_shapes=[
            pltpu.VMEM((nbr, PAIR, seq, LANES), BF16),
            pltpu.VMEM((nbr, seq, LANES), BF16),
            pltpu.VMEM((nbr, seq, 2 * LANES), BF16),
            pltpu.VMEM((3, seq, LANES), F32),
            pltpu.VMEM((nbr, seq, LANES), F32),
            pltpu.VMEM((nbr, seq, LANES), F32),
            pltpu.VMEM((nbr, seq, LANES), F32),
        ],
        compiler_params=pltpu.CompilerParams(vmem_limit_bytes=VMEM_LIMIT),
        name="dilated_attn",
    )(pa, pa, pa, bias_a)


GLA_TILE = 128
GLA_LEVELS = 7
GLA_UNROLL = 4
PB_Q, PB_K, PB_V, PB_R, PB_A = 0, B_QK, 2 * B_QK, 2 * B_QK + B_WIDTH, 2 * B_QK + 2 * B_WIDTH
PB_WIDTH = PB_A + LANES


def _log_sigmoid(z):
    return jnp.minimum(z, 0.0) - jnp.log1p(jnp.exp(-jnp.abs(z)))


def _gla_constants():
    t = GLA_TILE
    i = np.arange(t)[:, None]
    c = np.arange(t)[None, :]
    blocks = [(c <= i).astype(np.float32)]
    for lvl in range(1, GLA_LEVELS + 1):
        w = t >> lvl
        split = (i // (2 * w)) * (2 * w) + w - 1
        blocks.append((c <= i).astype(np.float32) - (c <= split).astype(np.float32))
    x = np.arange(t)[:, None] ^ np.arange(t)[None, :]
    level_of_pair = np.where(np.arange(t)[None, :] < np.arange(t)[:, None],
                             np.floor(np.log2(np.maximum(x, 1))).astype(np.int32), -1)
    return np.concatenate(blocks, axis=0), np.tile(level_of_pair, (1, B_HEADS)).astype(np.int32)


def _gla_kernel(pb_ref, w2_ref, ba_ref, nw_ref, g_ref, lvl_ref, y_ref, st_s):
    seq = pb_ref.shape[1]
    t = GLA_TILE
    kl = lax.broadcasted_iota(jnp.int32, (B_QK, B_WIDTH), 0) // B_KEY_DIM
    vl = lax.broadcasted_iota(jnp.int32, (B_QK, B_WIDTH), 1) // B_VAL_DIM
    expand = jnp.where(kl == vl, 1.0, 0.0).astype(BF16)
    vlt = lax.broadcasted_iota(jnp.int32, (B_WIDTH, B_QK), 0) // B_VAL_DIM
    klt = lax.broadcasted_iota(jnp.int32, (B_WIDTH, B_QK), 1) // B_KEY_DIM
    head_t = vlt == klt
    va = (lax.broadcasted_iota(jnp.int32, (2 * B_WIDTH, B_WIDTH), 0) % B_WIDTH) // B_VAL_DIM
    vb = lax.broadcasted_iota(jnp.int32, (2 * B_WIDTH, B_WIDTH), 1) // B_VAL_DIM
    head_mean2 = jnp.where(va == vb, 1.0 / B_VAL_DIM, 0.0).astype(BF16)
    row = lax.broadcasted_iota(jnp.int32, (t, B_QK), 0)
    rep_head_k = lax.broadcasted_iota(jnp.int32, (B_HEADS * t, B_QK), 0) // t
    own_key_lanes = rep_head_k == lax.broadcasted_iota(jnp.int32, (B_HEADS * t, B_QK), 1) // B_KEY_DIM
    rep_head_v = lax.broadcasted_iota(jnp.int32, (B_HEADS * t, B_WIDTH), 0) // t
    own_val_lanes = rep_head_v == lax.broadcasted_iota(jnp.int32, (B_HEADS * t, B_WIDTH), 1) // B_VAL_DIM
    zero_k4 = jnp.zeros((B_HEADS * t, B_QK), BF16)
    zero_v4 = jnp.zeros((B_HEADS * t, B_WIDTH), BF16)

    st_s[...] = jnp.zeros_like(st_s)

    def tile(ti, carry):
        rows = pl.ds(pl.multiple_of(ti * t, t), t)
        z = jnp.dot(pb_ref[0, rows, PB_A:PB_A + LANES], w2_ref[...], precision=HIGHEST,
                    preferred_element_type=F32) + ba_ref[...]
        log_a = _log_sigmoid(z) * (1.0 / B_GATE_TAU)
        a1 = log_a.astype(BF16)
        a2 = (log_a - a1.astype(F32)).astype(BF16)
        dd = jnp.dot(g_ref[...], jnp.concatenate([a1, a2], axis=1), preferred_element_type=F32)
        dist = dd[:, :B_QK] + dd[:, B_QK:]
        b = dist[:t]
        b_last = b[t - 1:t, :]
        q = pb_ref[0, rows, PB_Q:PB_Q + B_QK] * (B_KEY_DIM ** -0.5)
        k = pb_ref[0, rows, PB_K:PB_K + B_QK]
        v = pb_ref[0, rows, PB_V:PB_V + B_WIDTH]
        v_bf = v.astype(BF16)
        state = st_s[...]
        o = _nt_dot((q * jnp.exp(b)).astype(BF16), state.astype(BF16))
        k_dec = (k * jnp.exp(b_last - b)).astype(BF16)
        u_t = jnp.dot(v.T.astype(BF16), k_dec, preferred_element_type=F32)
        st_s[...] = state * jnp.exp(b_last) + jnp.where(head_t, u_t, 0.0)
        o = o + jnp.dot((q * k).astype(BF16), expand, preferred_element_type=F32) * v
        attn = jnp.zeros((t, B_HEADS * t), F32)
        for lvl in range(1, GLA_LEVELS + 1):
            w = t >> lvl
            decay = jnp.exp(-jnp.abs(dist[lvl * t:(lvl + 1) * t]))
            after = (row & w) != 0
            q_l = (jnp.where(after, q * decay, 0.0)).astype(BF16)
            k_l = jnp.where(after, 0.0, k * decay).astype(BF16)
            k4 = jnp.where(own_key_lanes, jnp.concatenate([k_l] * B_HEADS, axis=0), zero_k4)
            attn = jnp.where(lvl_ref[...] == GLA_LEVELS - lvl, _nt_dot(q_l, k4), attn)
        v4 = jnp.where(own_val_lanes, jnp.concatenate([v_bf] * B_HEADS, axis=0), zero_v4)
        o = o + jnp.dot(attn.astype(BF16), v4, preferred_element_type=F32)
        sq = o * o
        sq_hi = sq.astype(BF16)
        sq_lo = (sq - sq_hi.astype(F32)).astype(BF16)
        ms = jnp.dot(jnp.concatenate([sq_hi, sq_lo], axis=1), head_mean2, preferred_element_type=F32)
        r = pb_ref[0, rows, PB_R:PB_R + B_WIDTH]
        silu = r / (1.0 + jnp.exp(-r))
        y_ref[0, rows, :] = o * lax.rsqrt(ms + EPS) * nw_ref[...] * silu
        return carry

    lax.fori_loop(0, seq // t, tile, 0, unroll=GLA_UNROLL)


def _gla(pb, w2p, b_a, norm_w, batch, seq):
    const = lambda b: (0, 0)
    g_all, lvl = _gla_constants()
    g_all = jnp.asarray(g_all, BF16)
    lvl = jnp.asarray(lvl)
    return pl.pallas_call(
        _gla_kernel,
        grid=(batch,),
        in_specs=[
            pl.BlockSpec((1, seq, PB_WIDTH), lambda b: (b, 0, 0)),
            pl.BlockSpec((LANES, B_QK), const),
            pl.BlockSpec((1, B_QK), const),
            pl.BlockSpec((1, B_WIDTH), const),
            pl.BlockSpec(g_all.shape, const),
            pl.BlockSpec(lvl.shape, const),
        ],
        out_specs=pl.BlockSpec((1, seq, B_WIDTH), lambda b: (b, 0, 0)),
        out_shape=jax.ShapeDtypeStruct((batch, seq, B_WIDTH), F32),
        scratch_shapes=[pltpu.VMEM((B_WIDTH, B_QK), F32)],
        compiler_params=pltpu.CompilerParams(vmem_limit_bytes=VMEM_LIMIT),
        name="gla",
    )(pb, w2p, b_a, norm_w, g_all, lvl)


def _moba_kernel(q_ref, k_ref, v_ref, bias_ref, o_ref, ka_s, qa_s, va_s):
    seq = q_ref.shape[1]
    nblk = seq // C_BLOCK
    head0 = lax.broadcasted_iota(jnp.int32, (C_BLOCK, LANES), 1) < HEAD_DIM

    k = k_ref[0]
    q = q_ref[0]
    ka_s[:, :LANES] = k.astype(BF16)
    row_blk = lax.broadcasted_iota(jnp.int32, (seq, LANES), 0) // C_BLOCK
    lane_id = lax.broadcasted_iota(jnp.int32, (seq, LANES), 1)
    ka_s[:, LANES:] = jnp.where(lane_id == row_blk, 1.0, 0.0).astype(BF16)
    va_s[:, :LANES] = v_ref[0].astype(BF16)
    va_s[:, LANES:] = jnp.ones((seq, LANES), BF16)
    k_mean = jnp.mean(k.reshape(nblk, C_BLOCK, LANES), axis=1)
    q_scaled = q * (HEAD_DIM ** -0.5 * LOG2E)
    blk = lax.broadcasted_iota(jnp.int32, (nblk, seq), 0)
    q_blk = lax.broadcasted_iota(jnp.int32, (nblk, seq), 1) // C_BLOCK
    for hh in range(PAIR):
        qa_s[hh, :, :LANES] = jnp.where((lane_id < HEAD_DIM) == (hh == 0), q_scaled, 0.0).astype(BF16)
        gate = _nt_dot(k_mean.astype(BF16), qa_s[hh, :, :LANES])
        rank = jnp.zeros((nblk, seq), jnp.int32)
        for m in range(nblk):
            gm = gate[m:m + 1, :]
            beats = (gm > gate) | ((gm == gate) & (m < blk))
            rank = rank + jnp.where(beats & (m < q_blk), 1, 0)
        dropped = (blk < q_blk) & (rank >= C_TOPK)
        pen = jnp.where(dropped, NEG_INF, 0.0)
        assert nblk == SUBLANES
        pen_t = jnp.concatenate([pen, jnp.zeros((LANES - nblk, seq), F32)], axis=0).T
        qa_s[hh, :, LANES:] = pen_t.astype(BF16)

    def logits(qb):
        qrows = slice(qb * C_BLOCK, (qb + 1) * C_BLOCK)
        q2 = jnp.concatenate([qa_s[0, qrows, :], qa_s[1, qrows, :]], axis=0)
        return _nt_dot(q2, ka_s[:(qb + 1) * C_BLOCK, :]) + bias_ref[0, :, (nblk - 1 - qb) * C_BLOCK:]

    s_next = logits(0)
    for qb in range(nblk):
        s = s_next
        if qb + 1 < nblk:
            s_next = logits(qb + 1)
        m = jnp.max(s, axis=-1, keepdims=True)
        p = jnp.exp2(s - m).astype(BF16)
        r = jnp.dot(p, va_s[:(qb + 1) * C_BLOCK, :], preferred_element_type=F32)
        o_ref[0, qb * C_BLOCK:(qb + 1) * C_BLOCK, :] = jnp.where(
            head0, r[:C_BLOCK, :LANES] / r[:C_BLOCK, LANES:], r[C_BLOCK:, :LANES] / r[C_BLOCK:, LANES:])


def _moba(pc, bias_c, batch, seq):
    npair = C_HEADS // PAIR
    blk = lambda off: pl.BlockSpec((1, seq, LANES), lambda b, p: (b, 0, off + p))
    return pl.pallas_call(
        _moba_kernel,
        grid=(batch, npair),
        in_specs=[
            blk(0), blk(npair), blk(2 * npair),
            pl.BlockSpec((1, PAIR * C_BLOCK, seq), lambda b, p: (p, 0, 0)),
        ],
        out_specs=pl.BlockSpec((1, seq, LANES), lambda b, p: (b, 0, p)),
        out_shape=jax.ShapeDtypeStruct((batch, seq, C_WIDTH), F32),
        scratch_shapes=[
            pltpu.VMEM((seq, 2 * LANES), BF16),
            pltpu.VMEM((PAIR, seq, 2 * LANES), BF16),
            pltpu.VMEM((seq, 2 * LANES), BF16),
        ],
        compiler_params=pltpu.CompilerParams(vmem_limit_bytes=VMEM_LIMIT),
        name="moba_attn",
    )(pc, pc, pc, bias_c)


FF_CHUNK = 1024


def _outffn_kernel(x_ref, ya_ref, yb_ref, yc_ref, woa_ref, wob_ref, woc_ref, n2_ref,
                   w1_ref, w2_ref, fn_ref, o_ref, *, final):
    x = x_ref[...]
    x = x + jnp.dot(ya_ref[...].astype(BF16), woa_ref[...], preferred_element_type=F32)
    x = x + jnp.dot(yb_ref[...].astype(BF16), wob_ref[...], preferred_element_type=F32)
    x = x + jnp.dot(yc_ref[...].astype(BF16), woc_ref[...], preferred_element_type=F32)
    h = _rms(x, n2_ref[...]).astype(BF16)
    ffn = None
    for j in range(D_FF // FF_CHUNK):
        cols = slice(j * FF_CHUNK, (j + 1) * FF_CHUNK)
        a = jnp.dot(h, w1_ref[:, cols], preferred_element_type=F32)
        a = jnp.square(jnp.maximum(a, 0.0)).astype(BF16)
        part = jnp.dot(a, w2_ref[cols, :], preferred_element_type=F32)
        ffn = part if ffn is None else ffn + part
    x = x + ffn
    if final:
        x = _rms(x, fn_ref[...])
    o_ref[...] = x


def _outffn(x2d, ya, yb, yc, woa, wob, woc, n2, w1, w2, fn, tm, final):
    m = x2d.shape[0]
    const = lambda i: (0, 0)
    row = lambda i: (i, 0)
    full = lambda a: pl.BlockSpec(a.shape, const)
    return pl.pallas_call(
        functools.partial(_outffn_kernel, final=final),
        grid=(m // tm,),
        in_specs=[
            pl.BlockSpec((tm, D_MODEL), row),
            pl.BlockSpec((tm, ya.shape[1]), row),
            pl.BlockSpec((tm, yb.shape[1]), row),
            pl.BlockSpec((tm, yc.shape[1]), row),
            full(woa), full(wob), full(woc), full(n2), full(w1), full(w2), full(fn),
        ],
        out_specs=pl.BlockSpec((tm, D_MODEL), row),
        out_shape=jax.ShapeDtypeStruct((m, D_MODEL), F32),
        compiler_params=pltpu.CompilerParams(vmem_limit_bytes=VMEM_LIMIT),
        name="outproj_ffn",
    )(x2d, ya, yb, yc, woa, wob, woc, n2, w1, w2, fn)


def _split_w_in(w):
    sizes = (A_WIDTH, A_WIDTH, A_WIDTH, B_QK, B_QK, B_WIDTH, B_WIDTH, B_GATE_RANK, C_WIDTH, C_WIDTH, C_WIDTH)
    offs = np.concatenate([[0], np.cumsum(sizes)])
    aq, ak, av, bq, bk, bv, br, ba, cq, ck, cv = [w[:, offs[i]:offs[i + 1]] for i in range(len(sizes))]
    ba = jnp.pad(ba, ((0, 0), (0, LANES - B_GATE_RANK)))
    cat = lambda parts: jnp.concatenate(parts, axis=1).astype(BF16)
    return cat([aq, ak, av]), cat([cq, ck, cv]), cat([bq, bk, bv, br, ba])


def kernel(x, norm1_w, w_in, gla_w_a2, gla_b_a, gla_norm_w, w_out, norm2_w, w_ff1, w_ff2, rel_bias, final_norm_w):
    batch, seq, d = x.shape
    depth = w_in.shape[0]
    assert d == D_MODEL and seq % C_BLOCK == 0 and seq % (A_BLK * A_BRANCHES[-1][1]) == 0
    tokens = batch * seq
    tm = 512 if tokens % 512 == 0 else seq
    bias_a, bias_c = _bias_tiles(rel_bias, seq)
    x2d = x.reshape(tokens, d)
    fn = final_norm_w.reshape(1, d)
    for i in range(depth):
        wa, wc, wb = _split_w_in(w_in[i])
        pa, pc, pb = _inproj(x2d, norm1_w[i].reshape(1, d), wa, wc, wb, tm)
        ya = _dilated(pa.reshape(batch, seq, -1), bias_a, batch, seq)
        w2p = jnp.pad(gla_w_a2[i], ((0, LANES - B_GATE_RANK), (0, 0)))
        yb = _gla(pb.reshape(batch, seq, -1), w2p, gla_b_a[i].reshape(1, B_QK),
                  gla_norm_w[i].reshape(1, B_WIDTH), batch, seq)
        yc = _moba(pc.reshape(batch, seq, -1), bias_c, batch, seq)
        wo = w_out[i].astype(BF16)
        x2d = _outffn(
            x2d, ya.reshape(tokens, -1), yb.reshape(tokens, -1), yc.reshape(tokens, -1),
            wo[:A_WIDTH], wo[A_WIDTH:A_WIDTH + B_WIDTH], wo[A_WIDTH + B_WIDTH:],
            norm2_w[i].reshape(1, d), w_ff1[i].astype(BF16), w_ff2[i].astype(BF16), fn,
            tm, final=(i == depth - 1))
    return x2d.reshape(batch, seq, d)
```

```python
import functools
import math

import numpy as np
import jax
import jax.numpy as jnp
from jax import lax
from jax.experimental import pallas as pl
from jax.experimental.pallas import tpu as pltpu

D_MODEL = 1024
HEAD_DIM = 64
A_HEADS = 6
A_BRANCHES = ((128, 1), (512, 4), (2048, 16))
B_HEADS = 4
B_KEY_DIM = 32
B_VAL_DIM = 64
B_GATE_RANK = 16
B_GATE_TAU = 16.0
C_HEADS = 6
C_BLOCK = 256
C_TOPK = 3
REL_BUCKETS = 32
REL_MAX_DIST = 2048
D_FF = 4 * D_MODEL
EPS = 1e-6
NEG_INF = -1e30
LOG2E = math.log2(math.e)

A_WIDTH = A_HEADS * HEAD_DIM
B_QK = B_HEADS * B_KEY_DIM
B_WIDTH = B_HEADS * B_VAL_DIM
C_WIDTH = C_HEADS * HEAD_DIM

LANES = 128
SUBLANES = 8
A_BLK = 128
PAIR = LANES // HEAD_DIM
VMEM_LIMIT = 56 * 1024 * 1024
F32 = jnp.float32
BF16 = jnp.bfloat16
HIGHEST = lax.Precision.HIGHEST


def _bucket_thresholds():
    n = np.arange(0, REL_MAX_DIST + 1)
    exact = REL_BUCKETS // 2
    logv = (np.log(np.maximum(n, 1).astype(np.float32) / np.float32(exact))
            / np.float32(math.log(REL_MAX_DIST / exact))).astype(np.float32)
    large = np.minimum(exact + (logv * np.float32(REL_BUCKETS - exact)).astype(np.int32), REL_BUCKETS - 1)
    bucket = np.where(n < exact, n, large)
    assert np.all(np.diff(bucket) >= 0)
    return [int(np.argmax(bucket >= k)) for k in range(REL_BUCKETS)]


_THRESH = _bucket_thresholds()


def _nt_dot(a, b):
    return lax.dot_general(a, b, (((1,), (1,)), ((), ())), preferred_element_type=F32)


def _rms(x, w):
    return x * lax.rsqrt(jnp.mean(x * x, axis=-1, keepdims=True) + EPS) * w


def _bias_lookup(dist, rb_ref, col):
    val = jnp.full(dist.shape, rb_ref[0, col], F32)
    for k in range(1, REL_BUCKETS):
        val = jnp.where(dist >= _THRESH[k], rb_ref[k, col], val)
    return val


def _bias_kernel(rb_ref, ba_ref, bc_ref):
    h = pl.program_id(0)
    qi = lax.broadcasted_iota(jnp.int32, (A_BLK, 2 * A_BLK), 0)
    kj = lax.broadcasted_iota(jnp.int32, (A_BLK, 2 * A_BLK), 1)
    steps = qi + A_BLK - kj
    band = (steps >= 0) & (steps <= A_BLK)
    for g, (_, dil) in enumerate(A_BRANCHES):
        ba_ref[0, g] = jnp.where(band, _bias_lookup(steps * dil, rb_ref, h) * LOG2E, NEG_INF)
    qi = lax.broadcasted_iota(jnp.int32, (C_BLOCK, C_BLOCK), 0)
    kj = lax.broadcasted_iota(jnp.int32, (C_BLOCK, C_BLOCK), 1)
    nblk = bc_ref.shape[2] // C_BLOCK
    for delta in range(nblk):
        dist = delta * C_BLOCK + qi - kj
        bias = _bias_lookup(dist, rb_ref, A_HEADS + h) * LOG2E
        if delta == 0:
            bias = jnp.where(dist >= 0, bias, NEG_INF)
        bc_ref[0, :, (nblk - 1 - delta) * C_BLOCK:(nblk - delta) * C_BLOCK] = bias


def _bias_tiles(rel_bias, seq):
    return pl.pallas_call(
        _bias_kernel,
        grid=(A_HEADS,),
        in_specs=[pl.BlockSpec(memory_space=pltpu.SMEM)],
        out_specs=[
            pl.BlockSpec((1, len(A_BRANCHES), A_BLK, 2 * A_BLK), lambda h: (h // PAIR, 0, h % PAIR, 0)),
            pl.BlockSpec((1, C_BLOCK, seq), lambda h: (h // PAIR, h % PAIR, 0)),
        ],
        out_shape=[
            jax.ShapeDtypeStruct((A_HEADS // PAIR, len(A_BRANCHES), PAIR * A_BLK, 2 * A_BLK), F32),
            jax.ShapeDtypeStruct((C_HEADS // PAIR, PAIR * C_BLOCK, seq), F32),
        ],
        name="bias_tiles",
    )(rel_bias)


def _inproj_kernel(x_ref, nw_ref, wa_ref, wc_ref, wb_ref, pa_ref, pc_ref, pb_ref):
    h = _rms(x_ref[...], nw_ref[...]).astype(BF16)
    pa_ref[...] = jnp.dot(h, wa_ref[...], preferred_element_type=F32)
    pc_ref[...] = jnp.dot(h, wc_ref[...], preferred_element_type=F32)
    pb_ref[...] = jnp.dot(h, wb_ref[...], preferred_element_type=F32)


def _inproj(x2d, nw, wa, wc, wb, tm):
    m = x2d.shape[0]
    const = lambda i: (0, 0)
    row = lambda i: (i, 0)
    return pl.pallas_call(
        _inproj_kernel,
        grid=(m // tm,),
        in_specs=[
            pl.BlockSpec((tm, D_MODEL), row),
            pl.BlockSpec((1, D_MODEL), const),
            pl.BlockSpec(wa.shape, const),
            pl.BlockSpec(wc.shape, const),
            pl.BlockSpec(wb.shape, const),
        ],
        out_specs=[
            pl.BlockSpec((tm, wa.shape[1]), row),
            pl.BlockSpec((tm, wc.shape[1]), row),
            pl.BlockSpec((tm, wb.shape[1]), row),
        ],
        out_shape=[
            jax.ShapeDtypeStruct((m, wa.shape[1]), F32),
            jax.ShapeDtypeStruct((m, wc.shape[1]), F32),
            jax.ShapeDtypeStruct((m, wb.shape[1]), F32),
        ],
        compiler_params=pltpu.CompilerParams(vmem_limit_bytes=VMEM_LIMIT),
        name="inproj",
    )(x2d, nw, wa, wc, wb)


A_QSCALE = HEAD_DIM ** -0.5 * LOG2E
A_AHEAD = 4


def _dilated_kernel(q_ref, k_ref, v_ref, bias_ref, o_ref, qs_s, ks_s, vs_s, x_s, m_s, l_s, acc_s):
    seq = q_ref.shape[1]
    (w1, d1), (w2, d2), (w3, d3) = A_BRANCHES
    assert d1 == 1 and d3 == d2 * d2 and all(w // d == A_BLK for w, d in A_BRANCHES)
    coarse = seq // d2
    lane = lax.broadcasted_iota(jnp.int32, (A_BLK, LANES), 1)
    head0 = lane < HEAD_DIM
    vs_s[:, :, LANES:] = jnp.ones((len(A_BRANCHES), seq, LANES), BF16)

    def put(g, dst, q, k, v):
        lane_l = lax.broadcasted_iota(jnp.int32, q.shape, 1)
        qs_s[g, 0, dst, :] = jnp.where(lane_l < HEAD_DIM, q, 0.0).astype(BF16)
        qs_s[g, 1, dst, :] = jnp.where(lane_l < HEAD_DIM, 0.0, q).astype(BF16)
        ks_s[g, dst, :] = k.astype(BF16)
        vs_s[g, dst, :LANES] = v.astype(BF16)

    def logits(g, bi, first):
        qrows = pl.ds(bi * A_BLK, A_BLK)
        q2 = jnp.concatenate([qs_s[g, 0, qrows, :], qs_s[g, 1, qrows, :]], axis=0)
        if first:
            krows = qrows
            bias = bias_ref[0, g, :, A_BLK:]
        else:
            krows = pl.ds((bi - 1) * A_BLK, 2 * A_BLK)
            bias = bias_ref[0, g]
        return _nt_dot(q2, ks_s[g, krows, :]) + bias, krows

    def finish(g, s, krows, out_rows):
        m = jnp.max(s, axis=-1, keepdims=True)
        p = jnp.exp2(s - m).astype(BF16)
        r = jnp.dot(p, vs_s[g, krows, :], preferred_element_type=F32)
        m_s[g, out_rows, :] = jnp.where(head0, m[:A_BLK], m[A_BLK:])
        l_s[g, out_rows, :] = jnp.where(head0, r[:A_BLK, LANES:], r[A_BLK:, LANES:])
        acc_s[g, out_rows, :] = jnp.where(head0, r[:A_BLK, :LANES], r[A_BLK:, :LANES])

    def run(g, blocks):
        pending = []
        for bi, first, out_rows in blocks:
            pending.append((logits(g, bi, first), out_rows))
            if len(pending) > A_AHEAD:
                (s, krows), rows = pending.pop(0)
                finish(g, s, krows, rows)
        for (s, krows), rows in pending:
            finish(g, s, krows, rows)

    put(0, pl.ds(0, seq), q_ref[0] * A_QSCALE, k_ref[0], v_ref[0])
    run(0, [(bi, bi == 0, pl.ds(bi * A_BLK, A_BLK)) for bi in range(seq // A_BLK)])

    for r in range(d2):
        src = pl.ds(r, coarse, stride=d2)
        dst = pl.ds(r * coarse, coarse)
        q, k, v = q_ref[0, src, :] * A_QSCALE, k_ref[0, src, :], v_ref[0, src, :]
        x_s[0, dst, :], x_s[1, dst, :], x_s[2, dst, :] = q, k, v
        put(1, dst, q, k, v)
    nb = coarse // A_BLK
    run(1, [(bi, bi % nb == 0, pl.ds(bi * A_BLK, A_BLK)) for bi in range(d2 * nb)])

    fine = seq // d3
    nb = fine // A_BLK
    for r in range(d3):
        lo, hi = r % d2, r // d2
        src = pl.ds(lo * coarse + hi, fine, stride=d2)
        put(2, pl.ds(r * fine, fine), x_s[0, src, :], x_s[1, src, :], x_s[2, src, :])
    run(2, [(r * nb + n, n == 0, pl.ds((r % d2) * coarse + r // d2 + n * A_BLK * d2, A_BLK, stride=d2))
            for r in range(d3) for n in range(nb)])

    for r in range(d2):
        rows = pl.ds(r * coarse, coarse)
        tok = pl.ds(r, coarse, stride=d2)
        m1, m2, m3 = m_s[0, tok, :], m_s[1, rows, :], m_s[2, rows, :]
        m = jnp.maximum(jnp.maximum(m1, m2), m3)
        e1, e2, e3 = jnp.exp2(m1 - m), jnp.exp2(m2 - m), jnp.exp2(m3 - m)
        num = e1 * acc_s[0, tok, :] + e2 * acc_s[1, rows, :] + e3 * acc_s[2, rows, :]
        den = e1 * l_s[0, tok, :] + e2 * l_s[1, rows, :] + e3 * l_s[2, rows, :]
        o_ref[0, tok, :] = num / den


def _dilated(pa, bias_a, batch, seq):
    npair = A_HEADS // PAIR
    nbr = len(A_BRANCHES)
    blk = lambda off: pl.BlockSpec((1, seq, LANES), lambda b, p: (b, 0, off + p))
    return pl.pallas_call(
        _dilated_kernel,
        grid=(batch, npair),
        in_specs=[
            blk(0), blk(npair), blk(2 * npair),
            pl.BlockSpec((1, nbr, PAIR * A_BLK, 2 * A_BLK), lambda b, p: (p, 0, 0, 0)),
        ],
        out_specs=pl.BlockSpec((1, seq, LANES), lambda b, p: (b, 0, p)),
        out_shape=jax.ShapeDtypeStruct((batch, seq, A_WIDTH), F32),
        scratch_shapes=[
            pltpu.VMEM((nbr, PAIR, seq, LANES), BF16),
            pltpu.VMEM((nbr, seq, LANES), BF16),
            pltpu.VMEM((nbr, seq, 2 * LANES), BF16),
            pltpu.VMEM((3, seq, LANES), F32),
            pltpu.VMEM((nbr, seq, LANES), F32),
            pltpu.VMEM((nbr, seq, LANES), F32),
            pltpu.VMEM((nbr, seq, LANES), F32),
        ],
        compiler_params=pltpu.CompilerParams(vmem_limit_bytes=VMEM_LIMIT),
        name="dilated_attn",
    )(pa, pa, pa, bias_a)


GLA_TILE = 128
GLA_LEVELS = GLA_TILE.bit_length() - 1
GLA_GROUP = 16
PB_Q, PB_K, PB_V, PB_R, PB_A = 0, B_QK, 2 * B_QK, 2 * B_QK + B_WIDTH, 2 * B_QK + 2 * B_WIDTH
PB_WIDTH = PB_A + LANES


def _log_sigmoid(z):
    return jnp.minimum(z, 0.0) - jnp.log1p(jnp.exp(-jnp.abs(z)))


def _gla_constants():
    t = GLA_TILE
    i = np.arange(t)[:, None]
    c = np.arange(t)[None, :]
    blocks = [(c <= i).astype(np.float32)]
    for lvl in range(1, GLA_LEVELS + 1):
        w = t >> lvl
        split = (i // (2 * w)) * (2 * w) + w - 1
        blocks.append((c <= i).astype(np.float32) - (c <= split).astype(np.float32))
    x = np.arange(t)[:, None] ^ np.arange(t)[None, :]
    level_of_pair = np.where(np.arange(t)[None, :] < np.arange(t)[:, None],
                             np.floor(np.log2(np.maximum(x, 1))).astype(np.int32), -1)
    return np.concatenate(blocks, axis=0), np.tile(level_of_pair, (1, B_HEADS)).astype(np.int32)


def _gla_kernel(pb_ref, w2_ref, ba_ref, nw_ref, g_ref, lvl_ref, y_ref):
    seq = pb_ref.shape[1]
    t = GLA_TILE
    kl = lax.broadcasted_iota(jnp.int32, (B_QK, B_WIDTH), 0) // B_KEY_DIM
    vl = lax.broadcasted_iota(jnp.int32, (B_QK, B_WIDTH), 1) // B_VAL_DIM
    expand = jnp.where(kl == vl, 1.0, 0.0).astype(BF16)
    vlt = lax.broadcasted_iota(jnp.int32, (B_WIDTH, B_QK), 0) // B_VAL_DIM
    klt = lax.broadcasted_iota(jnp.int32, (B_WIDTH, B_QK), 1) // B_KEY_DIM
    head_t = vlt == klt
    va = (lax.broadcasted_iota(jnp.int32, (2 * B_WIDTH, B_WIDTH), 0) % B_WIDTH) // B_VAL_DIM
    vb = lax.broadcasted_iota(jnp.int32, (2 * B_WIDTH, B_WIDTH), 1) // B_VAL_DIM
    head_mean2 = jnp.where(va == vb, 1.0 / B_VAL_DIM, 0.0).astype(BF16)
    row = lax.broadcasted_iota(jnp.int32, (t, B_QK), 0)
    rep_head_k = lax.broadcasted_iota(jnp.int32, (B_HEADS * t, B_QK), 0) // t
    own_key_lanes = rep_head_k == lax.broadcasted_iota(jnp.int32, (B_HEADS * t, B_QK), 1) // B_KEY_DIM
    rep_head_v = lax.broadcasted_iota(jnp.int32, (B_HEADS * t, B_WIDTH), 0) // t
    own_val_lanes = rep_head_v == lax.broadcasted_iota(jnp.int32, (B_HEADS * t, B_WIDTH), 1) // B_VAL_DIM
    zero_k4 = jnp.zeros((B_HEADS * t, B_QK), BF16)
    zero_v4 = jnp.zeros((B_HEADS * t, B_WIDTH), BF16)

    def decays(ti):
        rows = pl.ds(pl.multiple_of(ti * t, t), t)
        z = jnp.dot(pb_ref[0, rows, PB_A:PB_A + LANES], w2_ref[...], precision=HIGHEST,
                    preferred_element_type=F32) + ba_ref[...]
        log_a = _log_sigmoid(z) * (LOG2E / B_GATE_TAU)
        a1 = log_a.astype(BF16)
        a2 = (log_a - a1.astype(F32)).astype(BF16)
        dd = jnp.dot(g_ref[...], jnp.concatenate([a1, a2], axis=1), preferred_element_type=F32)
        return dd[:, :B_QK] + dd[:, B_QK:]

    def within(ti, dist):
        rows = pl.ds(pl.multiple_of(ti * t, t), t)
        b = dist[:t]
        b_last = b[t - 1:t, :]
        q = pb_ref[0, rows, PB_Q:PB_Q + B_QK] * (B_KEY_DIM ** -0.5)
        k = pb_ref[0, rows, PB_K:PB_K + B_QK]
        v = pb_ref[0, rows, PB_V:PB_V + B_WIDTH]
        v_bf = v.astype(BF16)
        o = jnp.dot((q * k).astype(BF16), expand, preferred_element_type=F32) * v
        attn = jnp.zeros((t, B_HEADS * t), F32)
        for lvl in range(1, GLA_LEVELS + 1):
            w = t >> lvl
            decay = jnp.exp2(-jnp.abs(dist[lvl * t:(lvl + 1) * t]))
            after = (row & w) != 0
            q_l = (jnp.where(after, q * decay, 0.0)).astype(BF16)
            k_l = jnp.where(after, 0.0, k * decay).astype(BF16)
            k4 = jnp.where(own_key_lanes, jnp.concatenate([k_l] * B_HEADS, axis=0), zero_k4)
            attn = jnp.where(lvl_ref[...] == GLA_LEVELS - lvl, _nt_dot(q_l, k4), attn)
        v4 = jnp.where(own_val_lanes, jnp.concatenate([v_bf] * B_HEADS, axis=0), zero_v4)
        o = o + jnp.dot(attn.astype(BF16), v4, preferred_element_type=F32)
        q_dec = (q * jnp.exp2(b)).astype(BF16)
        k_dec = (k * jnp.exp2(b_last - b)).astype(BF16)
        u_t = jnp.dot(v.T.astype(BF16), k_dec, preferred_element_type=F32)
        return o, q_dec, jnp.where(head_t, u_t, 0.0), jnp.exp2(b_last)

    def finish(ti, o):
        rows = pl.ds(pl.multiple_of(ti * t, t), t)
        sq = o * o
        sq_hi = sq.astype(BF16)
        sq_lo = (sq - sq_hi.astype(F32)).astype(BF16)
        ms = jnp.dot(jnp.concatenate([sq_hi, sq_lo], axis=1), head_mean2, preferred_element_type=F32)
        r = pb_ref[0, rows, PB_R:PB_R + B_WIDTH]
        silu = r / (1.0 + jnp.exp(-r))
        y_ref[0, rows, :] = o * lax.rsqrt(ms + EPS) * nw_ref[...] * silu

    def group(gi, state):
        tiles = [gi * GLA_GROUP + j for j in range(GLA_GROUP)]
        dists = [decays(ti) for ti in tiles]
        parts = [within(ti, dist) for ti, dist in zip(tiles, dists)]
        for ti, (o, q_dec, u_t, gate) in zip(tiles, parts):
            finish(ti, o + _nt_dot(q_dec, state.astype(BF16)))
            state = state * gate + u_t
        return state

    lax.fori_loop(0, seq // (t * GLA_GROUP), group, jnp.zeros((B_WIDTH, B_QK), F32))


def _gla(pb, w2p, b_a, norm_w, batch, seq):
    const = lambda b: (0, 0)
    g_all, lvl = _gla_constants()
    g_all = jnp.asarray(g_all, BF16)
    lvl = jnp.asarray(lvl)
    return pl.pallas_call(
        _gla_kernel,
        grid=(batch,),
        in_specs=[
            pl.BlockSpec((1, seq, PB_WIDTH), lambda b: (b, 0, 0)),
            pl.BlockSpec((LANES, B_QK), const),
            pl.BlockSpec((1, B_QK), const),
            pl.BlockSpec((1, B_WIDTH), const),
            pl.BlockSpec(g_all.shape, const),
            pl.BlockSpec(lvl.shape, const),
        ],
        out_specs=pl.BlockSpec((1, seq, B_WIDTH), lambda b: (b, 0, 0)),
        out_shape=jax.ShapeDtypeStruct((batch, seq, B_WIDTH), F32),
        compiler_params=pltpu.CompilerParams(vmem_limit_bytes=VMEM_LIMIT),
        name="gla",
    )(pb, w2p, b_a, norm_w, g_all, lvl)


C_AHEAD = 2


def _moba_kernel(q_ref, k_ref, v_ref, bias_ref, o_ref, ka_s, qa_s, va_s):
    seq = q_ref.shape[1]
    nblk = seq // C_BLOCK
    head0 = lax.broadcasted_iota(jnp.int32, (C_BLOCK, LANES), 1) < HEAD_DIM

    k = k_ref[0]
    q = q_ref[0]
    ka_s[:, :LANES] = k.astype(BF16)
    row_blk = lax.broadcasted_iota(jnp.int32, (seq, LANES), 0) // C_BLOCK
    lane_id = lax.broadcasted_iota(jnp.int32, (seq, LANES), 1)
    ka_s[:, LANES:] = jnp.where(lane_id == row_blk, 1.0, 0.0).astype(BF16)
    va_s[:, :LANES] = v_ref[0].astype(BF16)
    va_s[:, LANES:] = jnp.ones((seq, LANES), BF16)
    k_mean = jnp.mean(k.reshape(nblk, C_BLOCK, LANES), axis=1)
    q_scaled = q * (HEAD_DIM ** -0.5 * LOG2E)
    blk = lax.broadcasted_iota(jnp.int32, (nblk, seq), 0)
    q_blk = lax.broadcasted_iota(jnp.int32, (nblk, seq), 1) // C_BLOCK
    for hh in range(PAIR):
        qa_s[hh, :, :LANES] = jnp.where((lane_id < HEAD_DIM) == (hh == 0), q_scaled, 0.0).astype(BF16)
        gate = _nt_dot(k_mean.astype(BF16), qa_s[hh, :, :LANES])
        rank = jnp.zeros((nblk, seq), jnp.int32)
        for m in range(nblk):
            gm = gate[m:m + 1, :]
            beats = (gm > gate) | ((gm == gate) & (m < blk))
            rank = rank + jnp.where(beats & (m < q_blk), 1, 0)
        dropped = (blk < q_blk) & (rank >= C_TOPK)
        pen = jnp.where(dropped, NEG_INF, 0.0)
        assert nblk == SUBLANES
        pen_t = jnp.concatenate([pen, jnp.zeros((LANES - nblk, seq), F32)], axis=0).T
        qa_s[hh, :, LANES:] = pen_t.astype(BF16)

    def logits(qb):
        qrows = slice(qb * C_BLOCK, (qb + 1) * C_BLOCK)
        q2 = jnp.concatenate([qa_s[0, qrows, :], qa_s[1, qrows, :]], axis=0)
        return _nt_dot(q2, ka_s[:(qb + 1) * C_BLOCK, :]) + bias_ref[0, :, (nblk - 1 - qb) * C_BLOCK:]

    def finish(qb, s):
        m = jnp.max(s, axis=-1, keepdims=True)
        p = jnp.exp2(s - m).astype(BF16)
        r = jnp.dot(p, va_s[:(qb + 1) * C_BLOCK, :], preferred_element_type=F32)
        o_ref[0, qb * C_BLOCK:(qb + 1) * C_BLOCK, :] = jnp.where(
            head0, r[:C_BLOCK, :LANES] / r[:C_BLOCK, LANES:], r[C_BLOCK:, :LANES] / r[C_BLOCK:, LANES:])

    pending = []
    for qb in reversed(range(nblk)):
        pending.append((qb, logits(qb)))
        if len(pending) > C_AHEAD:
            finish(*pending.pop(0))
    for item in pending:
        finish(*item)


def _moba(pc, bias_c, batch, seq):
    npair = C_HEADS // PAIR
    blk = lambda off: pl.BlockSpec((1, seq, LANES), lambda b, p: (b, 0, off + p))
    return pl.pallas_call(
        _moba_kernel,
        grid=(batch, npair),
        in_specs=[
            blk(0), blk(npair), blk(2 * npair),
            pl.BlockSpec((1, PAIR * C_BLOCK, seq), lambda b, p: (p, 0, 0)),
        ],
        out_specs=pl.BlockSpec((1, seq, LANES), lambda b, p: (b, 0, p)),
        out_shape=jax.ShapeDtypeStruct((batch, seq, C_WIDTH), F32),
        scratch_shapes=[
            pltpu.VMEM((seq, 2 * LANES), BF16),
            pltpu.VMEM((PAIR, seq, 2 * LANES), BF16),
            pltpu.VMEM((seq, 2 * LANES), BF16),
        ],
        compiler_params=pltpu.CompilerParams(vmem_limit_bytes=VMEM_LIMIT),
        name="moba_attn",
    )(pc, pc, pc, bias_c)


FF_CHUNK = 1024


def _outffn_kernel(x_ref, ya_ref, yb_ref, yc_ref, woa_ref, wob_ref, woc_ref, n2_ref,
                   w1_ref, w2_ref, fn_ref, o_ref, *, final):
    x = x_ref[...]
    x = x + jnp.dot(ya_ref[...].astype(BF16), woa_ref[...], preferred_element_type=F32)
    x = x + jnp.dot(yb_ref[...].astype(BF16), wob_ref[...], preferred_element_type=F32)
    x = x + jnp.dot(yc_ref[...].astype(BF16), woc_ref[...], preferred_element_type=F32)
    h = _rms(x, n2_ref[...]).astype(BF16)
    ffn = None
    for j in range(D_FF // FF_CHUNK):
        cols = slice(j * FF_CHUNK, (j + 1) * FF_CHUNK)
        a = jnp.dot(h, w1_ref[:, cols], preferred_element_type=F32)
        a = jnp.square(jnp.maximum(a, 0.0)).astype(BF16)
        part = jnp.dot(a, w2_ref[cols, :], preferred_element_type=F32)
        ffn = part if ffn is None else ffn + part
    x = x + ffn
    if final:
        x = _rms(x, fn_ref[...])
    o_ref[...] = x


def _outffn(x2d, ya, yb, yc, woa, wob, woc, n2, w1, w2, fn, tm, final):
    m = x2d.shape[0]
    const = lambda i: (0, 0)
    row = lambda i: (i, 0)
    full = lambda a: pl.BlockSpec(a.shape, const)
    return pl.pallas_call(
        functools.partial(_outffn_kernel, final=final),
        grid=(m // tm,),
        in_specs=[
            pl.BlockSpec((tm, D_MODEL), row),
            pl.BlockSpec((tm, ya.shape[1]), row),
            pl.BlockSpec((tm, yb.shape[1]), row),
            pl.BlockSpec((tm, yc.shape[1]), row),
            full(woa), full(wob), full(woc), full(n2), full(w1), full(w2), full(fn),
        ],
        out_specs=pl.BlockSpec((tm, D_MODEL), row),
        out_shape=jax.ShapeDtypeStruct((m, D_MODEL), F32),
        compiler_params=pltpu.CompilerParams(vmem_limit_bytes=VMEM_LIMIT),
        name="outproj_ffn",
    )(x2d, ya, yb, yc, woa, wob, woc, n2, w1, w2, fn)


def _split_w_in(w):
    sizes = (A_WIDTH, A_WIDTH, A_WIDTH, B_QK, B_QK, B_WIDTH, B_WIDTH, B_GATE_RANK, C_WIDTH, C_WIDTH, C_WIDTH)
    offs = np.concatenate([[0], np.cumsum(sizes)])
    aq, ak, av, bq, bk, bv, br, ba, cq, ck, cv = [w[:, offs[i]:offs[i + 1]] for i in range(len(sizes))]
    ba = jnp.pad(ba, ((0, 0), (0, LANES - B_GATE_RANK)))
    cat = lambda parts: jnp.concatenate(parts, axis=1).astype(BF16)
    return cat([aq, ak, av]), cat([cq, ck, cv]), cat([bq, bk, bv, br, ba])


def kernel(x, norm1_w, w_in, gla_w_a2, gla_b_a, gla_norm_w, w_out, norm2_w, w_ff1, w_ff2, rel_bias, final_norm_w):
    batch, seq, d = x.shape
    depth = w_in.shape[0]
    assert d == D_MODEL and seq % C_BLOCK == 0 and seq % (A_BLK * A_BRANCHES[-1][1]) == 0
    tokens = batch * seq
    tm = 512 if tokens % 512 == 0 else seq
    bias_a, bias_c = _bias_tiles(rel_bias, seq)
    x2d = x.reshape(tokens, d)
    fn = final_norm_w.reshape(1, d)
    for i in range(depth):
        wa, wc, wb = _split_w_in(w_in[i])
        pa, pc, pb = _inproj(x2d, norm1_w[i].reshape(1, d), wa, wc, wb, tm)
        ya = _dilated(pa.reshape(batch, seq, -1), bias_a, batch, seq)
        w2p = jnp.pad(gla_w_a2[i], ((0, LANES - B_GATE_RANK), (0, 0)))
        yb = _gla(pb.reshape(batch, seq, -1), w2p, gla_b_a[i].reshape(1, B_QK),
                  gla_norm_w[i].reshape(1, B_WIDTH), batch, seq)
        yc = _moba(pc.reshape(batch, seq, -1), bias_c, batch, seq)
        wo = w_out[i].astype(BF16)
        x2d = _outffn(
            x2d, ya.reshape(tokens, -1), yb.reshape(tokens, -1), yc.reshape(tokens, -1),
            wo[:A_WIDTH], wo[A_WIDTH:A_WIDTH + B_WIDTH], wo[A_WIDTH + B_WIDTH:],
            norm2_w[i].reshape(1, d), w_ff1[i].astype(BF16), w_ff2[i].astype(BF16), fn,
            tm, final=(i == depth - 1))
    return x2d.reshape(batch, seq, d)
```

```python
import functools
import math

import numpy as np
import jax
import jax.numpy as jnp
from jax import lax
from jax.experimental import pallas as pl
from jax.experimental.pallas import tpu as pltpu

D_MODEL = 1024
HEAD_DIM = 64
A_HEADS = 6
A_BRANCHES = ((128, 1), (512, 4), (2048, 16))
B_HEADS = 4
B_KEY_DIM = 32
B_VAL_DIM = 64
B_GATE_RANK = 16
B_GATE_TAU = 16.0
C_HEADS = 6
C_BLOCK = 256
C_TOPK = 3
REL_BUCKETS = 32
REL_MAX_DIST = 2048
D_FF = 4 * D_MODEL
EPS = 1e-6
NEG_INF = -1e30
LOG2E = math.log2(math.e)

A_WIDTH = A_HEADS * HEAD_DIM
B_QK = B_HEADS * B_KEY_DIM
B_WIDTH = B_HEADS * B_VAL_DIM
C_WIDTH = C_HEADS * HEAD_DIM

LANES = 128
SUBLANES = 8
A_BLK = 128
PAIR = LANES // HEAD_DIM
VMEM_LIMIT = 56 * 1024 * 1024
F32 = jnp.float32
BF16 = jnp.bfloat16
HIGHEST = lax.Precision.HIGHEST


def _bucket_thresholds():
    n = np.arange(0, REL_MAX_DIST + 1)
    exact = REL_BUCKETS // 2
    logv = (np.log(np.maximum(n, 1).astype(np.float32) / np.float32(exact))
            / np.float32(math.log(REL_MAX_DIST / exact))).astype(np.float32)
    large = np.minimum(exact + (logv * np.float32(REL_BUCKETS - exact)).astype(np.int32), REL_BUCKETS - 1)
    bucket = np.where(n < exact, n, large)
    assert np.all(np.diff(bucket) >= 0)
    return [int(np.argmax(bucket >= k)) for k in range(REL_BUCKETS)]


_THRESH = _bucket_thresholds()


def _nt_dot(a, b):
    return lax.dot_general(a, b, (((1,), (1,)), ((), ())), preferred_element_type=F32)


def _rms(x, w):
    return x * lax.rsqrt(jnp.mean(x * x, axis=-1, keepdims=True) + EPS) * w


def _bias_lookup(dist, rb_ref, col):
    val = jnp.full(dist.shape, rb_ref[0, col], F32)
    for k in range(1, REL_BUCKETS):
        val = jnp.where(dist >= _THRESH[k], rb_ref[k, col], val)
    return val


def _bias_kernel(rb_ref, ba_ref, bc_ref):
    h = pl.program_id(0)
    qi = lax.broadcasted_iota(jnp.int32, (A_BLK, 2 * A_BLK), 0)
    kj = lax.broadcasted_iota(jnp.int32, (A_BLK, 2 * A_BLK), 1)
    steps = qi + A_BLK - kj
    band = (steps >= 0) & (steps <= A_BLK)
    for g, (_, dil) in enumerate(A_BRANCHES):
        ba_ref[0, g] = jnp.where(band, _bias_lookup(steps * dil, rb_ref, h) * LOG2E, NEG_INF)
    qi = lax.broadcasted_iota(jnp.int32, (C_BLOCK, C_BLOCK), 0)
    kj = lax.broadcasted_iota(jnp.int32, (C_BLOCK, C_BLOCK), 1)
    nblk = bc_ref.shape[2] // C_BLOCK
    for delta in range(nblk):
        dist = delta * C_BLOCK + qi - kj
        bias = _bias_lookup(dist, rb_ref, A_HEADS + h) * LOG2E
        if delta == 0:
            bias = jnp.where(dist >= 0, bias, NEG_INF)
        bc_ref[0, :, (nblk - 1 - delta) * C_BLOCK:(nblk - delta) * C_BLOCK] = bias


def _bias_tiles(rel_bias, seq):
    return pl.pallas_call(
        _bias_kernel,
        grid=(A_HEADS,),
        in_specs=[pl.BlockSpec(memory_space=pltpu.SMEM)],
        out_specs=[
            pl.BlockSpec((1, len(A_BRANCHES), A_BLK, 2 * A_BLK), lambda h: (h // PAIR, 0, h % PAIR, 0)),
            pl.BlockSpec((1, C_BLOCK, seq), lambda h: (h // PAIR, h % PAIR, 0)),
        ],
        out_shape=[
            jax.ShapeDtypeStruct((A_HEADS // PAIR, len(A_BRANCHES), PAIR * A_BLK, 2 * A_BLK), F32),
            jax.ShapeDtypeStruct((C_HEADS // PAIR, PAIR * C_BLOCK, seq), F32),
        ],
        name="bias_tiles",
    )(rel_bias)


def _inproj_kernel(x_ref, nw_ref, wa_ref, wc_ref, wb_ref, pa_ref, pc_ref, pb_ref):
    h = _rms(x_ref[...], nw_ref[...]).astype(BF16)
    pa_ref[...] = jnp.dot(h, wa_ref[...], preferred_element_type=F32)
    pc_ref[...] = jnp.dot(h, wc_ref[...], preferred_element_type=F32)
    pb_ref[...] = jnp.dot(h, wb_ref[...], preferred_element_type=F32)


def _inproj(x2d, nw, wa, wc, wb, tm):
    m = x2d.shape[0]
    const = lambda i: (0, 0)
    row = lambda i: (i, 0)
    return pl.pallas_call(
        _inproj_kernel,
        grid=(m // tm,),
        in_specs=[
            pl.BlockSpec((tm, D_MODEL), row),
            pl.BlockSpec((1, D_MODEL), const),
            pl.BlockSpec(wa.shape, const),
            pl.BlockSpec(wc.shape, const),
            pl.BlockSpec(wb.shape, const),
        ],
        out_specs=[
            pl.BlockSpec((tm, wa.shape[1]), row),
            pl.BlockSpec((tm, wc.shape[1]), row),
            pl.BlockSpec((tm, wb.shape[1]), row),
        ],
        out_shape=[
            jax.ShapeDtypeStruct((m, wa.shape[1]), F32),
            jax.ShapeDtypeStruct((m, wc.shape[1]), F32),
            jax.ShapeDtypeStruct((m, wb.shape[1]), F32),
        ],
        compiler_params=pltpu.CompilerParams(vmem_limit_bytes=VMEM_LIMIT),
        name="inproj",
    )(x2d, nw, wa, wc, wb)


A_QSCALE = HEAD_DIM ** -0.5 * LOG2E
A_AHEAD = 4


def _dilated_kernel(q_ref, k_ref, v_ref, bias_ref, o_ref, qs_s, ks_s, vs_s, x_s, m_s, l_s, acc_s):
    seq = q_ref.shape[1]
    (w1, d1), (w2, d2), (w3, d3) = A_BRANCHES
    assert d1 == 1 and d3 == d2 * d2 and all(w // d == A_BLK for w, d in A_BRANCHES)
    coarse = seq // d2
    lane = lax.broadcasted_iota(jnp.int32, (A_BLK, LANES), 1)
    head0 = lane < HEAD_DIM
    vs_s[:, :, LANES:] = jnp.ones((len(A_BRANCHES), seq, LANES), BF16)

    def put(g, dst, q, k, v):
        lane_l = lax.broadcasted_iota(jnp.int32, q.shape, 1)
        qs_s[g, 0, dst, :] = jnp.where(lane_l < HEAD_DIM, q, 0.0).astype(BF16)
        qs_s[g, 1, dst, :] = jnp.where(lane_l < HEAD_DIM, 0.0, q).astype(BF16)
        ks_s[g, dst, :] = k.astype(BF16)
        vs_s[g, dst, :LANES] = v.astype(BF16)

    def logits(g, bi, first):
        qrows = pl.ds(bi * A_BLK, A_BLK)
        q2 = jnp.concatenate([qs_s[g, 0, qrows, :], qs_s[g, 1, qrows, :]], axis=0)
        if first:
            krows = qrows
            bias = bias_ref[0, g, :, A_BLK:]
        else:
            krows = pl.ds((bi - 1) * A_BLK, 2 * A_BLK)
            bias = bias_ref[0, g]
        return _nt_dot(q2, ks_s[g, krows, :]) + bias, krows

    def finish(g, s, krows, out_rows):
        m = jnp.max(s, axis=-1, keepdims=True)
        p = jnp.exp2(s - m).astype(BF16)
        r = jnp.dot(p, vs_s[g, krows, :], preferred_element_type=F32)
        m_s[g, out_rows, :] = jnp.where(head0, m[:A_BLK], m[A_BLK:])
        l_s[g, out_rows, :] = jnp.where(head0, r[:A_BLK, LANES:], r[A_BLK:, LANES:])
        acc_s[g, out_rows, :] = jnp.where(head0, r[:A_BLK, :LANES], r[A_BLK:, :LANES])

    def run(g, blocks):
        pending = []
        for bi, first, out_rows in blocks:
            pending.append((logits(g, bi, first), out_rows))
            if len(pending) > A_AHEAD:
                (s, krows), rows = pending.pop(0)
                finish(g, s, krows, rows)
        for (s, krows), rows in pending:
            finish(g, s, krows, rows)

    put(0, pl.ds(0, seq), q_ref[0] * A_QSCALE, k_ref[0], v_ref[0])
    run(0, [(bi, bi == 0, pl.ds(bi * A_BLK, A_BLK)) for bi in range(seq // A_BLK)])

    for r in range(d2):
        src = pl.ds(r, coarse, stride=d2)
        dst = pl.ds(r * coarse, coarse)
        q, k, v = q_ref[0, src, :] * A_QSCALE, k_ref[0, src, :], v_ref[0, src, :]
        x_s[0, dst, :], x_s[1, dst, :], x_s[2, dst, :] = q, k, v
        put(1, dst, q, k, v)
    nb = coarse // A_BLK
    run(1, [(bi, bi % nb == 0, pl.ds(bi * A_BLK, A_BLK)) for bi in range(d2 * nb)])

    fine = seq // d3
    nb = fine // A_BLK
    for r in range(d3):
        lo, hi = r % d2, r // d2
        src = pl.ds(lo * coarse + hi, fine, stride=d2)
        put(2, pl.ds(r * fine, fine), x_s[0, src, :], x_s[1, src, :], x_s[2, src, :])
    run(2, [(r * nb + n, n == 0, pl.ds((r % d2) * coarse + r // d2 + n * A_BLK * d2, A_BLK, stride=d2))
            for r in range(d3) for n in range(nb)])

    for r in range(d2):
        rows = pl.ds(r * coarse, coarse)
        tok = pl.ds(r, coarse, stride=d2)
        m1, m2, m3 = m_s[0, tok, :], m_s[1, rows, :], m_s[2, rows, :]
        m = jnp.maximum(jnp.maximum(m1, m2), m3)
        e1, e2, e3 = jnp.exp2(m1 - m), jnp.exp2(m2 - m), jnp.exp2(m3 - m)
        num = e1 * acc_s[0, tok, :] + e2 * acc_s[1, rows, :] + e3 * acc_s[2, rows, :]
        den = e1 * l_s[0, tok, :] + e2 * l_s[1, rows, :] + e3 * l_s[2, rows, :]
        o_ref[0, tok, :] = num / den


def _dilated(pa, bias_a, batch, seq):
    npair = A_HEADS // PAIR
    nbr = len(A_BRANCHES)
    blk = lambda off: pl.BlockSpec((1, seq, LANES), lambda b, p: (b, 0, off + p))
    return pl.pallas_call(
        _dilated_kernel,
        grid=(batch, npair),
        in_specs=[
            blk(0), blk(npair), blk(2 * npair),
            pl.BlockSpec((1, nbr, PAIR * A_BLK, 2 * A_BLK), lambda b, p: (p, 0, 0, 0)),
        ],
        out_specs=pl.BlockSpec((1, seq, LANES), lambda b, p: (b, 0, p)),
        out_shape=jax.ShapeDtypeStruct((batch, seq, A_WIDTH), F32),
        scratch_shapes=[
            pltpu.VMEM((nbr, PAIR, seq, LANES), BF16),
            pltpu.VMEM((nbr, seq, LANES), BF16),
            pltpu.VMEM((nbr, seq, 2 * LANES), BF16),
            pltpu.VMEM((3, seq, LANES), F32),
            pltpu.VMEM((nbr, seq, LANES), F32),
            pltpu.VMEM((nbr, seq, LANES), F32),
            pltpu.VMEM((nbr, seq, LANES), F32),
        ],
        compiler_params=pltpu.CompilerParams(vmem_limit_bytes=VMEM_LIMIT),
        name="dilated_attn",
    )(pa, pa, pa, bias_a)


GLA_TILE = 128
GLA_LEVELS = GLA_TILE.bit_length() - 1
GLA_GROUP = 16
PB_Q, PB_K, PB_V, PB_R, PB_A = 0, B_QK, 2 * B_QK, 2 * B_QK + B_WIDTH, 2 * B_QK + 2 * B_WIDTH
PB_WIDTH = PB_A + LANES


def _log_sigmoid(z):
    return jnp.minimum(z, 0.0) - jnp.log1p(jnp.exp(-jnp.abs(z)))


def _gla_constants():
    t = GLA_TILE
    i = np.arange(t)[:, None]
    c = np.arange(t)[None, :]
    blocks = [(c <= i).astype(np.float32)]
    for lvl in range(1, GLA_LEVELS + 1):
        w = t >> lvl
        split = (i // (2 * w)) * (2 * w) + w - 1
        blocks.append((c <= i).astype(np.float32) - (c <= split).astype(np.float32))
    x = np.arange(t)[:, None] ^ np.arange(t)[None, :]
    level_of_pair = np.where(np.arange(t)[None, :] < np.arange(t)[:, None],
                             np.floor(np.log2(np.maximum(x, 1))).astype(np.int32), -1)
    return np.concatenate(blocks, axis=0), np.tile(level_of_pair, (1, B_HEADS)).astype(np.int32)


def _gla_kernel(pb_ref, w2_ref, ba_ref, nw_ref, g_ref, lvl_ref, y_ref):
    seq = pb_ref.shape[1]
    t = GLA_TILE
    kl = lax.broadcasted_iota(jnp.int32, (B_QK, B_WIDTH), 0) // B_KEY_DIM
    vl = lax.broadcasted_iota(jnp.int32, (B_QK, B_WIDTH), 1) // B_VAL_DIM
    expand = jnp.where(kl == vl, 1.0, 0.0).astype(BF16)
    vlt = lax.broadcasted_iota(jnp.int32, (B_WIDTH, B_QK), 0) // B_VAL_DIM
    klt = lax.broadcasted_iota(jnp.int32, (B_WIDTH, B_QK), 1) // B_KEY_DIM
    head_t = vlt == klt
    va = (lax.broadcasted_iota(jnp.int32, (2 * B_WIDTH, B_WIDTH), 0) % B_WIDTH) // B_VAL_DIM
    vb = lax.broadcasted_iota(jnp.int32, (2 * B_WIDTH, B_WIDTH), 1) // B_VAL_DIM
    head_mean2 = jnp.where(va == vb, 1.0 / B_VAL_DIM, 0.0).astype(BF16)
    row = lax.broadcasted_iota(jnp.int32, (t, B_QK), 0)
    rep_head_k = lax.broadcasted_iota(jnp.int32, (B_HEADS * t, B_QK), 0) // t
    own_key_lanes = rep_head_k == lax.broadcasted_iota(jnp.int32, (B_HEADS * t, B_QK), 1) // B_KEY_DIM
    rep_head_v = lax.broadcasted_iota(jnp.int32, (B_HEADS * t, B_WIDTH), 0) // t
    own_val_lanes = rep_head_v == lax.broadcasted_iota(jnp.int32, (B_HEADS * t, B_WIDTH), 1) // B_VAL_DIM
    zero_k4 = jnp.zeros((B_HEADS * t, B_QK), BF16)
    zero_v4 = jnp.zeros((B_HEADS * t, B_WIDTH), BF16)

    def log_decay(ti):
        rows = pl.ds(pl.multiple_of(ti * t, t), t)
        z = jnp.dot(pb_ref[0, rows, PB_A:PB_A + LANES], w2_ref[...], precision=HIGHEST,
                    preferred_element_type=F32) + ba_ref[...]
        return _log_sigmoid(z) * (LOG2E / B_GATE_TAU)

    def decays(log_a):
        a1 = log_a.astype(BF16)
        a2 = (log_a - a1.astype(F32)).astype(BF16)
        dd = jnp.dot(g_ref[...], jnp.concatenate([a1, a2], axis=1), preferred_element_type=F32)
        return dd[:, :B_QK] + dd[:, B_QK:]

    def pairs(ti, dist):
        rows = pl.ds(pl.multiple_of(ti * t, t), t)
        q = pb_ref[0, rows, PB_Q:PB_Q + B_QK] * (B_KEY_DIM ** -0.5)
        k = pb_ref[0, rows, PB_K:PB_K + B_QK]
        attn = jnp.zeros((t, B_HEADS * t), F32)
        for lvl in range(1, GLA_LEVELS + 1):
            w = t >> lvl
            decay = jnp.exp2(-jnp.abs(dist[lvl * t:(lvl + 1) * t]))
            after = (row & w) != 0
            q_l = (jnp.where(after, q * decay, 0.0)).astype(BF16)
            k_l = jnp.where(after, 0.0, k * decay).astype(BF16)
            k4 = jnp.where(own_key_lanes, jnp.concatenate([k_l] * B_HEADS, axis=0), zero_k4)
            attn = jnp.where(lvl_ref[...] == GLA_LEVELS - lvl, _nt_dot(q_l, k4), attn)
        return attn.astype(BF16), q, k

    def within(ti, dist, attn, q, k):
        rows = pl.ds(pl.multiple_of(ti * t, t), t)
        b = dist[:t]
        b_last = b[t - 1:t, :]
        v = pb_ref[0, rows, PB_V:PB_V + B_WIDTH]
        v_bf = v.astype(BF16)
        v4 = jnp.where(own_val_lanes, jnp.concatenate([v_bf] * B_HEADS, axis=0), zero_v4)
        o = jnp.dot(attn, v4, preferred_element_type=F32)
        o = o + jnp.dot((q * k).astype(BF16), expand, preferred_element_type=F32) * v
        q_dec = (q * jnp.exp2(b)).astype(BF16)
        k_dec = (k * jnp.exp2(b_last - b)).astype(BF16)
        u_t = jnp.dot(v.T.astype(BF16), k_dec, preferred_element_type=F32)
        return o, q_dec, jnp.where(head_t, u_t, 0.0), jnp.exp2(b_last)

    def finish(ti, o):
        rows = pl.ds(pl.multiple_of(ti * t, t), t)
        sq = o * o
        sq_hi = sq.astype(BF16)
        sq_lo = (sq - sq_hi.astype(F32)).astype(BF16)
        ms = jnp.dot(jnp.concatenate([sq_hi, sq_lo], axis=1), head_mean2, preferred_element_type=F32)
        r = pb_ref[0, rows, PB_R:PB_R + B_WIDTH]
        silu = r / (1.0 + jnp.exp(-r))
        y_ref[0, rows, :] = o * lax.rsqrt(ms + EPS) * nw_ref[...] * silu

    def group(gi, state):
        tiles = [gi * GLA_GROUP + j for j in range(GLA_GROUP)]
        dists = [decays(log_a) for log_a in [log_decay(ti) for ti in tiles]]
        weights = [pairs(ti, dist) for ti, dist in zip(tiles, dists)]
        parts = [within(ti, dist, *w) for ti, dist, w in zip(tiles, dists, weights)]
        outs = []
        for o, q_dec, u_t, gate in parts:
            outs.append(o + _nt_dot(q_dec, state.astype(BF16)))
            state = state * gate + u_t
        for ti, o in zip(tiles, outs):
            finish(ti, o)
        return state

    lax.fori_loop(0, seq // (t * GLA_GROUP), group, jnp.zeros((B_WIDTH, B_QK), F32))


def _gla(pb, w2p, b_a, norm_w, batch, seq):
    const = lambda b: (0, 0)
    g_all, lvl = _gla_constants()
    g_all = jnp.asarray(g_all, BF16)
    lvl = jnp.asarray(lvl)
    return pl.pallas_call(
        _gla_kernel,
        grid=(batch,),
        in_specs=[
            pl.BlockSpec((1, seq, PB_WIDTH), lambda b: (b, 0, 0)),
            pl.BlockSpec((LANES, B_QK), const),
            pl.BlockSpec((1, B_QK), const),
            pl.BlockSpec((1, B_WIDTH), const),
            pl.BlockSpec(g_all.shape, const),
            pl.BlockSpec(lvl.shape, const),
        ],
        out_specs=pl.BlockSpec((1, seq, B_WIDTH), lambda b: (b, 0, 0)),
        out_shape=jax.ShapeDtypeStruct((batch, seq, B_WIDTH), F32),
        compiler_params=pltpu.CompilerParams(vmem_limit_bytes=VMEM_LIMIT),
        name="gla",
    )(pb, w2p, b_a, norm_w, g_all, lvl)


C_AHEAD = 2


def _moba_kernel(q_ref, k_ref, v_ref, bias_ref, o_ref, ka_s, qa_s, va_s):
    seq = q_ref.shape[1]
    nblk = seq // C_BLOCK
    head0 = lax.broadcasted_iota(jnp.int32, (C_BLOCK, LANES), 1) < HEAD_DIM

    k = k_ref[0]
    q = q_ref[0]
    ka_s[:, :LANES] = k.astype(BF16)
    row_blk = lax.broadcasted_iota(jnp.int32, (seq, LANES), 0) // C_BLOCK
    lane_id = lax.broadcasted_iota(jnp.int32, (seq, LANES), 1)
    ka_s[:, LANES:] = jnp.where(lane_id == row_blk, 1.0, 0.0).astype(BF16)
    va_s[:, :LANES] = v_ref[0].astype(BF16)
    va_s[:, LANES:] = jnp.ones((seq, LANES), BF16)
    k_mean = jnp.mean(k.reshape(nblk, C_BLOCK, LANES), axis=1)
    q_scaled = q * (HEAD_DIM ** -0.5 * LOG2E)
    blk = lax.broadcasted_iota(jnp.int32, (nblk, seq), 0)
    q_blk = lax.broadcasted_iota(jnp.int32, (nblk, seq), 1) // C_BLOCK
    for hh in range(PAIR):
        qa_s[hh, :, :LANES] = jnp.where((lane_id < HEAD_DIM) == (hh == 0), q_scaled, 0.0).astype(BF16)
        gate = _nt_dot(k_mean.astype(BF16), qa_s[hh, :, :LANES])
        rank = jnp.zeros((nblk, seq), jnp.int32)
        for m in range(nblk):
            gm = gate[m:m + 1, :]
            beats = (gm > gate) | ((gm == gate) & (m < blk))
            rank = rank + jnp.where(beats & (m < q_blk), 1, 0)
        dropped = (blk < q_blk) & (rank >= C_TOPK)
        pen = jnp.where(dropped, NEG_INF, 0.0)
        assert nblk == SUBLANES
        pen_t = jnp.concatenate([pen, jnp.zeros((LANES - nblk, seq), F32)], axis=0).T
        qa_s[hh, :, LANES:] = pen_t.astype(BF16)

    def logits(qb):
        qrows = slice(qb * C_BLOCK, (qb + 1) * C_BLOCK)
        q2 = jnp.concatenate([qa_s[0, qrows, :], qa_s[1, qrows, :]], axis=0)
        return _nt_dot(q2, ka_s[:(qb + 1) * C_BLOCK, :]) + bias_ref[0, :, (nblk - 1 - qb) * C_BLOCK:]

    def finish(qb, s):
        m = jnp.max(s, axis=-1, keepdims=True)
        p = jnp.exp2(s - m).astype(BF16)
        r = jnp.dot(p, va_s[:(qb + 1) * C_BLOCK, :], preferred_element_type=F32)
        o_ref[0, qb * C_BLOCK:(qb + 1) * C_BLOCK, :] = jnp.where(
            head0, r[:C_BLOCK, :LANES] / r[:C_BLOCK, LANES:], r[C_BLOCK:, :LANES] / r[C_BLOCK:, LANES:])

    pending = []
    for qb in reversed(range(nblk)):
        pending.append((qb, logits(qb)))
        if len(pending) > C_AHEAD:
            finish(*pending.pop(0))
    for item in pending:
        finish(*item)


def _moba(pc, bias_c, batch, seq):
    npair = C_HEADS // PAIR
    blk = lambda off: pl.BlockSpec((1, seq, LANES), lambda b, p: (b, 0, off + p))
    return pl.pallas_call(
        _moba_kernel,
        grid=(batch, npair),
        in_specs=[
            blk(0), blk(npair), blk(2 * npair),
            pl.BlockSpec((1, PAIR * C_BLOCK, seq), lambda b, p: (p, 0, 0)),
        ],
        out_specs=pl.BlockSpec((1, seq, LANES), lambda b, p: (b, 0, p)),
        out_shape=jax.ShapeDtypeStruct((batch, seq, C_WIDTH), F32),
        scratch_shapes=[
            pltpu.VMEM((seq, 2 * LANES), BF16),
            pltpu.VMEM((PAIR, seq, 2 * LANES), BF16),
            pltpu.VMEM((seq, 2 * LANES), BF16),
        ],
        compiler_params=pltpu.CompilerParams(vmem_limit_bytes=VMEM_LIMIT),
        name="moba_attn",
    )(pc, pc, pc, bias_c)


FF_CHUNK = 1024


def _outffn_kernel(x_ref, ya_ref, yb_ref, yc_ref, woa_ref, wob_ref, woc_ref, n2_ref,
                   w1_ref, w2_ref, fn_ref, o_ref, *, final):
    x = x_ref[...]
    x = x + jnp.dot(ya_ref[...].astype(BF16), woa_ref[...], preferred_element_type=F32)
    x = x + jnp.dot(yb_ref[...].astype(BF16), wob_ref[...], preferred_element_type=F32)
    x = x + jnp.dot(yc_ref[...].astype(BF16), woc_ref[...], preferred_element_type=F32)
    h = _rms(x, n2_ref[...]).astype(BF16)
    ffn = None
    for j in range(D_FF // FF_CHUNK):
        cols = slice(j * FF_CHUNK, (j + 1) * FF_CHUNK)
        a = jnp.dot(h, w1_ref[:, cols], preferred_element_type=F32)
        a = jnp.square(jnp.maximum(a, 0.0)).astype(BF16)
        part = jnp.dot(a, w2_ref[cols, :], preferred_element_type=F32)
        ffn = part if ffn is None else ffn + part
    x = x + ffn
    if final:
        x = _rms(x, fn_ref[...])
    o_ref[...] = x


def _outffn(x2d, ya, yb, yc, woa, wob, woc, n2, w1, w2, fn, tm, final):
    m = x2d.shape[0]
    const = lambda i: (0, 0)
    row = lambda i: (i, 0)
    full = lambda a: pl.BlockSpec(a.shape, const)
    return pl.pallas_call(
        functools.partial(_outffn_kernel, final=final),
        grid=(m // tm,),
        in_specs=[
            pl.BlockSpec((tm, D_MODEL), row),
            pl.BlockSpec((tm, ya.shape[1]), row),
            pl.BlockSpec((tm, yb.shape[1]), row),
            pl.BlockSpec((tm, yc.shape[1]), row),
            full(woa), full(wob), full(woc), full(n2), full(w1), full(w2), full(fn),
        ],
        out_specs=pl.BlockSpec((tm, D_MODEL), row),
        out_shape=jax.ShapeDtypeStruct((m, D_MODEL), F32),
        compiler_params=pltpu.CompilerParams(vmem_limit_bytes=VMEM_LIMIT),
        name="outproj_ffn",
    )(x2d, ya, yb, yc, woa, wob, woc, n2, w1, w2, fn)


def _split_w_in(w):
    sizes = (A_WIDTH, A_WIDTH, A_WIDTH, B_QK, B_QK, B_WIDTH, B_WIDTH, B_GATE_RANK, C_WIDTH, C_WIDTH, C_WIDTH)
    offs = np.concatenate([[0], np.cumsum(sizes)])
    aq, ak, av, bq, bk, bv, br, ba, cq, ck, cv = [w[:, offs[i]:offs[i + 1]] for i in range(len(sizes))]
    ba = jnp.pad(ba, ((0, 0), (0, LANES - B_GATE_RANK)))
    cat = lambda parts: jnp.concatenate(parts, axis=1).astype(BF16)
    return cat([aq, ak, av]), cat([cq, ck, cv]), cat([bq, bk, bv, br, ba])


def kernel(x, norm1_w, w_in, gla_w_a2, gla_b_a, gla_norm_w, w_out, norm2_w, w_ff1, w_ff2, rel_bias, final_norm_w):
    batch, seq, d = x.shape
    depth = w_in.shape[0]
    assert d == D_MODEL and seq % C_BLOCK == 0 and seq % (A_BLK * A_BRANCHES[-1][1]) == 0
    tokens = batch * seq
    tm = 512 if tokens % 512 == 0 else seq
    bias_a, bias_c = _bias_tiles(rel_bias, seq)
    x2d = x.reshape(tokens, d)
    fn = final_norm_w.reshape(1, d)
    for i in range(depth):
        wa, wc, wb = _split_w_in(w_in[i])
        pa, pc, pb = _inproj(x2d, norm1_w[i].reshape(1, d), wa, wc, wb, tm)
        ya = _dilated(pa.reshape(batch, seq, -1), bias_a, batch, seq)
        w2p = jnp.pad(gla_w_a2[i], ((0, LANES - B_GATE_RANK), (0, 0)))
        yb = _gla(pb.reshape(batch, seq, -1), w2p, gla_b_a[i].reshape(1, B_QK),
                  gla_norm_w[i].reshape(1, B_WIDTH), batch, seq)
        yc = _moba(pc.reshape(batch, seq, -1), bias_c, batch, seq)
        wo = w_out[i].astype(BF16)
        x2d = _outffn(
            x2d, ya.reshape(tokens, -1), yb.reshape(tokens, -1), yc.reshape(tokens, -1),
            wo[:A_WIDTH], wo[A_WIDTH:A_WIDTH + B_WIDTH], wo[A_WIDTH + B_WIDTH:],
            norm2_w[i].reshape(1, d), w_ff1[i].astype(BF16), w_ff2[i].astype(BF16), fn,
            tm, final=(i == depth - 1))
    return x2d.reshape(batch, seq, d)
```

```python
import functools
import math

import numpy as np
import jax
import jax.numpy as jnp
from jax import lax
from jax.experimental import pallas as pl
from jax.experimental.pallas import tpu as pltpu

D_MODEL = 1024
HEAD_DIM = 64
A_HEADS = 6
A_BRANCHES = ((128, 1), (512, 4), (2048, 16))
B_HEADS = 4
B_KEY_DIM = 32
B_VAL_DIM = 64
B_GATE_RANK = 16
B_GATE_TAU = 16.0
C_HEADS = 6
C_BLOCK = 256
C_TOPK = 3
REL_BUCKETS = 32
REL_MAX_DIST = 2048
D_FF = 4 * D_MODEL
EPS = 1e-6
NEG_INF = -1e30
LOG2E = math.log2(math.e)

A_WIDTH = A_HEADS * HEAD_DIM
B_QK = B_HEADS * B_KEY_DIM
B_WIDTH = B_HEADS * B_VAL_DIM
C_WIDTH = C_HEADS * HEAD_DIM

LANES = 128
SUBLANES = 8
A_BLK = 128
PAIR = LANES // HEAD_DIM
VMEM_LIMIT = 56 * 1024 * 1024
F32 = jnp.float32
BF16 = jnp.bfloat16


def _bucket_thresholds():
    n = np.arange(0, REL_MAX_DIST + 1)
    exact = REL_BUCKETS // 2
    logv = (np.log(np.maximum(n, 1).astype(np.float32) / np.float32(exact))
            / np.float32(math.log(REL_MAX_DIST / exact))).astype(np.float32)
    large = np.minimum(exact + (logv * np.float32(REL_BUCKETS - exact)).astype(np.int32), REL_BUCKETS - 1)
    bucket = np.where(n < exact, n, large)
    assert np.all(np.diff(bucket) >= 0)
    return [int(np.argmax(bucket >= k)) for k in range(REL_BUCKETS)]


_THRESH = _bucket_thresholds()


def _nt_dot(a, b):
    return lax.dot_general(a, b, (((1,), (1,)), ((), ())), preferred_element_type=F32)


def _rms(x, w):
    return x * lax.rsqrt(jnp.mean(x * x, axis=-1, keepdims=True) + EPS) * w


def _bias_lookup(dist, rb_ref, col):
    val = jnp.full(dist.shape, rb_ref[0, col], F32)
    for k in range(1, REL_BUCKETS):
        val = jnp.where(dist >= _THRESH[k], rb_ref[k, col], val)
    return val


def _bias_kernel(rb_ref, ba_ref, bc_ref):
    h = pl.program_id(0)
    qi = lax.broadcasted_iota(jnp.int32, (A_BLK, 2 * A_BLK), 0)
    kj = lax.broadcasted_iota(jnp.int32, (A_BLK, 2 * A_BLK), 1)
    steps = qi + A_BLK - kj
    band = (steps >= 0) & (steps <= A_BLK)
    for g, (_, dil) in enumerate(A_BRANCHES):
        ba_ref[0, g] = jnp.where(band, _bias_lookup(steps * dil, rb_ref, h) * LOG2E, NEG_INF)
    qi = lax.broadcasted_iota(jnp.int32, (C_BLOCK, C_BLOCK), 0)
    kj = lax.broadcasted_iota(jnp.int32, (C_BLOCK, C_BLOCK), 1)
    nblk = bc_ref.shape[2] // C_BLOCK
    for delta in range(nblk):
        dist = delta * C_BLOCK + qi - kj
        bias = _bias_lookup(dist, rb_ref, A_HEADS + h) * LOG2E
        if delta == 0:
            bias = jnp.where(dist >= 0, bias, NEG_INF)
        bc_ref[0, :, (nblk - 1 - delta) * C_BLOCK:(nblk - delta) * C_BLOCK] = bias


def _bias_tiles(rel_bias, seq):
    return pl.pallas_call(
        _bias_kernel,
        grid=(A_HEADS,),
        in_specs=[pl.BlockSpec(memory_space=pltpu.SMEM)],
        out_specs=[
            pl.BlockSpec((1, len(A_BRANCHES), A_BLK, 2 * A_BLK), lambda h: (h // PAIR, 0, h % PAIR, 0)),
            pl.BlockSpec((1, C_BLOCK, seq), lambda h: (h // PAIR, h % PAIR, 0)),
        ],
        out_shape=[
            jax.ShapeDtypeStruct((A_HEADS // PAIR, len(A_BRANCHES), PAIR * A_BLK, 2 * A_BLK), F32),
            jax.ShapeDtypeStruct((C_HEADS // PAIR, PAIR * C_BLOCK, seq), F32),
        ],
        name="bias_tiles",
    )(rel_bias)


def _inproj_kernel(x_ref, nw_ref, wa_ref, wc_ref, wb_ref, pa_ref, pc_ref, pb_ref):
    h = _rms(x_ref[...], nw_ref[...]).astype(BF16)
    pa_ref[...] = jnp.dot(h, wa_ref[...], preferred_element_type=F32)
    pc_ref[...] = jnp.dot(h, wc_ref[...], preferred_element_type=F32)
    pb_ref[...] = jnp.dot(h, wb_ref[...], preferred_element_type=F32)


def _inproj(x2d, nw, wa, wc, wb, tm):
    m = x2d.shape[0]
    const = lambda i: (0, 0)
    row = lambda i: (i, 0)
    return pl.pallas_call(
        _inproj_kernel,
        grid=(m // tm,),
        in_specs=[
            pl.BlockSpec((tm, D_MODEL), row),
            pl.BlockSpec((1, D_MODEL), const),
            pl.BlockSpec(wa.shape, const),
            pl.BlockSpec(wc.shape, const),
            pl.BlockSpec(wb.shape, const),
        ],
        out_specs=[
            pl.BlockSpec((tm, wa.shape[1]), row),
            pl.BlockSpec((tm, wc.shape[1]), row),
            pl.BlockSpec((tm, wb.shape[1]), row),
        ],
        out_shape=[
            jax.ShapeDtypeStruct((m, wa.shape[1]), F32),
            jax.ShapeDtypeStruct((m, wc.shape[1]), F32),
            jax.ShapeDtypeStruct((m, wb.shape[1]), F32),
        ],
        compiler_params=pltpu.CompilerParams(vmem_limit_bytes=VMEM_LIMIT),
        name="inproj",
    )(x2d, nw, wa, wc, wb)


A_QSCALE = HEAD_DIM ** -0.5 * LOG2E
A_AHEAD = 5


def _dilated_kernel(q_ref, k_ref, v_ref, bias_ref, o_ref, qs_s, ks_s, vs_s, x_s, m_s, l_s, acc_s):
    seq = q_ref.shape[1]
    (w1, d1), (w2, d2), (w3, d3) = A_BRANCHES
    assert d1 == 1 and d3 == d2 * d2 and all(w // d == A_BLK for w, d in A_BRANCHES)
    coarse = seq // d2
    lane = lax.broadcasted_iota(jnp.int32, (A_BLK, LANES), 1)
    head0 = lane < HEAD_DIM
    vs_s[:, :, LANES:] = jnp.ones((len(A_BRANCHES), seq, LANES), BF16)

    def put(g, dst, q, k, v):
        lane_l = lax.broadcasted_iota(jnp.int32, q.shape, 1)
        qs_s[g, 0, dst, :] = jnp.where(lane_l < HEAD_DIM, q, 0.0).astype(BF16)
        qs_s[g, 1, dst, :] = jnp.where(lane_l < HEAD_DIM, 0.0, q).astype(BF16)
        ks_s[g, dst, :] = k.astype(BF16)
        vs_s[g, dst, :LANES] = v.astype(BF16)

    def logits(g, bi, first):
        qrows = pl.ds(bi * A_BLK, A_BLK)
        q2 = jnp.concatenate([qs_s[g, 0, qrows, :], qs_s[g, 1, qrows, :]], axis=0)
        if first:
            krows = qrows
            bias = bias_ref[0, g, :, A_BLK:]
        else:
            krows = pl.ds((bi - 1) * A_BLK, 2 * A_BLK)
            bias = bias_ref[0, g]
        return _nt_dot(q2, ks_s[g, krows, :]) + bias, krows

    def finish(g, s, krows, out_rows):
        m = jnp.max(s, axis=-1, keepdims=True)
        p = jnp.exp2(s - m).astype(BF16)
        r = jnp.dot(p, vs_s[g, krows, :], preferred_element_type=F32)
        m_s[g, out_rows, :] = jnp.where(head0, m[:A_BLK], m[A_BLK:])
        l_s[g, out_rows, :] = jnp.where(head0, r[:A_BLK, LANES:], r[A_BLK:, LANES:])
        acc_s[g, out_rows, :] = jnp.where(head0, r[:A_BLK, :LANES], r[A_BLK:, :LANES])

    def run(blocks):
        pending = []
        for g, bi, first, out_rows in blocks:
            pending.append((g, logits(g, bi, first), out_rows))
            if len(pending) > A_AHEAD:
                g0, (s, krows), rows = pending.pop(0)
                finish(g0, s, krows, rows)
        for g0, (s, krows), rows in pending:
            finish(g0, s, krows, rows)

    put(0, pl.ds(0, seq), q_ref[0] * A_QSCALE, k_ref[0], v_ref[0])
    for r in range(d2):
        src = pl.ds(r, coarse, stride=d2)
        dst = pl.ds(r * coarse, coarse)
        q, k, v = q_ref[0, src, :] * A_QSCALE, k_ref[0, src, :], v_ref[0, src, :]
        x_s[0, dst, :], x_s[1, dst, :], x_s[2, dst, :] = q, k, v
        put(1, dst, q, k, v)
    fine = seq // d3
    for r in range(d3):
        src = pl.ds((r % d2) * coarse + r // d2, fine, stride=d2)
        put(2, pl.ds(r * fine, fine), x_s[0, src, :], x_s[1, src, :], x_s[2, src, :])

    nb1, nb2 = coarse // A_BLK, fine // A_BLK
    run([(0, bi, bi == 0, pl.ds(bi * A_BLK, A_BLK)) for bi in range(seq // A_BLK)]
        + [(1, bi, bi % nb1 == 0, pl.ds(bi * A_BLK, A_BLK)) for bi in range(d2 * nb1)]
        + [(2, r * nb2 + n, n == 0, pl.ds((r % d2) * coarse + r // d2 + n * A_BLK * d2, A_BLK, stride=d2))
           for r in range(d3) for n in range(nb2)])

    for r in range(d2):
        rows = pl.ds(r * coarse, coarse)
        tok = pl.ds(r, coarse, stride=d2)
        m1, m2, m3 = m_s[0, tok, :], m_s[1, rows, :], m_s[2, rows, :]
        m = jnp.maximum(jnp.maximum(m1, m2), m3)
        e1, e2, e3 = jnp.exp2(m1 - m), jnp.exp2(m2 - m), jnp.exp2(m3 - m)
        num = e1 * acc_s[0, tok, :] + e2 * acc_s[1, rows, :] + e3 * acc_s[2, rows, :]
        den = e1 * l_s[0, tok, :] + e2 * l_s[1, rows, :] + e3 * l_s[2, rows, :]
        o_ref[0, tok, :] = num / den


def _dilated(pa, bias_a, batch, seq):
    npair = A_HEADS // PAIR
    nbr = len(A_BRANCHES)
    blk = lambda off: pl.BlockSpec((1, seq, LANES), lambda b, p: (b, 0, off + p))
    return pl.pallas_call(
        _dilated_kernel,
        grid=(batch, npair),
        in_specs=[
            blk(0), blk(npair), blk(2 * npair),
            pl.BlockSpec((1, nbr, PAIR * A_BLK, 2 * A_BLK), lambda b, p: (p, 0, 0, 0)),
        ],
        out_specs=pl.BlockSpec((1, seq, LANES), lambda b, p: (b, 0, p)),
        out_shape=jax.ShapeDtypeStruct((batch, seq, A_WIDTH), F32),
        scratch_shapes=[
            pltpu.VMEM((nbr, PAIR, seq, LANES), BF16),
            pltpu.VMEM((nbr, seq, LANES), BF16),
            pltpu.VMEM((nbr, seq, 2 * LANES), BF16),
            pltpu.VMEM((3, seq, LANES), F32),
            pltpu.VMEM((nbr, seq, LANES), F32),
            pltpu.VMEM((nbr, seq, LANES), F32),
            pltpu.VMEM((nbr, seq, LANES), F32),
        ],
        compiler_params=pltpu.CompilerParams(vmem_limit_bytes=VMEM_LIMIT),
        name="dilated_attn",
    )(pa, pa, pa, bias_a)


GLA_TILE = 128
GLA_LEVELS = GLA_TILE.bit_length() - 1
GLA_GROUP = 16
PB_Q, PB_K, PB_V, PB_R, PB_A = 0, B_QK, 2 * B_QK, 2 * B_QK + B_WIDTH, 2 * B_QK + 2 * B_WIDTH
PB_WIDTH = PB_A + LANES


def _log_sigmoid(z):
    return jnp.minimum(z, 0.0) - jnp.log1p(jnp.exp(-jnp.abs(z)))


def _gla_constants():
    t = GLA_TILE
    i = np.arange(t)[:, None]
    c = np.arange(t)[None, :]
    blocks = [(c <= i).astype(np.float32)]
    for lvl in range(1, GLA_LEVELS + 1):
        w = t >> lvl
        split = (i // (2 * w)) * (2 * w) + w - 1
        blocks.append((c <= i).astype(np.float32) - (c <= split).astype(np.float32))
    x = np.arange(t)[:, None] ^ np.arange(t)[None, :]
    level_of_pair = np.where(np.arange(t)[None, :] < np.arange(t)[:, None],
                             np.floor(np.log2(np.maximum(x, 1))).astype(np.int32), -1)
    return np.concatenate(blocks, axis=0), np.tile(level_of_pair, (1, B_HEADS)).astype(np.int32)


def _gla_kernel(pb_ref, w2_ref, ba_ref, nw_ref, g_ref, lvl_ref, y_ref):
    seq = pb_ref.shape[1]
    t = GLA_TILE
    kl = lax.broadcasted_iota(jnp.int32, (B_QK, B_WIDTH), 0) // B_KEY_DIM
    vl = lax.broadcasted_iota(jnp.int32, (B_QK, B_WIDTH), 1) // B_VAL_DIM
    expand = jnp.where(kl == vl, 1.0, 0.0).astype(BF16)
    vlt = lax.broadcasted_iota(jnp.int32, (B_WIDTH, B_QK), 0) // B_VAL_DIM
    klt = lax.broadcasted_iota(jnp.int32, (B_WIDTH, B_QK), 1) // B_KEY_DIM
    head_t = vlt == klt
    va = (lax.broadcasted_iota(jnp.int32, (2 * B_WIDTH, B_WIDTH), 0) % B_WIDTH) // B_VAL_DIM
    vb = lax.broadcasted_iota(jnp.int32, (2 * B_WIDTH, B_WIDTH), 1) // B_VAL_DIM
    head_mean2 = jnp.where(va == vb, 1.0 / B_VAL_DIM, 0.0).astype(BF16)
    row = lax.broadcasted_iota(jnp.int32, (t, B_QK), 0)
    rep_head_k = lax.broadcasted_iota(jnp.int32, (B_HEADS * t, B_QK), 0) // t
    own_key_lanes = rep_head_k == lax.broadcasted_iota(jnp.int32, (B_HEADS * t, B_QK), 1) // B_KEY_DIM
    rep_head_v = lax.broadcasted_iota(jnp.int32, (B_HEADS * t, B_WIDTH), 0) // t
    own_val_lanes = rep_head_v == lax.broadcasted_iota(jnp.int32, (B_HEADS * t, B_WIDTH), 1) // B_VAL_DIM
    zero_k4 = jnp.zeros((B_HEADS * t, B_QK), BF16)
    zero_v4 = jnp.zeros((B_HEADS * t, B_WIDTH), BF16)

    def log_decay(ti):
        rows = pl.ds(pl.multiple_of(ti * t, t), t)
        a = pb_ref[0, rows, PB_A:PB_A + LANES]
        a_hi = a.astype(BF16)
        a_lo = (a - a_hi.astype(F32)).astype(BF16)
        z = jnp.dot(jnp.concatenate([a_hi, a_hi, a_lo], axis=1), w2_ref[...], preferred_element_type=F32)
        return _log_sigmoid(z + ba_ref[...]) * (LOG2E / B_GATE_TAU)

    def decays(log_a):
        a1 = log_a.astype(BF16)
        a2 = (log_a - a1.astype(F32)).astype(BF16)
        dd = jnp.dot(g_ref[...], jnp.concatenate([a1, a2], axis=1), preferred_element_type=F32)
        return dd[:, :B_QK] + dd[:, B_QK:]

    def pairs(ti, dist):
        rows = pl.ds(pl.multiple_of(ti * t, t), t)
        q = pb_ref[0, rows, PB_Q:PB_Q + B_QK] * (B_KEY_DIM ** -0.5)
        k = pb_ref[0, rows, PB_K:PB_K + B_QK]
        attn = jnp.zeros((t, B_HEADS * t), F32)
        for lvl in range(1, GLA_LEVELS + 1):
            w = t >> lvl
            decay = jnp.exp2(-jnp.abs(dist[lvl * t:(lvl + 1) * t]))
            after = (row & w) != 0
            q_l = (jnp.where(after, q * decay, 0.0)).astype(BF16)
            k_l = jnp.where(after, 0.0, k * decay).astype(BF16)
            k4 = jnp.where(own_key_lanes, jnp.concatenate([k_l] * B_HEADS, axis=0), zero_k4)
            attn = jnp.where(lvl_ref[...] == GLA_LEVELS - lvl, _nt_dot(q_l, k4), attn)
        return attn.astype(BF16), q, k

    def within(ti, dist, attn, q, k):
        rows = pl.ds(pl.multiple_of(ti * t, t), t)
        b = dist[:t]
        b_last = b[t - 1:t, :]
        v = pb_ref[0, rows, PB_V:PB_V + B_WIDTH]
        v_bf = v.astype(BF16)
        v4 = jnp.where(own_val_lanes, jnp.concatenate([v_bf] * B_HEADS, axis=0), zero_v4)
        o = jnp.dot(attn, v4, preferred_element_type=F32)
        o = o + jnp.dot((q * k).astype(BF16), expand, preferred_element_type=F32) * v
        q_dec = (q * jnp.exp2(b)).astype(BF16)
        k_dec = (k * jnp.exp2(b_last - b)).astype(BF16)
        u_t = jnp.dot(v.T.astype(BF16), k_dec, preferred_element_type=F32)
        return o, q_dec, jnp.where(head_t, u_t, 0.0), jnp.exp2(b_last)

    def finish(ti, o):
        rows = pl.ds(pl.multiple_of(ti * t, t), t)
        sq = o * o
        sq_hi = sq.astype(BF16)
        sq_lo = (sq - sq_hi.astype(F32)).astype(BF16)
        ms = jnp.dot(jnp.concatenate([sq_hi, sq_lo], axis=1), head_mean2, preferred_element_type=F32)
        r = pb_ref[0, rows, PB_R:PB_R + B_WIDTH]
        silu = r / (1.0 + jnp.exp(-r))
        y_ref[0, rows, :] = o * lax.rsqrt(ms + EPS) * nw_ref[...] * silu

    def group(gi, state):
        tiles = [gi * GLA_GROUP + j for j in range(GLA_GROUP)]
        dists = [decays(log_a) for log_a in [log_decay(ti) for ti in tiles]]
        weights = [pairs(ti, dist) for ti, dist in zip(tiles, dists)]
        parts = [within(ti, dist, *w) for ti, dist, w in zip(tiles, dists, weights)]
        outs = []
        for o, q_dec, u_t, gate in parts:
            outs.append(o + _nt_dot(q_dec, state.astype(BF16)))
            state = state * gate + u_t
        for ti, o in zip(tiles, outs):
            finish(ti, o)
        return state

    lax.fori_loop(0, seq // (t * GLA_GROUP), group, jnp.zeros((B_WIDTH, B_QK), F32))


def _gla(pb, w2p, b_a, norm_w, batch, seq):
    const = lambda b: (0, 0)
    g_all, lvl = _gla_constants()
    g_all = jnp.asarray(g_all, BF16)
    lvl = jnp.asarray(lvl)
    w_hi = w2p.astype(BF16)
    w_lo = (w2p - w_hi.astype(F32)).astype(BF16)
    w2p = jnp.concatenate([w_hi, w_lo, w_hi], axis=0)
    return pl.pallas_call(
        _gla_kernel,
        grid=(batch,),
        in_specs=[
            pl.BlockSpec((1, seq, PB_WIDTH), lambda b: (b, 0, 0)),
            pl.BlockSpec(w2p.shape, const),
            pl.BlockSpec((1, B_QK), const),
            pl.BlockSpec((1, B_WIDTH), const),
            pl.BlockSpec(g_all.shape, const),
            pl.BlockSpec(lvl.shape, const),
        ],
        out_specs=pl.BlockSpec((1, seq, B_WIDTH), lambda b: (b, 0, 0)),
        out_shape=jax.ShapeDtypeStruct((batch, seq, B_WIDTH), F32),
        compiler_params=pltpu.CompilerParams(vmem_limit_bytes=VMEM_LIMIT),
        name="gla",
    )(pb, w2p, b_a, norm_w, g_all, lvl)


C_AHEAD = 2


def _moba_kernel(q_ref, k_ref, v_ref, bias_ref, o_ref, ka_s, qa_s, va_s):
    seq = q_ref.shape[1]
    nblk = seq // C_BLOCK
    head0 = lax.broadcasted_iota(jnp.int32, (C_BLOCK, LANES), 1) < HEAD_DIM

    k = k_ref[0]
    q = q_ref[0]
    ka_s[:, :LANES] = k.astype(BF16)
    row_blk = lax.broadcasted_iota(jnp.int32, (seq, LANES), 0) // C_BLOCK
    lane_id = lax.broadcasted_iota(jnp.int32, (seq, LANES), 1)
    ka_s[:, LANES:] = jnp.where(lane_id == row_blk, 1.0, 0.0).astype(BF16)
    va_s[:, :LANES] = v_ref[0].astype(BF16)
    va_s[:, LANES:] = jnp.ones((seq, LANES), BF16)
    k_mean = jnp.mean(k.reshape(nblk, C_BLOCK, LANES), axis=1)
    q_scaled = q * (HEAD_DIM ** -0.5 * LOG2E)
    blk = lax.broadcasted_iota(jnp.int32, (nblk, seq), 0)
    q_blk = lax.broadcasted_iota(jnp.int32, (nblk, seq), 1) // C_BLOCK
    for hh in range(PAIR):
        qa_s[hh, :, :LANES] = jnp.where((lane_id < HEAD_DIM) == (hh == 0), q_scaled, 0.0).astype(BF16)
        gate = _nt_dot(k_mean.astype(BF16), qa_s[hh, :, :LANES])
        rank = jnp.zeros((nblk, seq), jnp.int32)
        for m in range(nblk):
            gm = gate[m:m + 1, :]
            beats = (gm > gate) | ((gm == gate) & (m < blk))
            rank = rank + jnp.where(beats & (m < q_blk), 1, 0)
        dropped = (blk < q_blk) & (rank >= C_TOPK)
        pen = jnp.where(dropped, NEG_INF, 0.0)
        assert nblk == SUBLANES
        pen_t = jnp.concatenate([pen, jnp.zeros((LANES - nblk, seq), F32)], axis=0).T
        qa_s[hh, :, LANES:] = pen_t.astype(BF16)

    def logits(qb):
        qrows = slice(qb * C_BLOCK, (qb + 1) * C_BLOCK)
        kdim = slice(0, LANES) if qb <= C_TOPK else slice(0, 2 * LANES)
        q2 = jnp.concatenate([qa_s[0, qrows, kdim], qa_s[1, qrows, kdim]], axis=0)
        return _nt_dot(q2, ka_s[:(qb + 1) * C_BLOCK, kdim]) + bias_ref[0, :, (nblk - 1 - qb) * C_BLOCK:]

    def finish(qb, s):
        m = jnp.max(s, axis=-1, keepdims=True)
        p = jnp.exp2(s - m).astype(BF16)
        r = jnp.dot(p, va_s[:(qb + 1) * C_BLOCK, :], preferred_element_type=F32)
        o_ref[0, qb * C_BLOCK:(qb + 1) * C_BLOCK, :] = jnp.where(
            head0, r[:C_BLOCK, :LANES] / r[:C_BLOCK, LANES:], r[C_BLOCK:, :LANES] / r[C_BLOCK:, LANES:])

    early = [qb for qb in reversed(range(nblk)) if qb <= C_TOPK]
    late = [qb for qb in reversed(range(nblk)) if qb > C_TOPK]
    order = [qb for pair in zip(early, late) for qb in pair] + early[len(late):] + late[len(early):]
    pending = []
    for qb in order:
        pending.append((qb, logits(qb)))
        if len(pending) > C_AHEAD:
            finish(*pending.pop(0))
    for item in pending:
        finish(*item)


def _moba(pc, bias_c, batch, seq):
    npair = C_HEADS // PAIR
    blk = lambda off: pl.BlockSpec((1, seq, LANES), lambda b, p: (b, 0, off + p))
    return pl.pallas_call(
        _moba_kernel,
        grid=(batch, npair),
        in_specs=[
            blk(0), blk(npair), blk(2 * npair),
            pl.BlockSpec((1, PAIR * C_BLOCK, seq), lambda b, p: (p, 0, 0)),
        ],
        out_specs=pl.BlockSpec((1, seq, LANES), lambda b, p: (b, 0, p)),
        out_shape=jax.ShapeDtypeStruct((batch, seq, C_WIDTH), F32),
        scratch_shapes=[
            pltpu.VMEM((seq, 2 * LANES), BF16),
            pltpu.VMEM((PAIR, seq, 2 * LANES), BF16),
            pltpu.VMEM((seq, 2 * LANES), BF16),
        ],
        compiler_params=pltpu.CompilerParams(vmem_limit_bytes=VMEM_LIMIT),
        name="moba_attn",
    )(pc, pc, pc, bias_c)


FF_CHUNK = 1024


def _outffn_kernel(x_ref, ya_ref, yb_ref, yc_ref, woa_ref, wob_ref, woc_ref, n2_ref,
                   w1_ref, w2_ref, fn_ref, o_ref, *, final):
    x = x_ref[...]
    x = x + jnp.dot(ya_ref[...].astype(BF16), woa_ref[...], preferred_element_type=F32)
    x = x + jnp.dot(yb_ref[...].astype(BF16), wob_ref[...], preferred_element_type=F32)
    x = x + jnp.dot(yc_ref[...].astype(BF16), woc_ref[...], preferred_element_type=F32)
    h = _rms(x, n2_ref[...]).astype(BF16)
    ffn = None
    for j in range(D_FF // FF_CHUNK):
        cols = slice(j * FF_CHUNK, (j + 1) * FF_CHUNK)
        a = jnp.dot(h, w1_ref[:, cols], preferred_element_type=F32)
        a = jnp.square(jnp.maximum(a, 0.0)).astype(BF16)
        part = jnp.dot(a, w2_ref[cols, :], preferred_element_type=F32)
        ffn = part if ffn is None else ffn + part
    x = x + ffn
    if final:
        x = _rms(x, fn_ref[...])
    o_ref[...] = x


def _outffn(x2d, ya, yb, yc, woa, wob, woc, n2, w1, w2, fn, tm, final):
    m = x2d.shape[0]
    const = lambda i: (0, 0)
    row = lambda i: (i, 0)
    full = lambda a: pl.BlockSpec(a.shape, const)
    return pl.pallas_call(
        functools.partial(_outffn_kernel, final=final),
        grid=(m // tm,),
        in_specs=[
            pl.BlockSpec((tm, D_MODEL), row),
            pl.BlockSpec((tm, ya.shape[1]), row),
            pl.BlockSpec((tm, yb.shape[1]), row),
            pl.BlockSpec((tm, yc.shape[1]), row),
            full(woa), full(wob), full(woc), full(n2), full(w1), full(w2), full(fn),
        ],
        out_specs=pl.BlockSpec((tm, D_MODEL), row),
        out_shape=jax.ShapeDtypeStruct((m, D_MODEL), F32),
        compiler_params=pltpu.CompilerParams(vmem_limit_bytes=VMEM_LIMIT),
        name="outproj_ffn",
    )(x2d, ya, yb, yc, woa, wob, woc, n2, w1, w2, fn)


def _split_w_in(w):
    sizes = (A_WIDTH, A_WIDTH, A_WIDTH, B_QK, B_QK, B_WIDTH, B_WIDTH, B_GATE_RANK, C_WIDTH, C_WIDTH, C_WIDTH)
    offs = np.concatenate([[0], np.cumsum(sizes)])
    aq, ak, av, bq, bk, bv, br, ba, cq, ck, cv = [w[:, offs[i]:offs[i + 1]] for i in range(len(sizes))]
    ba = jnp.pad(ba, ((0, 0), (0, LANES - B_GATE_RANK)))
    cat = lambda parts: jnp.concatenate(parts, axis=1).astype(BF16)
    return cat([aq, ak, av]), cat([cq, ck, cv]), cat([bq, bk, bv, br, ba])


def kernel(x, norm1_w, w_in, gla_w_a2, gla_b_a, gla_norm_w, w_out, norm2_w, w_ff1, w_ff2, rel_bias, final_norm_w):
    batch, seq, d = x.shape
    depth = w_in.shape[0]
    assert d == D_MODEL and seq % C_BLOCK == 0 and seq % (A_BLK * A_BRANCHES[-1][1]) == 0
    tokens = batch * seq
    tm = 512 if tokens % 512 == 0 else seq
    bias_a, bias_c = _bias_tiles(rel_bias, seq)
    x2d = x.reshape(tokens, d)
    fn = final_norm_w.reshape(1, d)
    for i in range(depth):
        wa, wc, wb = _split_w_in(w_in[i])
        pa, pc, pb = _inproj(x2d, norm1_w[i].reshape(1, d), wa, wc, wb, tm)
        ya = _dilated(pa.reshape(batch, seq, -1), bias_a, batch, seq)
        w2p = jnp.pad(gla_w_a2[i], ((0, LANES - B_GATE_RANK), (0, 0)))
        yb = _gla(pb.reshape(batch, seq, -1), w2p, gla_b_a[i].reshape(1, B_QK),
                  gla_norm_w[i].reshape(1, B_WIDTH), batch, seq)
        yc = _moba(pc.reshape(batch, seq, -1), bias_c, batch, seq)
        wo = w_out[i].astype(BF16)
        x2d = _outffn(
            x2d, ya.reshape(tokens, -1), yb.reshape(tokens, -1), yc.reshape(tokens, -1),
            wo[:A_WIDTH], wo[A_WIDTH:A_WIDTH + B_WIDTH], wo[A_WIDTH + B_WIDTH:],
            norm2_w[i].reshape(1, d), w_ff1[i].astype(BF16), w_ff2[i].astype(BF16), fn,
            tm, final=(i == depth - 1))
    return x2d.reshape(batch, seq, d)
```

```python
import functools
import math

import numpy as np
import jax
import jax.numpy as jnp
from jax import lax
from jax.experimental import pallas as pl
from jax.experimental.pallas import tpu as pltpu

D_MODEL = 1024
HEAD_DIM = 64
A_HEADS = 6
A_BRANCHES = ((128, 1), (512, 4), (2048, 16))
B_HEADS = 4
B_KEY_DIM = 32
B_VAL_DIM = 64
B_GATE_RANK = 16
B_GATE_TAU = 16.0
C_HEADS = 6
C_BLOCK = 256
C_TOPK = 3
REL_BUCKETS = 32
REL_MAX_DIST = 2048
D_FF = 4 * D_MODEL
EPS = 1e-6
NEG_INF = -1e30
LOG2E = math.log2(math.e)

A_WIDTH = A_HEADS * HEAD_DIM
B_QK = B_HEADS * B_KEY_DIM
B_WIDTH = B_HEADS * B_VAL_DIM
C_WIDTH = C_HEADS * HEAD_DIM

LANES = 128
SUBLANES = 8
A_BLK = 128
PAIR = LANES // HEAD_DIM
VMEM_LIMIT = 56 * 1024 * 1024
F32 = jnp.float32
BF16 = jnp.bfloat16


def _bucket_thresholds():
    n = np.arange(0, REL_MAX_DIST + 1)
    exact = REL_BUCKETS // 2
    logv = (np.log(np.maximum(n, 1).astype(np.float32) / np.float32(exact))
            / np.float32(math.log(REL_MAX_DIST / exact))).astype(np.float32)
    large = np.minimum(exact + (logv * np.float32(REL_BUCKETS - exact)).astype(np.int32), REL_BUCKETS - 1)
    bucket = np.where(n < exact, n, large)
    assert np.all(np.diff(bucket) >= 0)
    return [int(np.argmax(bucket >= k)) for k in range(REL_BUCKETS)]


_THRESH = _bucket_thresholds()


def _nt_dot(a, b):
    return lax.dot_general(a, b, (((1,), (1,)), ((), ())), preferred_element_type=F32)


def _rms(x, w):
    return x * lax.rsqrt(jnp.mean(x * x, axis=-1, keepdims=True) + EPS) * w


def _bias_lookup(dist, rb_ref, col):
    val = jnp.full(dist.shape, rb_ref[0, col], F32)
    for k in range(1, REL_BUCKETS):
        val = jnp.where(dist >= _THRESH[k], rb_ref[k, col], val)
    return val


def _bias_kernel(rb_ref, ba_ref, bc_ref):
    h = pl.program_id(0)
    qi = lax.broadcasted_iota(jnp.int32, (A_BLK, 2 * A_BLK), 0)
    kj = lax.broadcasted_iota(jnp.int32, (A_BLK, 2 * A_BLK), 1)
    steps = qi + A_BLK - kj
    band = (steps >= 0) & (steps <= A_BLK)
    for g, (_, dil) in enumerate(A_BRANCHES):
        ba_ref[0, g] = jnp.where(band, _bias_lookup(steps * dil, rb_ref, h) * LOG2E, NEG_INF)
    qi = lax.broadcasted_iota(jnp.int32, (C_BLOCK, C_BLOCK), 0)
    kj = lax.broadcasted_iota(jnp.int32, (C_BLOCK, C_BLOCK), 1)
    nblk = bc_ref.shape[2] // C_BLOCK
    for delta in range(nblk):
        dist = delta * C_BLOCK + qi - kj
        bias = _bias_lookup(dist, rb_ref, A_HEADS + h) * LOG2E
        if delta == 0:
            bias = jnp.where(dist >= 0, bias, NEG_INF)
        bc_ref[0, :, (nblk - 1 - delta) * C_BLOCK:(nblk - delta) * C_BLOCK] = bias


def _bias_tiles(rel_bias, seq):
    return pl.pallas_call(
        _bias_kernel,
        grid=(A_HEADS,),
        in_specs=[pl.BlockSpec(memory_space=pltpu.SMEM)],
        out_specs=[
            pl.BlockSpec((1, len(A_BRANCHES), A_BLK, 2 * A_BLK), lambda h: (h // PAIR, 0, h % PAIR, 0)),
            pl.BlockSpec((1, C_BLOCK, seq), lambda h: (h // PAIR, h % PAIR, 0)),
        ],
        out_shape=[
            jax.ShapeDtypeStruct((A_HEADS // PAIR, len(A_BRANCHES), PAIR * A_BLK, 2 * A_BLK), F32),
            jax.ShapeDtypeStruct((C_HEADS // PAIR, PAIR * C_BLOCK, seq), F32),
        ],
        name="bias_tiles",
    )(rel_bias)


def _inproj_kernel(x_ref, nw_ref, wa_ref, wc_ref, wb_ref, pa_ref, pc_ref, pb_ref):
    h = _rms(x_ref[...], nw_ref[...]).astype(BF16)
    pa_ref[...] = jnp.dot(h, wa_ref[...], preferred_element_type=F32)
    pc_ref[...] = jnp.dot(h, wc_ref[...], preferred_element_type=F32)
    pb_ref[...] = jnp.dot(h, wb_ref[...], preferred_element_type=F32)


def _inproj(x2d, nw, wa, wc, wb, tm):
    m = x2d.shape[0]
    const = lambda i: (0, 0)
    row = lambda i: (i, 0)
    return pl.pallas_call(
        _inproj_kernel,
        grid=(m // tm,),
        in_specs=[
            pl.BlockSpec((tm, D_MODEL), row),
            pl.BlockSpec((1, D_MODEL), const),
            pl.BlockSpec(wa.shape, const),
            pl.BlockSpec(wc.shape, const),
            pl.BlockSpec(wb.shape, const),
        ],
        out_specs=[
            pl.BlockSpec((tm, wa.shape[1]), row),
            pl.BlockSpec((tm, wc.shape[1]), row),
            pl.BlockSpec((tm, wb.shape[1]), row),
        ],
        out_shape=[
            jax.ShapeDtypeStruct((m, wa.shape[1]), F32),
            jax.ShapeDtypeStruct((m, wc.shape[1]), F32),
            jax.ShapeDtypeStruct((m, wb.shape[1]), F32),
        ],
        compiler_params=pltpu.CompilerParams(vmem_limit_bytes=VMEM_LIMIT),
        name="inproj",
    )(x2d, nw, wa, wc, wb)


A_QSCALE = HEAD_DIM ** -0.5 * LOG2E
A_AHEAD = 5


def _dilated_kernel(q_ref, k_ref, v_ref, bias_ref, o_ref, qs_s, ks_s, vs_s, x_s, m_s, l_s, acc_s):
    seq = q_ref.shape[1]
    (w1, d1), (w2, d2), (w3, d3) = A_BRANCHES
    assert d1 == 1 and d3 == d2 * d2 and all(w // d == A_BLK for w, d in A_BRANCHES)
    coarse = seq // d2
    lane = lax.broadcasted_iota(jnp.int32, (A_BLK, LANES), 1)
    head0 = lane < HEAD_DIM
    vs_s[:, :, LANES:] = jnp.ones((len(A_BRANCHES), seq, LANES), BF16)

    def put(g, dst, q, k, v):
        lane_l = lax.broadcasted_iota(jnp.int32, q.shape, 1)
        qs_s[g, 0, dst, :] = jnp.where(lane_l < HEAD_DIM, q, 0.0).astype(BF16)
        qs_s[g, 1, dst, :] = jnp.where(lane_l < HEAD_DIM, 0.0, q).astype(BF16)
        ks_s[g, dst, :] = k.astype(BF16)
        vs_s[g, dst, :LANES] = v.astype(BF16)

    def logits(g, bi, first):
        qrows = pl.ds(bi * A_BLK, A_BLK)
        q2 = jnp.concatenate([qs_s[g, 0, qrows, :], qs_s[g, 1, qrows, :]], axis=0)
        if first:
            krows = qrows
            bias = bias_ref[0, g, :, A_BLK:]
        else:
            krows = pl.ds((bi - 1) * A_BLK, 2 * A_BLK)
            bias = bias_ref[0, g]
        return _nt_dot(q2, ks_s[g, krows, :]) + bias, krows

    def finish(g, s, krows, out_rows):
        m = jnp.max(s, axis=-1, keepdims=True)
        p = jnp.exp2(s - m).astype(BF16)
        r = jnp.dot(p, vs_s[g, krows, :], preferred_element_type=F32)
        m_s[g, out_rows, :] = jnp.where(head0, m[:A_BLK], m[A_BLK:])
        l_s[g, out_rows, :] = jnp.where(head0, r[:A_BLK, LANES:], r[A_BLK:, LANES:])
        acc_s[g, out_rows, :] = jnp.where(head0, r[:A_BLK, :LANES], r[A_BLK:, :LANES])

    def run(blocks):
        pending = []
        for g, bi, first, out_rows in blocks:
            pending.append((g, logits(g, bi, first), out_rows))
            if len(pending) > A_AHEAD:
                g0, (s, krows), rows = pending.pop(0)
                finish(g0, s, krows, rows)
        for g0, (s, krows), rows in pending:
            finish(g0, s, krows, rows)

    put(0, pl.ds(0, seq), q_ref[0] * A_QSCALE, k_ref[0], v_ref[0])
    for r in range(d2):
        src = pl.ds(r, coarse, stride=d2)
        dst = pl.ds(r * coarse, coarse)
        q, k, v = q_ref[0, src, :] * A_QSCALE, k_ref[0, src, :], v_ref[0, src, :]
        x_s[0, dst, :], x_s[1, dst, :], x_s[2, dst, :] = q, k, v
        put(1, dst, q, k, v)
    fine = seq // d3
    for r in range(d3):
        src = pl.ds((r % d2) * coarse + r // d2, fine, stride=d2)
        put(2, pl.ds(r * fine, fine), x_s[0, src, :], x_s[1, src, :], x_s[2, src, :])

    nb1, nb2 = coarse // A_BLK, fine // A_BLK
    run([(0, bi, bi == 0, pl.ds(bi * A_BLK, A_BLK)) for bi in range(seq // A_BLK)]
        + [(1, bi, bi % nb1 == 0, pl.ds(bi * A_BLK, A_BLK)) for bi in range(d2 * nb1)]
        + [(2, r * nb2 + n, n == 0, pl.ds((r % d2) * coarse + r // d2 + n * A_BLK * d2, A_BLK, stride=d2))
           for r in range(d3) for n in range(nb2)])

    for r in range(d2):
        rows = pl.ds(r * coarse, coarse)
        tok = pl.ds(r, coarse, stride=d2)
        m1, m2, m3 = m_s[0, tok, :], m_s[1, rows, :], m_s[2, rows, :]
        m = jnp.maximum(jnp.maximum(m1, m2), m3)
        e1, e2, e3 = jnp.exp2(m1 - m), jnp.exp2(m2 - m), jnp.exp2(m3 - m)
        num = e1 * acc_s[0, tok, :] + e2 * acc_s[1, rows, :] + e3 * acc_s[2, rows, :]
        den = e1 * l_s[0, tok, :] + e2 * l_s[1, rows, :] + e3 * l_s[2, rows, :]
        o_ref[0, tok, :] = num / den


def _dilated(pa, bias_a, batch, seq):
    npair = A_HEADS // PAIR
    nbr = len(A_BRANCHES)
    blk = lambda off: pl.BlockSpec((1, seq, LANES), lambda p, b: (b, 0, off + p))
    return pl.pallas_call(
        _dilated_kernel,
        grid=(npair, batch),
        in_specs=[
            blk(0), blk(npair), blk(2 * npair),
            pl.BlockSpec((1, nbr, PAIR * A_BLK, 2 * A_BLK), lambda p, b: (p, 0, 0, 0)),
        ],
        out_specs=pl.BlockSpec((1, seq, LANES), lambda p, b: (b, 0, p)),
        out_shape=jax.ShapeDtypeStruct((batch, seq, A_WIDTH), F32),
        scratch_shapes=[
            pltpu.VMEM((nbr, PAIR, seq, LANES), BF16),
            pltpu.VMEM((nbr, seq, LANES), BF16),
            pltpu.VMEM((nbr, seq, 2 * LANES), BF16),
            pltpu.VMEM((3, seq, LANES), F32),
            pltpu.VMEM((nbr, seq, LANES), F32),
            pltpu.VMEM((nbr, seq, LANES), F32),
            pltpu.VMEM((nbr, seq, LANES), F32),
        ],
        compiler_params=pltpu.CompilerParams(vmem_limit_bytes=VMEM_LIMIT),
        name="dilated_attn",
    )(pa, pa, pa, bias_a)


GLA_TILE = 128
GLA_LEVELS = GLA_TILE.bit_length() - 1
GLA_GROUP = 16
PB_Q, PB_K, PB_V, PB_R, PB_A = 0, B_QK, 2 * B_QK, 2 * B_QK + B_WIDTH, 2 * B_QK + 2 * B_WIDTH
PB_WIDTH = PB_A + LANES


def _log_sigmoid(z):
    return jnp.minimum(z, 0.0) - jnp.log1p(jnp.exp(-jnp.abs(z)))


def _gla_constants():
    t = GLA_TILE
    i = np.arange(t)[:, None]
    c = np.arange(t)[None, :]
    blocks = [(c <= i).astype(np.float32)]
    for lvl in range(1, GLA_LEVELS + 1):
        w = t >> lvl
        split = (i // (2 * w)) * (2 * w) + w - 1
        blocks.append((c <= i).astype(np.float32) - (c <= split).astype(np.float32))
    x = np.arange(t)[:, None] ^ np.arange(t)[None, :]
    level_of_pair = np.where(np.arange(t)[None, :] < np.arange(t)[:, None],
                             np.floor(np.log2(np.maximum(x, 1))).astype(np.int32), -1)
    return np.concatenate(blocks, axis=0), np.tile(level_of_pair, (1, B_HEADS)).astype(np.int32)


def _gla_kernel(pb_ref, w2_ref, ba_ref, nw_ref, g_ref, lvl_ref, y_ref):
    seq = pb_ref.shape[1]
    t = GLA_TILE
    kl = lax.broadcasted_iota(jnp.int32, (B_QK, B_WIDTH), 0) // B_KEY_DIM
    vl = lax.broadcasted_iota(jnp.int32, (B_QK, B_WIDTH), 1) // B_VAL_DIM
    expand = jnp.where(kl == vl, 1.0, 0.0).astype(BF16)
    vlt = lax.broadcasted_iota(jnp.int32, (B_WIDTH, B_QK), 0) // B_VAL_DIM
    klt = lax.broadcasted_iota(jnp.int32, (B_WIDTH, B_QK), 1) // B_KEY_DIM
    head_t = vlt == klt
    va = (lax.broadcasted_iota(jnp.int32, (2 * B_WIDTH, B_WIDTH), 0) % B_WIDTH) // B_VAL_DIM
    vb = lax.broadcasted_iota(jnp.int32, (2 * B_WIDTH, B_WIDTH), 1) // B_VAL_DIM
    head_mean2 = jnp.where(va == vb, 1.0 / B_VAL_DIM, 0.0).astype(BF16)
    row = lax.broadcasted_iota(jnp.int32, (t, B_QK), 0)
    rep_head_k = lax.broadcasted_iota(jnp.int32, (B_HEADS * t, B_QK), 0) // t
    own_key_lanes = rep_head_k == lax.broadcasted_iota(jnp.int32, (B_HEADS * t, B_QK), 1) // B_KEY_DIM
    rep_head_v = lax.broadcasted_iota(jnp.int32, (B_HEADS * t, B_WIDTH), 0) // t
    own_val_lanes = rep_head_v == lax.broadcasted_iota(jnp.int32, (B_HEADS * t, B_WIDTH), 1) // B_VAL_DIM
    zero_k4 = jnp.zeros((B_HEADS * t, B_QK), BF16)
    zero_v4 = jnp.zeros((B_HEADS * t, B_WIDTH), BF16)

    def log_decay(ti):
        rows = pl.ds(pl.multiple_of(ti * t, t), t)
        a = pb_ref[0, rows, PB_A:PB_A + LANES]
        a_hi = a.astype(BF16)
        a_lo = (a - a_hi.astype(F32)).astype(BF16)
        z = jnp.dot(jnp.concatenate([a_hi, a_hi, a_lo], axis=1), w2_ref[...], preferred_element_type=F32)
        return _log_sigmoid(z + ba_ref[...]) * (LOG2E / B_GATE_TAU)

    def decays(log_a):
        a1 = log_a.astype(BF16)
        a2 = (log_a - a1.astype(F32)).astype(BF16)
        dd = jnp.dot(g_ref[...], jnp.concatenate([a1, a2], axis=1), preferred_element_type=F32)
        return dd[:, :B_QK] + dd[:, B_QK:]

    def pairs(ti, dist):
        rows = pl.ds(pl.multiple_of(ti * t, t), t)
        q = pb_ref[0, rows, PB_Q:PB_Q + B_QK] * (B_KEY_DIM ** -0.5)
        k = pb_ref[0, rows, PB_K:PB_K + B_QK]
        attn = jnp.zeros((t, B_HEADS * t), F32)
        for lvl in range(1, GLA_LEVELS + 1):
            w = t >> lvl
            decay = jnp.exp2(-jnp.abs(dist[lvl * t:(lvl + 1) * t]))
            after = (row & w) != 0
            q_l = (jnp.where(after, q * decay, 0.0)).astype(BF16)
            k_l = jnp.where(after, 0.0, k * decay).astype(BF16)
            k4 = jnp.where(own_key_lanes, jnp.concatenate([k_l] * B_HEADS, axis=0), zero_k4)
            attn = jnp.where(lvl_ref[...] == GLA_LEVELS - lvl, _nt_dot(q_l, k4), attn)
        return attn.astype(BF16), q, k

    def within(ti, dist, attn, q, k):
        rows = pl.ds(pl.multiple_of(ti * t, t), t)
        b = dist[:t]
        b_last = b[t - 1:t, :]
        v = pb_ref[0, rows, PB_V:PB_V + B_WIDTH]
        v_bf = v.astype(BF16)
        v4 = jnp.where(own_val_lanes, jnp.concatenate([v_bf] * B_HEADS, axis=0), zero_v4)
        o = jnp.dot(attn, v4, preferred_element_type=F32)
        o = o + jnp.dot((q * k).astype(BF16), expand, preferred_element_type=F32) * v
        q_dec = (q * jnp.exp2(b)).astype(BF16)
        k_dec = (k * jnp.exp2(b_last - b)).astype(BF16)
        u_t = jnp.dot(v.T.astype(BF16), k_dec, preferred_element_type=F32)
        return o, q_dec, jnp.where(head_t, u_t, 0.0), jnp.exp2(b_last)

    def finish(ti, o):
        rows = pl.ds(pl.multiple_of(ti * t, t), t)
        sq = o * o
        sq_hi = sq.astype(BF16)
        sq_lo = (sq - sq_hi.astype(F32)).astype(BF16)
        ms = jnp.dot(jnp.concatenate([sq_hi, sq_lo], axis=1), head_mean2, preferred_element_type=F32)
        r = pb_ref[0, rows, PB_R:PB_R + B_WIDTH]
        silu = r / (1.0 + jnp.exp(-r))
        y_ref[0, rows, :] = o * lax.rsqrt(ms + EPS) * nw_ref[...] * silu

    def group(gi, state):
        tiles = [gi * GLA_GROUP + j for j in range(GLA_GROUP)]
        dists = [decays(log_a) for log_a in [log_decay(ti) for ti in tiles]]
        weights = [pairs(ti, dist) for ti, dist in zip(tiles, dists)]
        parts = [within(ti, dist, *w) for ti, dist, w in zip(tiles, dists, weights)]
        outs = []
        for o, q_dec, u_t, gate in parts:
            outs.append(o + _nt_dot(q_dec, state.astype(BF16)))
            state = state * gate + u_t
        for ti, o in zip(tiles, outs):
            finish(ti, o)
        return state

    lax.fori_loop(0, seq // (t * GLA_GROUP), group, jnp.zeros((B_WIDTH, B_QK), F32))


def _gla(pb, w2p, b_a, norm_w, batch, seq):
    const = lambda b: (0, 0)
    g_all, lvl = _gla_constants()
    g_all = jnp.asarray(g_all, BF16)
    lvl = jnp.asarray(lvl)
    w_hi = w2p.astype(BF16)
    w_lo = (w2p - w_hi.astype(F32)).astype(BF16)
    w2p = jnp.concatenate([w_hi, w_lo, w_hi], axis=0)
    return pl.pallas_call(
        _gla_kernel,
        grid=(batch,),
        in_specs=[
            pl.BlockSpec((1, seq, PB_WIDTH), lambda b: (b, 0, 0)),
            pl.BlockSpec(w2p.shape, const),
            pl.BlockSpec((1, B_QK), const),
            pl.BlockSpec((1, B_WIDTH), const),
            pl.BlockSpec(g_all.shape, const),
            pl.BlockSpec(lvl.shape, const),
        ],
        out_specs=pl.BlockSpec((1, seq, B_WIDTH), lambda b: (b, 0, 0)),
        out_shape=jax.ShapeDtypeStruct((batch, seq, B_WIDTH), F32),
        compiler_params=pltpu.CompilerParams(vmem_limit_bytes=VMEM_LIMIT),
        name="gla",
    )(pb, w2p, b_a, norm_w, g_all, lvl)


C_AHEAD = 2


def _moba_kernel(q_ref, k_ref, v_ref, bias_ref, o_ref, ka_s, qa_s, va_s):
    seq = q_ref.shape[1]
    nblk = seq // C_BLOCK
    head0 = lax.broadcasted_iota(jnp.int32, (C_BLOCK, LANES), 1) < HEAD_DIM

    k = k_ref[0]
    q = q_ref[0]
    ka_s[:, :LANES] = k.astype(BF16)
    row_blk = lax.broadcasted_iota(jnp.int32, (seq, LANES), 0) // C_BLOCK
    lane_id = lax.broadcasted_iota(jnp.int32, (seq, LANES), 1)
    ka_s[:, LANES:] = jnp.where(lane_id == row_blk, 1.0, 0.0).astype(BF16)
    va_s[:, :LANES] = v_ref[0].astype(BF16)
    va_s[:, LANES:] = jnp.ones((seq, LANES), BF16)
    k_mean = jnp.mean(k.reshape(nblk, C_BLOCK, LANES), axis=1)
    q_scaled = q * (HEAD_DIM ** -0.5 * LOG2E)
    blk = lax.broadcasted_iota(jnp.int32, (nblk, seq), 0)
    q_blk = lax.broadcasted_iota(jnp.int32, (nblk, seq), 1) // C_BLOCK
    for hh in range(PAIR):
        qa_s[hh, :, :LANES] = jnp.where((lane_id < HEAD_DIM) == (hh == 0), q_scaled, 0.0).astype(BF16)
        gate = _nt_dot(k_mean.astype(BF16), qa_s[hh, :, :LANES])
        rank = jnp.zeros((nblk, seq), jnp.int32)
        for m in range(nblk):
            gm = gate[m:m + 1, :]
            beats = (gm > gate) | ((gm == gate) & (m < blk))
            rank = rank + jnp.where(beats & (m < q_blk), 1, 0)
        dropped = (blk < q_blk) & (rank >= C_TOPK)
        pen = jnp.where(dropped, NEG_INF, 0.0)
        assert nblk == SUBLANES
        pen_t = jnp.concatenate([pen, jnp.zeros((LANES - nblk, seq), F32)], axis=0).T
        qa_s[hh, :, LANES:] = pen_t.astype(BF16)

    def logits(qb):
        qrows = slice(qb * C_BLOCK, (qb + 1) * C_BLOCK)
        kdim = slice(0, LANES) if qb <= C_TOPK else slice(0, 2 * LANES)
        q2 = jnp.concatenate([qa_s[0, qrows, kdim], qa_s[1, qrows, kdim]], axis=0)
        return _nt_dot(q2, ka_s[:(qb + 1) * C_BLOCK, kdim]) + bias_ref[0, :, (nblk - 1 - qb) * C_BLOCK:]

    def finish(qb, s):
        m = jnp.max(s, axis=-1, keepdims=True)
        p = jnp.exp2(s - m).astype(BF16)
        r = jnp.dot(p, va_s[:(qb + 1) * C_BLOCK, :], preferred_element_type=F32)
        o_ref[0, qb * C_BLOCK:(qb + 1) * C_BLOCK, :] = jnp.where(
            head0, r[:C_BLOCK, :LANES] / r[:C_BLOCK, LANES:], r[C_BLOCK:, :LANES] / r[C_BLOCK:, LANES:])

    early = [qb for qb in reversed(range(nblk)) if qb <= C_TOPK]
    late = [qb for qb in reversed(range(nblk)) if qb > C_TOPK]
    order = [qb for pair in zip(early, late) for qb in pair] + early[len(late):] + late[len(early):]
    pending = []
    for qb in order:
        pending.append((qb, logits(qb)))
        if len(pending) > C_AHEAD:
            finish(*pending.pop(0))
    for item in pending:
        finish(*item)


def _moba(pc, bias_c, batch, seq):
    npair = C_HEADS // PAIR
    blk = lambda off: pl.BlockSpec((1, seq, LANES), lambda p, b: (b, 0, off + p))
    return pl.pallas_call(
        _moba_kernel,
        grid=(npair, batch),
        in_specs=[
            blk(0), blk(npair), blk(2 * npair),
            pl.BlockSpec((1, PAIR * C_BLOCK, seq), lambda p, b: (p, 0, 0)),
        ],
        out_specs=pl.BlockSpec((1, seq, LANES), lambda p, b: (b, 0, p)),
        out_shape=jax.ShapeDtypeStruct((batch, seq, C_WIDTH), F32),
        scratch_shapes=[
            pltpu.VMEM((seq, 2 * LANES), BF16),
            pltpu.VMEM((PAIR, seq, 2 * LANES), BF16),
            pltpu.VMEM((seq, 2 * LANES), BF16),
        ],
        compiler_params=pltpu.CompilerParams(vmem_limit_bytes=VMEM_LIMIT),
        name="moba_attn",
    )(pc, pc, pc, bias_c)


FF_CHUNK = 1024


def _outffn_kernel(x_ref, ya_ref, yb_ref, yc_ref, woa_ref, wob_ref, woc_ref, n2_ref,
                   w1_ref, w2_ref, fn_ref, o_ref, *, final):
    x = x_ref[...]
    x = x + jnp.dot(ya_ref[...].astype(BF16), woa_ref[...], preferred_element_type=F32)
    x = x + jnp.dot(yb_ref[...].astype(BF16), wob_ref[...], preferred_element_type=F32)
    x = x + jnp.dot(yc_ref[...].astype(BF16), woc_ref[...], preferred_element_type=F32)
    h = _rms(x, n2_ref[...]).astype(BF16)
    ffn = None
    for j in range(D_FF // FF_CHUNK):
        cols = slice(j * FF_CHUNK, (j + 1) * FF_CHUNK)
        a = jnp.dot(h, w1_ref[:, cols], preferred_element_type=F32)
        a = jnp.square(jnp.maximum(a, 0.0)).astype(BF16)
        part = jnp.dot(a, w2_ref[cols, :], preferred_element_type=F32)
        ffn = part if ffn is None else ffn + part
    x = x + ffn
    if final:
        x = _rms(x, fn_ref[...])
    o_ref[...] = x


def _outffn(x2d, ya, yb, yc, woa, wob, woc, n2, w1, w2, fn, tm, final):
    m = x2d.shape[0]
    const = lambda i: (0, 0)
    row = lambda i: (i, 0)
    full = lambda a: pl.BlockSpec(a.shape, const)
    return pl.pallas_call(
        functools.partial(_outffn_kernel, final=final),
        grid=(m // tm,),
        in_specs=[
            pl.BlockSpec((tm, D_MODEL), row),
            pl.BlockSpec((tm, ya.shape[1]), row),
            pl.BlockSpec((tm, yb.shape[1]), row),
            pl.BlockSpec((tm, yc.shape[1]), row),
            full(woa), full(wob), full(woc), full(n2), full(w1), full(w2), full(fn),
        ],
        out_specs=pl.BlockSpec((tm, D_MODEL), row),
        out_shape=jax.ShapeDtypeStruct((m, D_MODEL), F32),
        compiler_params=pltpu.CompilerParams(vmem_limit_bytes=VMEM_LIMIT),
        name="outproj_ffn",
    )(x2d, ya, yb, yc, woa, wob, woc, n2, w1, w2, fn)


def _split_w_in(w):
    sizes = (A_WIDTH, A_WIDTH, A_WIDTH, B_QK, B_QK, B_WIDTH, B_WIDTH, B_GATE_RANK, C_WIDTH, C_WIDTH, C_WIDTH)
    offs = np.concatenate([[0], np.cumsum(sizes)])
    aq, ak, av, bq, bk, bv, br, ba, cq, ck, cv = [w[:, offs[i]:offs[i + 1]] for i in range(len(sizes))]
    ba = jnp.pad(ba, ((0, 0), (0, LANES - B_GATE_RANK)))
    cat = lambda parts: jnp.concatenate(parts, axis=1).astype(BF16)
    return cat([aq, ak, av]), cat([cq, ck, cv]), cat([bq, bk, bv, br, ba])


def kernel(x, norm1_w, w_in, gla_w_a2, gla_b_a, gla_norm_w, w_out, norm2_w, w_ff1, w_ff2, rel_bias, final_norm_w):
    batch, seq, d = x.shape
    depth = w_in.shape[0]
    assert d == D_MODEL and seq % C_BLOCK == 0 and seq % (A_BLK * A_BRANCHES[-1][1]) == 0
    tokens = batch * seq
    tm = 512 if tokens % 512 == 0 else seq
    bias_a, bias_c = _bias_tiles(rel_bias, seq)
    x2d = x.reshape(tokens, d)
    fn = final_norm_w.reshape(1, d)
    for i in range(depth):
        wa, wc, wb = _split_w_in(w_in[i])
        pa, pc, pb = _inproj(x2d, norm1_w[i].reshape(1, d), wa, wc, wb, tm)
        ya = _dilated(pa.reshape(batch, seq, -1), bias_a, batch, seq)
        w2p = jnp.pad(gla_w_a2[i], ((0, LANES - B_GATE_RANK), (0, 0)))
        yb = _gla(pb.reshape(batch, seq, -1), w2p, gla_b_a[i].reshape(1, B_QK),
                  gla_norm_w[i].reshape(1, B_WIDTH), batch, seq)
        yc = _moba(pc.reshape(batch, seq, -1), bias_c, batch, seq)
        wo = w_out[i].astype(BF16)
        x2d = _outffn(
            x2d, ya.reshape(tokens, -1), yb.reshape(tokens, -1), yc.reshape(tokens, -1),
            wo[:A_WIDTH], wo[A_WIDTH:A_WIDTH + B_WIDTH], wo[A_WIDTH + B_WIDTH:],
            norm2_w[i].reshape(1, d), w_ff1[i].astype(BF16), w_ff2[i].astype(BF16), fn,
            tm, final=(i == depth - 1))
    return x2d.reshape(batch, seq, d)
```

```python
import functools
import math

import numpy as np
import jax
import jax.numpy as jnp
from jax import lax
from jax.experimental import pallas as pl
from jax.experimental.pallas import tpu as pltpu

D_MODEL = 1024
HEAD_DIM = 64
A_HEADS = 6
A_BRANCHES = ((128, 1), (512, 4), (2048, 16))
B_HEADS = 4
B_KEY_DIM = 32
B_VAL_DIM = 64
B_GATE_RANK = 16
B_GATE_TAU = 16.0
C_HEADS = 6
C_BLOCK = 256
C_TOPK = 3
REL_BUCKETS = 32
REL_MAX_DIST = 2048
D_FF = 4 * D_MODEL
EPS = 1e-6
NEG_INF = -1e30
LOG2E = math.log2(math.e)

A_WIDTH = A_HEADS * HEAD_DIM
B_QK = B_HEADS * B_KEY_DIM
B_WIDTH = B_HEADS * B_VAL_DIM
C_WIDTH = C_HEADS * HEAD_DIM

LANES = 128
SUBLANES = 8
A_BLK = 128
PAIR = LANES // HEAD_DIM
VMEM_LIMIT = 56 * 1024 * 1024
F32 = jnp.float32
BF16 = jnp.bfloat16


def _bucket_thresholds():
    n = np.arange(0, REL_MAX_DIST + 1)
    exact = REL_BUCKETS // 2
    logv = (np.log(np.maximum(n, 1).astype(np.float32) / np.float32(exact))
            / np.float32(math.log(REL_MAX_DIST / exact))).astype(np.float32)
    large = np.minimum(exact + (logv * np.float32(REL_BUCKETS - exact)).astype(np.int32), REL_BUCKETS - 1)
    bucket = np.where(n < exact, n, large)
    assert np.all(np.diff(bucket) >= 0)
    return [int(np.argmax(bucket >= k)) for k in range(REL_BUCKETS)]


_THRESH = _bucket_thresholds()


def _nt_dot(a, b):
    return lax.dot_general(a, b, (((1,), (1,)), ((), ())), preferred_element_type=F32)


def _rms(x, w):
    return x * lax.rsqrt(jnp.mean(x * x, axis=-1, keepdims=True) + EPS) * w


def _bias_lookup(dist, rb_ref, col):
    val = jnp.full(dist.shape, rb_ref[0, col], F32)
    for k in range(1, REL_BUCKETS):
        val = jnp.where(dist >= _THRESH[k], rb_ref[k, col], val)
    return val


def _bias_kernel(rb_ref, ba_ref, bc_ref):
    h = pl.program_id(0)
    qi = lax.broadcasted_iota(jnp.int32, (A_BLK, 2 * A_BLK), 0)
    kj = lax.broadcasted_iota(jnp.int32, (A_BLK, 2 * A_BLK), 1)
    steps = qi + A_BLK - kj
    band = (steps >= 0) & (steps <= A_BLK)
    for g, (_, dil) in enumerate(A_BRANCHES):
        ba_ref[0, g] = jnp.where(band, _bias_lookup(steps * dil, rb_ref, h) * LOG2E, NEG_INF)
    qi = lax.broadcasted_iota(jnp.int32, (C_BLOCK, C_BLOCK), 0)
    kj = lax.broadcasted_iota(jnp.int32, (C_BLOCK, C_BLOCK), 1)
    nblk = bc_ref.shape[2] // C_BLOCK
    for delta in range(nblk):
        dist = delta * C_BLOCK + qi - kj
        bias = _bias_lookup(dist, rb_ref, A_HEADS + h) * LOG2E
        if delta == 0:
            bias = jnp.where(dist >= 0, bias, NEG_INF)
        bc_ref[0, :, (nblk - 1 - delta) * C_BLOCK:(nblk - delta) * C_BLOCK] = bias


def _bias_tiles(rel_bias, seq):
    return pl.pallas_call(
        _bias_kernel,
        grid=(A_HEADS,),
        in_specs=[pl.BlockSpec(memory_space=pltpu.SMEM)],
        out_specs=[
            pl.BlockSpec((1, len(A_BRANCHES), A_BLK, 2 * A_BLK), lambda h: (h // PAIR, 0, h % PAIR, 0)),
            pl.BlockSpec((1, C_BLOCK, seq), lambda h: (h // PAIR, h % PAIR, 0)),
        ],
        out_shape=[
            jax.ShapeDtypeStruct((A_HEADS // PAIR, len(A_BRANCHES), PAIR * A_BLK, 2 * A_BLK), F32),
            jax.ShapeDtypeStruct((C_HEADS // PAIR, PAIR * C_BLOCK, seq), F32),
        ],
        name="bias_tiles",
    )(rel_bias)


def _inproj_kernel(x_ref, nw_ref, wa_ref, wc_ref, wb_ref, pa_ref, pc_ref, pb_ref):
    h = _rms(x_ref[...], nw_ref[...]).astype(BF16)
    pa_ref[...] = jnp.dot(h, wa_ref[...], preferred_element_type=F32)
    pc_ref[...] = jnp.dot(h, wc_ref[...], preferred_element_type=F32)
    pb_ref[...] = jnp.dot(h, wb_ref[...], preferred_element_type=F32)


def _inproj(x2d, nw, wa, wc, wb, tm):
    m = x2d.shape[0]
    const = lambda i: (0, 0)
    row = lambda i: (i, 0)
    return pl.pallas_call(
        _inproj_kernel,
        grid=(m // tm,),
        in_specs=[
            pl.BlockSpec((tm, D_MODEL), row),
            pl.BlockSpec((1, D_MODEL), const),
            pl.BlockSpec(wa.shape, const),
            pl.BlockSpec(wc.shape, const),
            pl.BlockSpec(wb.shape, const),
        ],
        out_specs=[
            pl.BlockSpec((tm, wa.shape[1]), row),
            pl.BlockSpec((tm, wc.shape[1]), row),
            pl.BlockSpec((tm, wb.shape[1]), row),
        ],
        out_shape=[
            jax.ShapeDtypeStruct((m, wa.shape[1]), F32),
            jax.ShapeDtypeStruct((m, wc.shape[1]), F32),
            jax.ShapeDtypeStruct((m, wb.shape[1]), F32),
        ],
        compiler_params=pltpu.CompilerParams(vmem_limit_bytes=VMEM_LIMIT),
        name="inproj",
    )(x2d, nw, wa, wc, wb)


A_QSCALE = HEAD_DIM ** -0.5 * LOG2E
A_AHEAD = 5


def _dilated_kernel(q_ref, k_ref, v_ref, bias_ref, o_ref, qs_s, ks_s, vs_s, x_s, m_s, l_s, acc_s):
    seq = q_ref.shape[1]
    (w1, d1), (w2, d2), (w3, d3) = A_BRANCHES
    assert d1 == 1 and d3 == d2 * d2 and all(w // d == A_BLK for w, d in A_BRANCHES)
    coarse = seq // d2
    lane = lax.broadcasted_iota(jnp.int32, (A_BLK, LANES), 1)
    head0 = lane < HEAD_DIM
    vs_s[:, :, LANES:] = jnp.ones((len(A_BRANCHES), seq, LANES), BF16)

    def put(g, dst, q, k, v):
        lane_l = lax.broadcasted_iota(jnp.int32, q.shape, 1)
        qs_s[g, 0, dst, :] = jnp.where(lane_l < HEAD_DIM, q, 0.0).astype(BF16)
        qs_s[g, 1, dst, :] = jnp.where(lane_l < HEAD_DIM, 0.0, q).astype(BF16)
        ks_s[g, dst, :] = k.astype(BF16)
        vs_s[g, dst, :LANES] = v.astype(BF16)

    def logits(g, bi, first):
        qrows = pl.ds(bi * A_BLK, A_BLK)
        q2 = jnp.concatenate([qs_s[g, 0, qrows, :], qs_s[g, 1, qrows, :]], axis=0)
        if first:
            krows = qrows
            bias = bias_ref[0, g, :, A_BLK:]
        else:
            krows = pl.ds((bi - 1) * A_BLK, 2 * A_BLK)
            bias = bias_ref[0, g]
        return _nt_dot(q2, ks_s[g, krows, :]) + bias, krows

    def finish(g, s, krows, out_rows):
        m = jnp.max(s, axis=-1, keepdims=True)
        p = jnp.exp2(s - m).astype(BF16)
        r = jnp.dot(p, vs_s[g, krows, :], preferred_element_type=F32)
        m_s[g, out_rows, :] = jnp.where(head0, m[:A_BLK], m[A_BLK:])
        l_s[g, out_rows, :] = jnp.where(head0, r[:A_BLK, LANES:], r[A_BLK:, LANES:])
        acc_s[g, out_rows, :] = jnp.where(head0, r[:A_BLK, :LANES], r[A_BLK:, :LANES])

    def run(blocks):
        pending = []
        for g, bi, first, out_rows in blocks:
            pending.append((g, logits(g, bi, first), out_rows))
            if len(pending) > A_AHEAD:
                g0, (s, krows), rows = pending.pop(0)
                finish(g0, s, krows, rows)
        for g0, (s, krows), rows in pending:
            finish(g0, s, krows, rows)

    put(0, pl.ds(0, seq), q_ref[0] * A_QSCALE, k_ref[0], v_ref[0])
    for r in range(d2):
        src = pl.ds(r, coarse, stride=d2)
        dst = pl.ds(r * coarse, coarse)
        q, k, v = q_ref[0, src, :] * A_QSCALE, k_ref[0, src, :], v_ref[0, src, :]
        x_s[0, dst, :], x_s[1, dst, :], x_s[2, dst, :] = q, k, v
        put(1, dst, q, k, v)
    fine = seq // d3
    for r in range(d3):
        src = pl.ds((r % d2) * coarse + r // d2, fine, stride=d2)
        put(2, pl.ds(r * fine, fine), x_s[0, src, :], x_s[1, src, :], x_s[2, src, :])

    nb1, nb2 = coarse // A_BLK, fine // A_BLK
    run([(0, bi, bi == 0, pl.ds(bi * A_BLK, A_BLK)) for bi in range(seq // A_BLK)]
        + [(1, bi, bi % nb1 == 0, pl.ds(bi * A_BLK, A_BLK)) for bi in range(d2 * nb1)]
        + [(2, r * nb2 + n, n == 0, pl.ds((r % d2) * coarse + r // d2 + n * A_BLK * d2, A_BLK, stride=d2))
           for r in range(d3) for n in range(nb2)])

    for r in range(d2):
        rows = pl.ds(r * coarse, coarse)
        tok = pl.ds(r, coarse, stride=d2)
        m1, m2, m3 = m_s[0, tok, :], m_s[1, rows, :], m_s[2, rows, :]
        m = jnp.maximum(jnp.maximum(m1, m2), m3)
        e1, e2, e3 = jnp.exp2(m1 - m), jnp.exp2(m2 - m), jnp.exp2(m3 - m)
        num = e1 * acc_s[0, tok, :] + e2 * acc_s[1, rows, :] + e3 * acc_s[2, rows, :]
        den = e1 * l_s[0, tok, :] + e2 * l_s[1, rows, :] + e3 * l_s[2, rows, :]
        o_ref[0, tok, :] = num / den


def _dilated(pa, bias_a, batch, seq):
    npair = A_HEADS // PAIR
    nbr = len(A_BRANCHES)
    blk = lambda off: pl.BlockSpec((1, seq, LANES), lambda p, b: (b, 0, off + p))
    return pl.pallas_call(
        _dilated_kernel,
        grid=(npair, batch),
        in_specs=[
            blk(0), blk(npair), blk(2 * npair),
            pl.BlockSpec((1, nbr, PAIR * A_BLK, 2 * A_BLK), lambda p, b: (p, 0, 0, 0)),
        ],
        out_specs=pl.BlockSpec((1, seq, LANES), lambda p, b: (b, 0, p)),
        out_shape=jax.ShapeDtypeStruct((batch, seq, A_WIDTH), F32),
        scratch_shapes=[
            pltpu.VMEM((nbr, PAIR, seq, LANES), BF16),
            pltpu.VMEM((nbr, seq, LANES), BF16),
            pltpu.VMEM((nbr, seq, 2 * LANES), BF16),
            pltpu.VMEM((3, seq, LANES), F32),
            pltpu.VMEM((nbr, seq, LANES), F32),
            pltpu.VMEM((nbr, seq, LANES), F32),
            pltpu.VMEM((nbr, seq, LANES), F32),
        ],
        compiler_params=pltpu.CompilerParams(vmem_limit_bytes=VMEM_LIMIT),
        name="dilated_attn",
    )(pa, pa, pa, bias_a)


GLA_TILE = 128
GLA_LEVELS = GLA_TILE.bit_length() - 1
GLA_GROUP = 16
PB_Q, PB_K, PB_V, PB_R, PB_A = 0, B_QK, 2 * B_QK, 2 * B_QK + B_WIDTH, 2 * B_QK + 2 * B_WIDTH
PB_WIDTH = PB_A + LANES


def _log_sigmoid(z):
    return jnp.minimum(z, 0.0) - jnp.log1p(jnp.exp(-jnp.abs(z)))


def _gla_constants():
    t = GLA_TILE
    i = np.arange(t)[:, None]
    c = np.arange(t)[None, :]
    blocks = [(c <= i).astype(np.float32)]
    for lvl in range(1, GLA_LEVELS + 1):
        w = t >> lvl
        split = (i // (2 * w)) * (2 * w) + w - 1
        blocks.append((c <= i).astype(np.float32) - (c <= split).astype(np.float32))
    x = np.arange(t)[:, None] ^ np.arange(t)[None, :]
    level_of_pair = np.where(np.arange(t)[None, :] < np.arange(t)[:, None],
                             np.floor(np.log2(np.maximum(x, 1))).astype(np.int32), -1)
    return np.concatenate(blocks, axis=0), np.tile(level_of_pair, (1, B_HEADS)).astype(np.int32)


def _gla_kernel(pb_ref, w2_ref, ba_ref, nw_ref, g_ref, lvl_ref, y_ref):
    seq = pb_ref.shape[1]
    t = GLA_TILE
    kl = lax.broadcasted_iota(jnp.int32, (B_QK, B_WIDTH), 0) // B_KEY_DIM
    vl = lax.broadcasted_iota(jnp.int32, (B_QK, B_WIDTH), 1) // B_VAL_DIM
    expand = jnp.where(kl == vl, 1.0, 0.0).astype(BF16)
    vlt = lax.broadcasted_iota(jnp.int32, (B_WIDTH, B_QK), 0) // B_VAL_DIM
    klt = lax.broadcasted_iota(jnp.int32, (B_WIDTH, B_QK), 1) // B_KEY_DIM
    head_t = vlt == klt
    va = (lax.broadcasted_iota(jnp.int32, (2 * B_WIDTH, B_WIDTH), 0) % B_WIDTH) // B_VAL_DIM
    vb = lax.broadcasted_iota(jnp.int32, (2 * B_WIDTH, B_WIDTH), 1) // B_VAL_DIM
    head_mean2 = jnp.where(va == vb, 1.0 / B_VAL_DIM, 0.0).astype(BF16)
    row = lax.broadcasted_iota(jnp.int32, (t, B_QK), 0)
    rep_head_k = lax.broadcasted_iota(jnp.int32, (B_HEADS * t, B_QK), 0) // t
    own_key_lanes = rep_head_k == lax.broadcasted_iota(jnp.int32, (B_HEADS * t, B_QK), 1) // B_KEY_DIM
    rep_head_v = lax.broadcasted_iota(jnp.int32, (B_HEADS * t, B_WIDTH), 0) // t
    own_val_lanes = rep_head_v == lax.broadcasted_iota(jnp.int32, (B_HEADS * t, B_WIDTH), 1) // B_VAL_DIM
    zero_k4 = jnp.zeros((B_HEADS * t, B_QK), BF16)
    zero_v4 = jnp.zeros((B_HEADS * t, B_WIDTH), BF16)

    def log_decay(ti):
        rows = pl.ds(pl.multiple_of(ti * t, t), t)
        a = pb_ref[0, rows, PB_A:PB_A + LANES]
        a_hi = a.astype(BF16)
        a_lo = (a - a_hi.astype(F32)).astype(BF16)
        z = jnp.dot(jnp.concatenate([a_hi, a_hi, a_lo], axis=1), w2_ref[...], preferred_element_type=F32)
        return _log_sigmoid(z + ba_ref[...]) * (LOG2E / B_GATE_TAU)

    def decays(log_a):
        a1 = log_a.astype(BF16)
        a2 = (log_a - a1.astype(F32)).astype(BF16)
        dd = jnp.dot(g_ref[...], jnp.concatenate([a1, a2], axis=1), preferred_element_type=F32)
        return dd[:, :B_QK] + dd[:, B_QK:]

    def pairs(ti, dist):
        rows = pl.ds(pl.multiple_of(ti * t, t), t)
        q = pb_ref[0, rows, PB_Q:PB_Q + B_QK] * (B_KEY_DIM ** -0.5)
        k = pb_ref[0, rows, PB_K:PB_K + B_QK]
        attn = jnp.zeros((t, B_HEADS * t), F32)
        for lvl in range(1, GLA_LEVELS + 1):
            w = t >> lvl
            decay = jnp.exp2(-jnp.abs(dist[lvl * t:(lvl + 1) * t]))
            after = (row & w) != 0
            q_l = (jnp.where(after, q * decay, 0.0)).astype(BF16)
            k_l = jnp.where(after, 0.0, k * decay).astype(BF16)
            k4 = jnp.where(own_key_lanes, jnp.concatenate([k_l] * B_HEADS, axis=0), zero_k4)
            attn = jnp.where(lvl_ref[...] == GLA_LEVELS - lvl, _nt_dot(q_l, k4), attn)
        return attn.astype(BF16), q, k

    def within(ti, dist, attn, q, k):
        rows = pl.ds(pl.multiple_of(ti * t, t), t)
        b = dist[:t]
        b_last = b[t - 1:t, :]
        v = pb_ref[0, rows, PB_V:PB_V + B_WIDTH]
        v_bf = v.astype(BF16)
        v4 = jnp.where(own_val_lanes, jnp.concatenate([v_bf] * B_HEADS, axis=0), zero_v4)
        o = jnp.dot(attn, v4, preferred_element_type=F32)
        o = o + jnp.dot((q * k).astype(BF16), expand, preferred_element_type=F32) * v
        q_dec = (q * jnp.exp2(b)).astype(BF16)
        k_dec = (k * jnp.exp2(b_last - b)).astype(BF16)
        u_t = jnp.dot(v.T.astype(BF16), k_dec, preferred_element_type=F32)
        return o, q_dec, jnp.where(head_t, u_t, 0.0), jnp.exp2(b_last)

    def finish(ti, o):
        rows = pl.ds(pl.multiple_of(ti * t, t), t)
        sq = o * o
        sq_hi = sq.astype(BF16)
        sq_lo = (sq - sq_hi.astype(F32)).astype(BF16)
        ms = jnp.dot(jnp.concatenate([sq_hi, sq_lo], axis=1), head_mean2, preferred_element_type=F32)
        r = pb_ref[0, rows, PB_R:PB_R + B_WIDTH]
        silu = r / (1.0 + jnp.exp(-r))
        y_ref[0, rows, :] = o * lax.rsqrt(ms + EPS) * nw_ref[...] * silu

    def group(gi, state):
        tiles = [gi * GLA_GROUP + j for j in range(GLA_GROUP)]
        dists = [decays(log_a) for log_a in [log_decay(ti) for ti in tiles]]
        weights = [pairs(ti, dist) for ti, dist in zip(tiles, dists)]
        parts = [within(ti, dist, *w) for ti, dist, w in zip(tiles, dists, weights)]
        outs = []
        for o, q_dec, u_t, gate in parts:
            outs.append(o + _nt_dot(q_dec, state.astype(BF16)))
            state = state * gate + u_t
        for ti, o in zip(tiles, outs):
            finish(ti, o)
        return state

    lax.fori_loop(0, seq // (t * GLA_GROUP), group, jnp.zeros((B_WIDTH, B_QK), F32))


def _gla(pb, w2p, b_a, norm_w, batch, seq):
    const = lambda b: (0, 0)
    g_all, lvl = _gla_constants()
    g_all = jnp.asarray(g_all, BF16)
    lvl = jnp.asarray(lvl)
    w_hi = w2p.astype(BF16)
    w_lo = (w2p - w_hi.astype(F32)).astype(BF16)
    w2p = jnp.concatenate([w_hi, w_lo, w_hi], axis=0)
    return pl.pallas_call(
        _gla_kernel,
        grid=(batch,),
        in_specs=[
            pl.BlockSpec((1, seq, PB_WIDTH), lambda b: (b, 0, 0)),
            pl.BlockSpec(w2p.shape, const),
            pl.BlockSpec((1, B_QK), const),
            pl.BlockSpec((1, B_WIDTH), const),
            pl.BlockSpec(g_all.shape, const),
            pl.BlockSpec(lvl.shape, const),
        ],
        out_specs=pl.BlockSpec((1, seq, B_WIDTH), lambda b: (b, 0, 0)),
        out_shape=jax.ShapeDtypeStruct((batch, seq, B_WIDTH), F32),
        compiler_params=pltpu.CompilerParams(vmem_limit_bytes=VMEM_LIMIT),
        name="gla",
    )(pb, w2p, b_a, norm_w, g_all, lvl)


C_AHEAD = 2


def _moba_kernel(q_ref, k_ref, v_ref, bias_ref, o_ref, ka_s, qa_s, va_s):
    seq = q_ref.shape[1]
    nblk = seq // C_BLOCK
    head0 = lax.broadcasted_iota(jnp.int32, (C_BLOCK, LANES), 1) < HEAD_DIM

    k = k_ref[0]
    q = q_ref[0]
    ka_s[:, :LANES] = k.astype(BF16)
    row_blk = lax.broadcasted_iota(jnp.int32, (seq, LANES), 0) // C_BLOCK
    lane_id = lax.broadcasted_iota(jnp.int32, (seq, LANES), 1)
    ka_s[:, LANES:] = jnp.where(lane_id == row_blk, 1.0, 0.0).astype(BF16)
    va_s[:, :LANES] = v_ref[0].astype(BF16)
    va_s[:, LANES:] = jnp.ones((seq, LANES), BF16)
    k_mean = jnp.mean(k.reshape(nblk, C_BLOCK, LANES), axis=1)
    q_scaled = q * (HEAD_DIM ** -0.5 * LOG2E)
    blk = lax.broadcasted_iota(jnp.int32, (nblk, seq), 0)
    q_blk = lax.broadcasted_iota(jnp.int32, (nblk, seq), 1) // C_BLOCK
    for hh in range(PAIR):
        qa_s[hh, :, :LANES] = jnp.where((lane_id < HEAD_DIM) == (hh == 0), q_scaled, 0.0).astype(BF16)
        gate = _nt_dot(k_mean.astype(BF16), qa_s[hh, :, :LANES])
        rank = jnp.zeros((nblk, seq), jnp.int32)
        for m in range(nblk):
            gm = gate[m:m + 1, :]
            beats = (gm > gate) | ((gm == gate) & (m < blk))
            rank = rank + jnp.where(beats & (m < q_blk), 1, 0)
        dropped = (blk < q_blk) & (rank >= C_TOPK)
        pen = jnp.where(dropped, NEG_INF, 0.0)
        assert nblk == SUBLANES
        pen_t = jnp.concatenate([pen, jnp.zeros((LANES - nblk, seq), F32)], axis=0).T
        qa_s[hh, :, LANES:] = pen_t.astype(BF16)

    def logits(qb):
        qrows = slice(qb * C_BLOCK, (qb + 1) * C_BLOCK)
        kdim = slice(0, LANES) if qb <= C_TOPK else slice(0, 2 * LANES)
        q2 = jnp.concatenate([qa_s[0, qrows, kdim], qa_s[1, qrows, kdim]], axis=0)
        return _nt_dot(q2, ka_s[:(qb + 1) * C_BLOCK, kdim]) + bias_ref[0, :, (nblk - 1 - qb) * C_BLOCK:]

    def finish(qb, s):
        m = jnp.max(s, axis=-1, keepdims=True)
        p = jnp.exp2(s - m).astype(BF16)
        r = jnp.dot(p, va_s[:(qb + 1) * C_BLOCK, :], preferred_element_type=F32)
        o_ref[0, qb * C_BLOCK:(qb + 1) * C_BLOCK, :] = jnp.where(
            head0, r[:C_BLOCK, :LANES] / r[:C_BLOCK, LANES:], r[C_BLOCK:, :LANES] / r[C_BLOCK:, LANES:])

    early = [qb for qb in reversed(range(nblk)) if qb <= C_TOPK]
    late = [qb for qb in reversed(range(nblk)) if qb > C_TOPK]
    order = [qb for pair in zip(early, late) for qb in pair] + early[len(late):] + late[len(early):]
    pending = []
    for qb in order:
        pending.append((qb, logits(qb)))
        if len(pending) > C_AHEAD:
            finish(*pending.pop(0))
    for item in pending:
        finish(*item)


def _moba(pc, bias_c, batch, seq):
    npair = C_HEADS // PAIR
    blk = lambda off: pl.BlockSpec((1, seq, LANES), lambda p, b: (b, 0, off + p))
    return pl.pallas_call(
        _moba_kernel,
        grid=(npair, batch),
        in_specs=[
            blk(0), blk(npair), blk(2 * npair),
            pl.BlockSpec((1, PAIR * C_BLOCK, seq), lambda p, b: (p, 0, 0)),
        ],
        out_specs=pl.BlockSpec((1, seq, LANES), lambda p, b: (b, 0, p)),
        out_shape=jax.ShapeDtypeStruct((batch, seq, C_WIDTH), F32),
        scratch_shapes=[
            pltpu.VMEM((seq, 2 * LANES), BF16),
            pltpu.VMEM((PAIR, seq, 2 * LANES), BF16),
            pltpu.VMEM((seq, 2 * LANES), BF16),
        ],
        compiler_params=pltpu.CompilerParams(vmem_limit_bytes=VMEM_LIMIT),
        name="moba_attn",
    )(pc, pc, pc, bias_c)


FF_CHUNK = 1024


def _outffn_kernel(x_ref, ya_ref, yb_ref, yc_ref, wo_ref, n2_ref, w1_ref, w2_ref, fn_ref, o_ref, *, final):
    mix = jnp.concatenate([ya_ref[...].astype(BF16), yb_ref[...].astype(BF16), yc_ref[...].astype(BF16)], axis=1)
    x = x_ref[...] + jnp.dot(mix, wo_ref[...], preferred_element_type=F32)
    h = _rms(x, n2_ref[...]).astype(BF16)
    ffn = None
    for j in range(D_FF // FF_CHUNK):
        cols = slice(j * FF_CHUNK, (j + 1) * FF_CHUNK)
        a = jnp.dot(h, w1_ref[:, cols], preferred_element_type=F32)
        a = jnp.square(jnp.maximum(a, 0.0)).astype(BF16)
        part = jnp.dot(a, w2_ref[cols, :], preferred_element_type=F32)
        ffn = part if ffn is None else ffn + part
    x = x + ffn
    if final:
        x = _rms(x, fn_ref[...])
    o_ref[...] = x


def _outffn(x2d, ya, yb, yc, wo, n2, w1, w2, fn, tm, final):
    m = x2d.shape[0]
    const = lambda i: (0, 0)
    row = lambda i: (i, 0)
    full = lambda a: pl.BlockSpec(a.shape, const)
    return pl.pallas_call(
        functools.partial(_outffn_kernel, final=final),
        grid=(m // tm,),
        in_specs=[
            pl.BlockSpec((tm, D_MODEL), row),
            pl.BlockSpec((tm, ya.shape[1]), row),
            pl.BlockSpec((tm, yb.shape[1]), row),
            pl.BlockSpec((tm, yc.shape[1]), row),
            full(wo), full(n2), full(w1), full(w2), full(fn),
        ],
        out_specs=pl.BlockSpec((tm, D_MODEL), row),
        out_shape=jax.ShapeDtypeStruct((m, D_MODEL), F32),
        compiler_params=pltpu.CompilerParams(vmem_limit_bytes=VMEM_LIMIT),
        name="outproj_ffn",
    )(x2d, ya, yb, yc, wo, n2, w1, w2, fn)


def _split_w_in(w):
    sizes = (A_WIDTH, A_WIDTH, A_WIDTH, B_QK, B_QK, B_WIDTH, B_WIDTH, B_GATE_RANK, C_WIDTH, C_WIDTH, C_WIDTH)
    offs = np.concatenate([[0], np.cumsum(sizes)])
    aq, ak, av, bq, bk, bv, br, ba, cq, ck, cv = [w[:, offs[i]:offs[i + 1]] for i in range(len(sizes))]
    ba = jnp.pad(ba, ((0, 0), (0, LANES - B_GATE_RANK)))
    cat = lambda parts: jnp.concatenate(parts, axis=1).astype(BF16)
    return cat([aq, ak, av]), cat([cq, ck, cv]), cat([bq, bk, bv, br, ba])


def kernel(x, norm1_w, w_in, gla_w_a2, gla_b_a, gla_norm_w, w_out, norm2_w, w_ff1, w_ff2, rel_bias, final_norm_w):
    batch, seq, d = x.shape
    depth = w_in.shape[0]
    assert d == D_MODEL and seq % C_BLOCK == 0 and seq % (A_BLK * A_BRANCHES[-1][1]) == 0
    tokens = batch * seq
    tm = 512 if tokens % 512 == 0 else seq
    bias_a, bias_c = _bias_tiles(rel_bias, seq)
    x2d = x.reshape(tokens, d)
    fn = final_norm_w.reshape(1, d)
    for i in range(depth):
        wa, wc, wb = _split_w_in(w_in[i])
        pa, pc, pb = _inproj(x2d, norm1_w[i].reshape(1, d), wa, wc, wb, tm)
        ya = _dilated(pa.reshape(batch, seq, -1), bias_a, batch, seq)
        w2p = jnp.pad(gla_w_a2[i], ((0, LANES - B_GATE_RANK), (0, 0)))
        yb = _gla(pb.reshape(batch, seq, -1), w2p, gla_b_a[i].reshape(1, B_QK),
                  gla_norm_w[i].reshape(1, B_WIDTH), batch, seq)
        yc = _moba(pc.reshape(batch, seq, -1), bias_c, batch, seq)
        x2d = _outffn(
            x2d, ya.reshape(tokens, -1), yb.reshape(tokens, -1), yc.reshape(tokens, -1), w_out[i].astype(BF16),
            norm2_w[i].reshape(1, d), w_ff1[i].astype(BF16), w_ff2[i].astype(BF16), fn,
            tm, final=(i == depth - 1))
    return x2d.reshape(batch, seq, d)
```

```python
import functools
import math

import numpy as np
import jax
import jax.numpy as jnp
from jax import lax
from jax.experimental import pallas as pl
from jax.experimental.pallas import tpu as pltpu

D_MODEL = 1024
HEAD_DIM = 64
A_HEADS = 6
A_BRANCHES = ((128, 1), (512, 4), (2048, 16))
B_HEADS = 4
B_KEY_DIM = 32
B_VAL_DIM = 64
B_GATE_RANK = 16
B_GATE_TAU = 16.0
C_HEADS = 6
C_BLOCK = 256
C_TOPK = 3
REL_BUCKETS = 32
REL_MAX_DIST = 2048
D_FF = 4 * D_MODEL
EPS = 1e-6
NEG_INF = -1e30
LOG2E = math.log2(math.e)

A_WIDTH = A_HEADS * HEAD_DIM
B_QK = B_HEADS * B_KEY_DIM
B_WIDTH = B_HEADS * B_VAL_DIM
C_WIDTH = C_HEADS * HEAD_DIM

LANES = 128
SUBLANES = 8
A_BLK = 128
PAIR = LANES // HEAD_DIM
VMEM_LIMIT = 56 * 1024 * 1024
F32 = jnp.float32
BF16 = jnp.bfloat16


def _bucket_thresholds():
    n = np.arange(0, REL_MAX_DIST + 1)
    exact = REL_BUCKETS // 2
    logv = (np.log(np.maximum(n, 1).astype(np.float32) / np.float32(exact))
            / np.float32(math.log(REL_MAX_DIST / exact))).astype(np.float32)
    large = np.minimum(exact + (logv * np.float32(REL_BUCKETS - exact)).astype(np.int32), REL_BUCKETS - 1)
    bucket = np.where(n < exact, n, large)
    assert np.all(np.diff(bucket) >= 0)
    return [int(np.argmax(bucket >= k)) for k in range(REL_BUCKETS)]


_THRESH = _bucket_thresholds()


def _nt_dot(a, b):
    return lax.dot_general(a, b, (((1,), (1,)), ((), ())), preferred_element_type=F32)


def _rms(x, w):
    return x * lax.rsqrt(jnp.mean(x * x, axis=-1, keepdims=True) + EPS) * w


def _bias_lookup(dist, rb_ref, col):
    val = jnp.full(dist.shape, rb_ref[0, col], F32)
    for k in range(1, REL_BUCKETS):
        val = jnp.where(dist >= _THRESH[k], rb_ref[k, col], val)
    return val


def _bias_kernel(rb_ref, ba_ref, bc_ref):
    h = pl.program_id(0)
    qi = lax.broadcasted_iota(jnp.int32, (A_BLK, 2 * A_BLK), 0)
    kj = lax.broadcasted_iota(jnp.int32, (A_BLK, 2 * A_BLK), 1)
    steps = qi + A_BLK - kj
    band = (steps >= 0) & (steps <= A_BLK)
    for g, (_, dil) in enumerate(A_BRANCHES):
        ba_ref[0, g] = jnp.where(band, _bias_lookup(steps * dil, rb_ref, h) * LOG2E, NEG_INF)
    qi = lax.broadcasted_iota(jnp.int32, (C_BLOCK, C_BLOCK), 0)
    kj = lax.broadcasted_iota(jnp.int32, (C_BLOCK, C_BLOCK), 1)
    nblk = bc_ref.shape[2] // C_BLOCK
    for delta in range(nblk):
        dist = delta * C_BLOCK + qi - kj
        bias = _bias_lookup(dist, rb_ref, A_HEADS + h) * LOG2E
        if delta == 0:
            bias = jnp.where(dist >= 0, bias, NEG_INF)
        bc_ref[0, :, (nblk - 1 - delta) * C_BLOCK:(nblk - delta) * C_BLOCK] = bias


def _bias_tiles(rel_bias, seq):
    return pl.pallas_call(
        _bias_kernel,
        grid=(A_HEADS,),
        in_specs=[pl.BlockSpec(memory_space=pltpu.SMEM)],
        out_specs=[
            pl.BlockSpec((1, len(A_BRANCHES), A_BLK, 2 * A_BLK), lambda h: (h // PAIR, 0, h % PAIR, 0)),
            pl.BlockSpec((1, C_BLOCK, seq), lambda h: (h // PAIR, h % PAIR, 0)),
        ],
        out_shape=[
            jax.ShapeDtypeStruct((A_HEADS // PAIR, len(A_BRANCHES), PAIR * A_BLK, 2 * A_BLK), F32),
            jax.ShapeDtypeStruct((C_HEADS // PAIR, PAIR * C_BLOCK, seq), F32),
        ],
        name="bias_tiles",
    )(rel_bias)


def _inproj_kernel(x_ref, nw_ref, wa_ref, wc_ref, wb_ref, pa_ref, pc_ref, pb_ref):
    h = _rms(x_ref[...], nw_ref[...]).astype(BF16)
    pa_ref[...] = jnp.dot(h, wa_ref[...], preferred_element_type=F32)
    pc_ref[...] = jnp.dot(h, wc_ref[...], preferred_element_type=F32)
    pb_ref[...] = jnp.dot(h, wb_ref[...], preferred_element_type=F32)


def _inproj(x2d, nw, wa, wc, wb, tm):
    m = x2d.shape[0]
    const = lambda i: (0, 0)
    row = lambda i: (i, 0)
    return pl.pallas_call(
        _inproj_kernel,
        grid=(m // tm,),
        in_specs=[
            pl.BlockSpec((tm, D_MODEL), row),
            pl.BlockSpec((1, D_MODEL), const),
            pl.BlockSpec(wa.shape, const),
            pl.BlockSpec(wc.shape, const),
            pl.BlockSpec(wb.shape, const),
        ],
        out_specs=[
            pl.BlockSpec((tm, wa.shape[1]), row),
            pl.BlockSpec((tm, wc.shape[1]), row),
            pl.BlockSpec((tm, wb.shape[1]), row),
        ],
        out_shape=[
            jax.ShapeDtypeStruct((m, wa.shape[1]), F32),
            jax.ShapeDtypeStruct((m, wc.shape[1]), F32),
            jax.ShapeDtypeStruct((m, wb.shape[1]), F32),
        ],
        compiler_params=pltpu.CompilerParams(vmem_limit_bytes=VMEM_LIMIT),
        name="inproj",
    )(x2d, nw, wa, wc, wb)


A_QSCALE = HEAD_DIM ** -0.5 * LOG2E
A_AHEAD = 5


def _dilated_kernel(q_ref, k_ref, v_ref, bias_ref, o_ref, qs_s, ks_s, vs_s, x_s, m_s, l_s, acc_s):
    seq = q_ref.shape[1]
    (w1, d1), (w2, d2), (w3, d3) = A_BRANCHES
    assert d1 == 1 and d3 == d2 * d2 and all(w // d == A_BLK for w, d in A_BRANCHES)
    coarse = seq // d2
    lane = lax.broadcasted_iota(jnp.int32, (A_BLK, LANES), 1)
    head0 = lane < HEAD_DIM
    vs_s[:, :, LANES:] = jnp.ones((len(A_BRANCHES), seq, LANES), BF16)

    def put(g, dst, q, k, v):
        lane_l = lax.broadcasted_iota(jnp.int32, q.shape, 1)
        qs_s[g, 0, dst, :] = jnp.where(lane_l < HEAD_DIM, q, 0.0).astype(BF16)
        qs_s[g, 1, dst, :] = jnp.where(lane_l < HEAD_DIM, 0.0, q).astype(BF16)
        ks_s[g, dst, :] = k.astype(BF16)
        vs_s[g, dst, :LANES] = v.astype(BF16)

    def logits(g, bi, first):
        qrows = pl.ds(bi * A_BLK, A_BLK)
        q2 = jnp.concatenate([qs_s[g, 0, qrows, :], qs_s[g, 1, qrows, :]], axis=0)
        if first:
            krows = qrows
            bias = bias_ref[0, g, :, A_BLK:]
        else:
            krows = pl.ds((bi - 1) * A_BLK, 2 * A_BLK)
            bias = bias_ref[0, g]
        return _nt_dot(q2, ks_s[g, krows, :]) + bias, krows

    def finish(g, s, krows, out_rows):
        m = jnp.max(s, axis=-1, keepdims=True)
        p = jnp.exp2(s - m).astype(BF16)
        r = jnp.dot(p, vs_s[g, krows, :], preferred_element_type=F32)
        m_s[g, out_rows, :] = jnp.where(head0, m[:A_BLK], m[A_BLK:])
        l_s[g, out_rows, :] = jnp.where(head0, r[:A_BLK, LANES:], r[A_BLK:, LANES:])
        acc_s[g, out_rows, :] = jnp.where(head0, r[:A_BLK, :LANES], r[A_BLK:, :LANES])

    def run(blocks):
        pending = []
        for g, bi, first, out_rows in blocks:
            pending.append((g, logits(g, bi, first), out_rows))
            if len(pending) > A_AHEAD:
                g0, (s, krows), rows = pending.pop(0)
                finish(g0, s, krows, rows)
        for g0, (s, krows), rows in pending:
            finish(g0, s, krows, rows)

    put(0, pl.ds(0, seq), q_ref[0] * A_QSCALE, k_ref[0], v_ref[0])
    for r in range(d2):
        src = pl.ds(r, coarse, stride=d2)
        dst = pl.ds(r * coarse, coarse)
        q, k, v = q_ref[0, src, :] * A_QSCALE, k_ref[0, src, :], v_ref[0, src, :]
        x_s[0, dst, :], x_s[1, dst, :], x_s[2, dst, :] = q, k, v
        put(1, dst, q, k, v)
    fine = seq // d3
    for r in range(d3):
        src = pl.ds((r % d2) * coarse + r // d2, fine, stride=d2)
        put(2, pl.ds(r * fine, fine), x_s[0, src, :], x_s[1, src, :], x_s[2, src, :])

    nb1, nb2 = coarse // A_BLK, fine // A_BLK
    run([(0, bi, bi == 0, pl.ds(bi * A_BLK, A_BLK)) for bi in range(seq // A_BLK)]
        + [(1, bi, bi % nb1 == 0, pl.ds(bi * A_BLK, A_BLK)) for bi in range(d2 * nb1)]
        + [(2, r * nb2 + n, n == 0, pl.ds((r % d2) * coarse + r // d2 + n * A_BLK * d2, A_BLK, stride=d2))
           for r in range(d3) for n in range(nb2)])

    for r in range(d2):
        rows = pl.ds(r * coarse, coarse)
        tok = pl.ds(r, coarse, stride=d2)
        m1, m2, m3 = m_s[0, tok, :], m_s[1, rows, :], m_s[2, rows, :]
        m = jnp.maximum(jnp.maximum(m1, m2), m3)
        e1, e2, e3 = jnp.exp2(m1 - m), jnp.exp2(m2 - m), jnp.exp2(m3 - m)
        num = e1 * acc_s[0, tok, :] + e2 * acc_s[1, rows, :] + e3 * acc_s[2, rows, :]
        den = e1 * l_s[0, tok, :] + e2 * l_s[1, rows, :] + e3 * l_s[2, rows, :]
        o_ref[0, tok, :] = num / den


def _dilated(pa, bias_a, batch, seq):
    npair = A_HEADS // PAIR
    nbr = len(A_BRANCHES)
    blk = lambda off: pl.BlockSpec((1, seq, LANES), lambda p, b: (b, 0, off + p))
    return pl.pallas_call(
        _dilated_kernel,
        grid=(npair, batch),
        in_specs=[
            blk(0), blk(npair), blk(2 * npair),
            pl.BlockSpec((1, nbr, PAIR * A_BLK, 2 * A_BLK), lambda p, b: (p, 0, 0, 0)),
        ],
        out_specs=pl.BlockSpec((1, seq, LANES), lambda p, b: (b, 0, p)),
        out_shape=jax.ShapeDtypeStruct((batch, seq, A_WIDTH), F32),
        scratch_shapes=[
            pltpu.VMEM((nbr, PAIR, seq, LANES), BF16),
            pltpu.VMEM((nbr, seq, LANES), BF16),
            pltpu.VMEM((nbr, seq, 2 * LANES), BF16),
            pltpu.VMEM((3, seq, LANES), F32),
            pltpu.VMEM((nbr, seq, LANES), F32),
            pltpu.VMEM((nbr, seq, LANES), F32),
            pltpu.VMEM((nbr, seq, LANES), F32),
        ],
        compiler_params=pltpu.CompilerParams(vmem_limit_bytes=VMEM_LIMIT),
        name="dilated_attn",
    )(pa, pa, pa, bias_a)


GLA_TILE = 128
GLA_LEVELS = GLA_TILE.bit_length() - 1
GLA_GROUP = 16
PB_Q, PB_K, PB_V, PB_R, PB_A = 0, B_QK, 2 * B_QK, 2 * B_QK + B_WIDTH, 2 * B_QK + 2 * B_WIDTH
PB_WIDTH = PB_A + LANES


def _log_sigmoid(z):
    return jnp.minimum(z, 0.0) - jnp.log1p(jnp.exp(-jnp.abs(z)))


def _gla_constants():
    t = GLA_TILE
    i = np.arange(t)[:, None]
    c = np.arange(t)[None, :]
    blocks = [(c <= i).astype(np.float32)]
    for lvl in range(1, GLA_LEVELS + 1):
        w = t >> lvl
        split = (i // (2 * w)) * (2 * w) + w - 1
        blocks.append((c <= i).astype(np.float32) - (c <= split).astype(np.float32))
    x = np.arange(t)[:, None] ^ np.arange(t)[None, :]
    level_of_pair = np.where(np.arange(t)[None, :] < np.arange(t)[:, None],
                             np.floor(np.log2(np.maximum(x, 1))).astype(np.int32), -1)
    return np.concatenate(blocks, axis=0), np.tile(level_of_pair, (1, B_HEADS)).astype(np.int32)


def _gla_kernel(pb_ref, w2_ref, ba_ref, nw_ref, g_ref, lvl_ref, y_ref):
    seq = pb_ref.shape[1]
    t = GLA_TILE
    kl = lax.broadcasted_iota(jnp.int32, (B_QK, B_WIDTH), 0) // B_KEY_DIM
    vl = lax.broadcasted_iota(jnp.int32, (B_QK, B_WIDTH), 1) // B_VAL_DIM
    expand = jnp.where(kl == vl, 1.0, 0.0).astype(BF16)
    vlt = lax.broadcasted_iota(jnp.int32, (B_WIDTH, B_QK), 0) // B_VAL_DIM
    klt = lax.broadcasted_iota(jnp.int32, (B_WIDTH, B_QK), 1) // B_KEY_DIM
    head_t = vlt == klt
    va = (lax.broadcasted_iota(jnp.int32, (2 * B_WIDTH, B_WIDTH), 0) % B_WIDTH) // B_VAL_DIM
    vb = lax.broadcasted_iota(jnp.int32, (2 * B_WIDTH, B_WIDTH), 1) // B_VAL_DIM
    head_mean2 = jnp.where(va == vb, 1.0 / B_VAL_DIM, 0.0).astype(BF16)
    row = lax.broadcasted_iota(jnp.int32, (t, B_QK), 0)
    rep_head_k = lax.broadcasted_iota(jnp.int32, (B_HEADS * t, B_QK), 0) // t
    own_key_lanes = rep_head_k == lax.broadcasted_iota(jnp.int32, (B_HEADS * t, B_QK), 1) // B_KEY_DIM
    rep_head_v = lax.broadcasted_iota(jnp.int32, (B_HEADS * t, B_WIDTH), 0) // t
    own_val_lanes = rep_head_v == lax.broadcasted_iota(jnp.int32, (B_HEADS * t, B_WIDTH), 1) // B_VAL_DIM
    zero_k4 = jnp.zeros((B_HEADS * t, B_QK), BF16)
    zero_v4 = jnp.zeros((B_HEADS * t, B_WIDTH), BF16)

    def log_decay(ti):
        rows = pl.ds(pl.multiple_of(ti * t, t), t)
        a = pb_ref[0, rows, PB_A:PB_A + LANES]
        a_hi = a.astype(BF16)
        a_lo = (a - a_hi.astype(F32)).astype(BF16)
        z = jnp.dot(jnp.concatenate([a_hi, a_hi, a_lo], axis=1), w2_ref[...], preferred_element_type=F32)
        return _log_sigmoid(z + ba_ref[...]) * (LOG2E / B_GATE_TAU)

    def decays(log_a):
        a1 = log_a.astype(BF16)
        a2 = (log_a - a1.astype(F32)).astype(BF16)
        dd = jnp.dot(g_ref[...], jnp.concatenate([a1, a2], axis=1), preferred_element_type=F32)
        return dd[:, :B_QK] + dd[:, B_QK:]

    def pairs(ti, dist):
        rows = pl.ds(pl.multiple_of(ti * t, t), t)
        q = pb_ref[0, rows, PB_Q:PB_Q + B_QK] * (B_KEY_DIM ** -0.5)
        k = pb_ref[0, rows, PB_K:PB_K + B_QK]
        attn = jnp.zeros((t, B_HEADS * t), F32)
        for lvl in range(1, GLA_LEVELS + 1):
            w = t >> lvl
            decay = jnp.exp2(-jnp.abs(dist[lvl * t:(lvl + 1) * t]))
            after = (row & w) != 0
            q_l = (jnp.where(after, q * decay, 0.0)).astype(BF16)
            k_l = jnp.where(after, 0.0, k * decay).astype(BF16)
            k4 = jnp.where(own_key_lanes, jnp.concatenate([k_l] * B_HEADS, axis=0), zero_k4)
            attn = jnp.where(lvl_ref[...] == GLA_LEVELS - lvl, _nt_dot(q_l, k4), attn)
        return attn.astype(BF16), q, k

    def within(ti, dist, attn, q, k):
        rows = pl.ds(pl.multiple_of(ti * t, t), t)
        b = dist[:t]
        b_last = b[t - 1:t, :]
        v = pb_ref[0, rows, PB_V:PB_V + B_WIDTH]
        v_bf = v.astype(BF16)
        v4 = jnp.where(own_val_lanes, jnp.concatenate([v_bf] * B_HEADS, axis=0), zero_v4)
        o = jnp.dot(attn, v4, preferred_element_type=F32)
        o = o + jnp.dot((q * k).astype(BF16), expand, preferred_element_type=F32) * v
        q_dec = (q * jnp.exp2(b)).astype(BF16)
        k_dec = (k * jnp.exp2(b_last - b)).astype(BF16)
        u_t = jnp.dot(v.T.astype(BF16), k_dec, preferred_element_type=F32)
        return o, q_dec, jnp.where(head_t, u_t, 0.0), jnp.exp2(b_last)

    def finish(ti, o):
        rows = pl.ds(pl.multiple_of(ti * t, t), t)
        sq = o * o
        sq_hi = sq.astype(BF16)
        sq_lo = (sq - sq_hi.astype(F32)).astype(BF16)
        ms = jnp.dot(jnp.concatenate([sq_hi, sq_lo], axis=1), head_mean2, preferred_element_type=F32)
        r = pb_ref[0, rows, PB_R:PB_R + B_WIDTH]
        silu = r / (1.0 + jnp.exp(-r))
        y_ref[0, rows, :] = o * lax.rsqrt(ms + EPS) * nw_ref[...] * silu

    def group(gi, state):
        tiles = [gi * GLA_GROUP + j for j in range(GLA_GROUP)]
        dists = [decays(log_a) for log_a in [log_decay(ti) for ti in tiles]]
        weights = [pairs(ti, dist) for ti, dist in zip(tiles, dists)]
        parts = [within(ti, dist, *w) for ti, dist, w in zip(tiles, dists, weights)]
        outs = []
        for o, q_dec, u_t, gate in parts:
            outs.append(o + _nt_dot(q_dec, state.astype(BF16)))
            state = state * gate + u_t
        for ti, o in zip(tiles, outs):
            finish(ti, o)
        return state

    lax.fori_loop(0, seq // (t * GLA_GROUP), group, jnp.zeros((B_WIDTH, B_QK), F32))


def _gla(pb, w2p, b_a, norm_w, batch, seq):
    const = lambda b: (0, 0)
    g_all, lvl = _gla_constants()
    g_all = jnp.asarray(g_all, BF16)
    lvl = jnp.asarray(lvl)
    w_hi = w2p.astype(BF16)
    w_lo = (w2p - w_hi.astype(F32)).astype(BF16)
    w2p = jnp.concatenate([w_hi, w_lo, w_hi], axis=0)
    return pl.pallas_call(
        _gla_kernel,
        grid=(batch,),
        in_specs=[
            pl.BlockSpec((1, seq, PB_WIDTH), lambda b: (b, 0, 0)),
            pl.BlockSpec(w2p.shape, const),
            pl.BlockSpec((1, B_QK), const),
            pl.BlockSpec((1, B_WIDTH), const),
            pl.BlockSpec(g_all.shape, const),
            pl.BlockSpec(lvl.shape, const),
        ],
        out_specs=pl.BlockSpec((1, seq, B_WIDTH), lambda b: (b, 0, 0)),
        out_shape=jax.ShapeDtypeStruct((batch, seq, B_WIDTH), F32),
        compiler_params=pltpu.CompilerParams(vmem_limit_bytes=VMEM_LIMIT),
        name="gla",
    )(pb, w2p, b_a, norm_w, g_all, lvl)


C_AHEAD = 2


def _moba_kernel(q_ref, k_ref, v_ref, bias_ref, o_ref, ka_s, qa_s, va_s):
    seq = q_ref.shape[1]
    nblk = seq // C_BLOCK
    head0 = lax.broadcasted_iota(jnp.int32, (C_BLOCK, LANES), 1) < HEAD_DIM

    k = k_ref[0]
    q = q_ref[0]
    ka_s[:, :LANES] = k.astype(BF16)
    row_blk = lax.broadcasted_iota(jnp.int32, (seq, LANES), 0) // C_BLOCK
    lane_id = lax.broadcasted_iota(jnp.int32, (seq, LANES), 1)
    ka_s[:, LANES:] = jnp.where(lane_id == row_blk, 1.0, 0.0).astype(BF16)
    va_s[:, :LANES] = v_ref[0].astype(BF16)
    va_s[:, LANES:] = jnp.ones((seq, LANES), BF16)
    k_mean = jnp.mean(k.reshape(nblk, C_BLOCK, LANES), axis=1)
    q_scaled = q * (HEAD_DIM ** -0.5 * LOG2E)
    blk = lax.broadcasted_iota(jnp.int32, (nblk, seq), 0)
    q_blk = lax.broadcasted_iota(jnp.int32, (nblk, seq), 1) // C_BLOCK
    for hh in range(PAIR):
        qa_s[hh, :, :LANES] = jnp.where((lane_id < HEAD_DIM) == (hh == 0), q_scaled, 0.0).astype(BF16)
        gate = _nt_dot(k_mean.astype(BF16), qa_s[hh, :, :LANES])
        rank = jnp.zeros((nblk, seq), jnp.int32)
        for m in range(nblk):
            gm = gate[m:m + 1, :]
            beats = (gm > gate) | ((gm == gate) & (m < blk))
            rank = rank + jnp.where(beats & (m < q_blk), 1, 0)
        dropped = (blk < q_blk) & (rank >= C_TOPK)
        pen = jnp.where(dropped, NEG_INF, 0.0)
        assert nblk == SUBLANES
        pen_t = jnp.concatenate([pen, jnp.zeros((LANES - nblk, seq), F32)], axis=0).T
        qa_s[hh, :, LANES:] = pen_t.astype(BF16)

    def logits(qb):
        qrows = slice(qb * C_BLOCK, (qb + 1) * C_BLOCK)
        kdim = slice(0, LANES) if qb <= C_TOPK else slice(0, 2 * LANES)
        q2 = jnp.concatenate([qa_s[0, qrows, kdim], qa_s[1, qrows, kdim]], axis=0)
        return _nt_dot(q2, ka_s[:(qb + 1) * C_BLOCK, kdim]) + bias_ref[0, :, (nblk - 1 - qb) * C_BLOCK:]

    def finish(qb, s):
        m = jnp.max(s, axis=-1, keepdims=True)
        p = jnp.exp2(s - m).astype(BF16)
        r = jnp.dot(p, va_s[:(qb + 1) * C_BLOCK, :], preferred_element_type=F32)
        o_ref[0, qb * C_BLOCK:(qb + 1) * C_BLOCK, :] = jnp.where(
            head0, r[:C_BLOCK, :LANES] / r[:C_BLOCK, LANES:], r[C_BLOCK:, :LANES] / r[C_BLOCK:, LANES:])

    early = [qb for qb in reversed(range(nblk)) if qb <= C_TOPK]
    late = [qb for qb in reversed(range(nblk)) if qb > C_TOPK]
    order = [qb for pair in zip(early, late) for qb in pair] + early[len(late):] + late[len(early):]
    pending = []
    for qb in order:
        pending.append((qb, logits(qb)))
        if len(pending) > C_AHEAD:
            finish(*pending.pop(0))
    for item in pending:
        finish(*item)


def _moba(pc, bias_c, batch, seq):
    npair = C_HEADS // PAIR
    blk = lambda off: pl.BlockSpec((1, seq, LANES), lambda p, b: (b, 0, off + p))
    return pl.pallas_call(
        _moba_kernel,
        grid=(npair, batch),
        in_specs=[
            blk(0), blk(npair), blk(2 * npair),
            pl.BlockSpec((1, PAIR * C_BLOCK, seq), lambda p, b: (p, 0, 0)),
        ],
        out_specs=pl.BlockSpec((1, seq, LANES), lambda p, b: (b, 0, p)),
        out_shape=jax.ShapeDtypeStruct((batch, seq, C_WIDTH), F32),
        scratch_shapes=[
            pltpu.VMEM((seq, 2 * LANES), BF16),
            pltpu.VMEM((PAIR, seq, 2 * LANES), BF16),
            pltpu.VMEM((seq, 2 * LANES), BF16),
        ],
        compiler_params=pltpu.CompilerParams(vmem_limit_bytes=VMEM_LIMIT),
        name="moba_attn",
    )(pc, pc, pc, bias_c)


FF_CHUNK = 1024


def _outffn_kernel(x_ref, ya_ref, yb_ref, yc_ref, wo_ref, n2_ref, w1_ref, w2_ref, fn_ref, o_ref, *, final):
    mix = jnp.concatenate([ya_ref[...].astype(BF16), yb_ref[...].astype(BF16), yc_ref[...].astype(BF16)], axis=1)
    x = x_ref[...] + jnp.dot(mix, wo_ref[...], preferred_element_type=F32)
    h = _rms(x, n2_ref[...]).astype(BF16)
    ffn = None
    for j in range(D_FF // FF_CHUNK):
        cols = slice(j * FF_CHUNK, (j + 1) * FF_CHUNK)
        a = jnp.dot(h, w1_ref[:, cols], preferred_element_type=F32)
        a = jnp.square(jnp.maximum(a, 0.0)).astype(BF16)
        part = jnp.dot(a, w2_ref[cols, :], preferred_element_type=F32)
        ffn = part if ffn is None else ffn + part
    x = x + ffn
    if final:
        x = _rms(x, fn_ref[...])
    o_ref[...] = x


def _outffn(x2d, ya, yb, yc, wo, n2, w1, w2, fn, tm, final):
    m = x2d.shape[0]
    const = lambda i: (0, 0)
    row = lambda i: (i, 0)
    full = lambda a: pl.BlockSpec(a.shape, const, pipeline_mode=pl.Buffered(1))
    return pl.pallas_call(
        functools.partial(_outffn_kernel, final=final),
        grid=(m // tm,),
        in_specs=[
            pl.BlockSpec((tm, D_MODEL), row),
            pl.BlockSpec((tm, ya.shape[1]), row),
            pl.BlockSpec((tm, yb.shape[1]), row),
            pl.BlockSpec((tm, yc.shape[1]), row),
            full(wo), full(n2), full(w1), full(w2), full(fn),
        ],
        out_specs=pl.BlockSpec((tm, D_MODEL), row),
        out_shape=jax.ShapeDtypeStruct((m, D_MODEL), F32),
        compiler_params=pltpu.CompilerParams(vmem_limit_bytes=VMEM_LIMIT),
        name="outproj_ffn",
    )(x2d, ya, yb, yc, wo, n2, w1, w2, fn)


def _split_w_in(w):
    sizes = (A_WIDTH, A_WIDTH, A_WIDTH, B_QK, B_QK, B_WIDTH, B_WIDTH, B_GATE_RANK, C_WIDTH, C_WIDTH, C_WIDTH)
    offs = np.concatenate([[0], np.cumsum(sizes)])
    aq, ak, av, bq, bk, bv, br, ba, cq, ck, cv = [w[:, offs[i]:offs[i + 1]] for i in range(len(sizes))]
    ba = jnp.pad(ba, ((0, 0), (0, LANES - B_GATE_RANK)))
    cat = lambda parts: jnp.concatenate(parts, axis=1).astype(BF16)
    return cat([aq, ak, av]), cat([cq, ck, cv]), cat([bq, bk, bv, br, ba])


def kernel(x, norm1_w, w_in, gla_w_a2, gla_b_a, gla_norm_w, w_out, norm2_w, w_ff1, w_ff2, rel_bias, final_norm_w):
    batch, seq, d = x.shape
    depth = w_in.shape[0]
    assert d == D_MODEL and seq % C_BLOCK == 0 and seq % (A_BLK * A_BRANCHES[-1][1]) == 0
    tokens = batch * seq
    tm = 512 if tokens % 512 == 0 else seq
    tm_ffn = 1024 if tokens % 1024 == 0 else tm
    bias_a, bias_c = _bias_tiles(rel_bias, seq)
    x2d = x.reshape(tokens, d)
    fn = final_norm_w.reshape(1, d)
    for i in range(depth):
        wa, wc, wb = _split_w_in(w_in[i])
        pa, pc, pb = _inproj(x2d, norm1_w[i].reshape(1, d), wa, wc, wb, tm)
        ya = _dilated(pa.reshape(batch, seq, -1), bias_a, batch, seq)
        w2p = jnp.pad(gla_w_a2[i], ((0, LANES - B_GATE_RANK), (0, 0)))
        yb = _gla(pb.reshape(batch, seq, -1), w2p, gla_b_a[i].reshape(1, B_QK),
                  gla_norm_w[i].reshape(1, B_WIDTH), batch, seq)
        yc = _moba(pc.reshape(batch, seq, -1), bias_c, batch, seq)
        x2d = _outffn(
            x2d, ya.reshape(tokens, -1), yb.reshape(tokens, -1), yc.reshape(tokens, -1), w_out[i].astype(BF16),
            norm2_w[i].reshape(1, d), w_ff1[i].astype(BF16), w_ff2[i].astype(BF16), fn,
            tm_ffn, final=(i == depth - 1))
    return x2d.reshape(batch, seq, d)
```

```python
import functools
import math

import numpy as np
import jax
import jax.numpy as jnp
from jax import lax
from jax.experimental import pallas as pl
from jax.experimental.pallas import tpu as pltpu

D_MODEL = 1024
HEAD_DIM = 64
A_HEADS = 6
A_BRANCHES = ((128, 1), (512, 4), (2048, 16))
B_HEADS = 4
B_KEY_DIM = 32
B_VAL_DIM = 64
B_GATE_RANK = 16
B_GATE_TAU = 16.0
C_HEADS = 6
C_BLOCK = 256
C_TOPK = 3
REL_BUCKETS = 32
REL_MAX_DIST = 2048
D_FF = 4 * D_MODEL
EPS = 1e-6
NEG_INF = -1e30
LOG2E = math.log2(math.e)

A_WIDTH = A_HEADS * HEAD_DIM
B_QK = B_HEADS * B_KEY_DIM
B_WIDTH = B_HEADS * B_VAL_DIM
C_WIDTH = C_HEADS * HEAD_DIM

LANES = 128
SUBLANES = 8
A_BLK = 128
PAIR = LANES // HEAD_DIM
VMEM_LIMIT = 56 * 1024 * 1024
F32 = jnp.float32
BF16 = jnp.bfloat16


def _bucket_thresholds():
    n = np.arange(0, REL_MAX_DIST + 1)
    exact = REL_BUCKETS // 2
    logv = (np.log(np.maximum(n, 1).astype(np.float32) / np.float32(exact))
            / np.float32(math.log(REL_MAX_DIST / exact))).astype(np.float32)
    large = np.minimum(exact + (logv * np.float32(REL_BUCKETS - exact)).astype(np.int32), REL_BUCKETS - 1)
    bucket = np.where(n < exact, n, large)
    assert np.all(np.diff(bucket) >= 0)
    return [int(np.argmax(bucket >= k)) for k in range(REL_BUCKETS)]


_THRESH = _bucket_thresholds()


def _nt_dot(a, b):
    return lax.dot_general(a, b, (((1,), (1,)), ((), ())), preferred_element_type=F32)


def _rms(x, w):
    return x * lax.rsqrt(jnp.mean(x * x, axis=-1, keepdims=True) + EPS) * w


def _bias_lookup(dist, rb_ref, col):
    val = jnp.full(dist.shape, rb_ref[0, col], F32)
    for k in range(1, REL_BUCKETS):
        val = jnp.where(dist >= _THRESH[k], rb_ref[k, col], val)
    return val


def _bias_kernel(rb_ref, ba_ref, bc_ref):
    h = pl.program_id(0)
    qi = lax.broadcasted_iota(jnp.int32, (A_BLK, 2 * A_BLK), 0)
    kj = lax.broadcasted_iota(jnp.int32, (A_BLK, 2 * A_BLK), 1)
    steps = qi + A_BLK - kj
    band = (steps >= 0) & (steps <= A_BLK)
    for g, (_, dil) in enumerate(A_BRANCHES):
        ba_ref[0, g] = jnp.where(band, _bias_lookup(steps * dil, rb_ref, h) * LOG2E, NEG_INF)
    qi = lax.broadcasted_iota(jnp.int32, (C_BLOCK, C_BLOCK), 0)
    kj = lax.broadcasted_iota(jnp.int32, (C_BLOCK, C_BLOCK), 1)
    nblk = bc_ref.shape[2] // C_BLOCK
    for delta in range(nblk):
        dist = delta * C_BLOCK + qi - kj
        bias = _bias_lookup(dist, rb_ref, A_HEADS + h) * LOG2E
        if delta == 0:
            bias = jnp.where(dist >= 0, bias, NEG_INF)
        bc_ref[0, :, (nblk - 1 - delta) * C_BLOCK:(nblk - delta) * C_BLOCK] = bias


def _bias_tiles(rel_bias, seq):
    return pl.pallas_call(
        _bias_kernel,
        grid=(A_HEADS,),
        in_specs=[pl.BlockSpec(memory_space=pltpu.SMEM)],
        out_specs=[
            pl.BlockSpec((1, len(A_BRANCHES), A_BLK, 2 * A_BLK), lambda h: (h // PAIR, 0, h % PAIR, 0)),
            pl.BlockSpec((1, C_BLOCK, seq), lambda h: (h // PAIR, h % PAIR, 0)),
        ],
        out_shape=[
            jax.ShapeDtypeStruct((A_HEADS // PAIR, len(A_BRANCHES), PAIR * A_BLK, 2 * A_BLK), F32),
            jax.ShapeDtypeStruct((C_HEADS // PAIR, PAIR * C_BLOCK, seq), F32),
        ],
        name="bias_tiles",
    )(rel_bias)


def _inproj_kernel(x_ref, nw_ref, wa_ref, wc_ref, wb_ref, pa_ref, pc_ref, pb_ref):
    h = _rms(x_ref[...], nw_ref[...]).astype(BF16)
    pa_ref[...] = jnp.dot(h, wa_ref[...], preferred_element_type=F32)
    pc_ref[...] = jnp.dot(h, wc_ref[...], preferred_element_type=F32)
    pb_ref[...] = jnp.dot(h, wb_ref[...], preferred_element_type=F32)


def _inproj(x2d, nw, wa, wc, wb, tm):
    m = x2d.shape[0]
    const = lambda i: (0, 0)
    row = lambda i: (i, 0)
    return pl.pallas_call(
        _inproj_kernel,
        grid=(m // tm,),
        in_specs=[
            pl.BlockSpec((tm, D_MODEL), row),
            pl.BlockSpec((1, D_MODEL), const),
            pl.BlockSpec(wa.shape, const, pipeline_mode=pl.Buffered(1)),
            pl.BlockSpec(wc.shape, const, pipeline_mode=pl.Buffered(1)),
            pl.BlockSpec(wb.shape, const, pipeline_mode=pl.Buffered(1)),
        ],
        out_specs=[
            pl.BlockSpec((tm, wa.shape[1]), row),
            pl.BlockSpec((tm, wc.shape[1]), row),
            pl.BlockSpec((tm, wb.shape[1]), row),
        ],
        out_shape=[
            jax.ShapeDtypeStruct((m, wa.shape[1]), F32),
            jax.ShapeDtypeStruct((m, wc.shape[1]), F32),
            jax.ShapeDtypeStruct((m, wb.shape[1]), F32),
        ],
        compiler_params=pltpu.CompilerParams(vmem_limit_bytes=VMEM_LIMIT),
        name="inproj",
    )(x2d, nw, wa, wc, wb)


A_QSCALE = HEAD_DIM ** -0.5 * LOG2E
A_AHEAD = 5


def _dilated_kernel(q_ref, k_ref, v_ref, bias_ref, o_ref, qs_s, ks_s, vs_s, x_s, m_s, l_s, acc_s):
    seq = q_ref.shape[1]
    (w1, d1), (w2, d2), (w3, d3) = A_BRANCHES
    assert d1 == 1 and d3 == d2 * d2 and all(w // d == A_BLK for w, d in A_BRANCHES)
    coarse = seq // d2
    lane = lax.broadcasted_iota(jnp.int32, (A_BLK, LANES), 1)
    head0 = lane < HEAD_DIM
    vs_s[:, :, LANES:] = jnp.ones((len(A_BRANCHES), seq, LANES), BF16)

    def put(g, dst, q, k, v):
        lane_l = lax.broadcasted_iota(jnp.int32, q.shape, 1)
        qs_s[g, 0, dst, :] = jnp.where(lane_l < HEAD_DIM, q, 0.0).astype(BF16)
        qs_s[g, 1, dst, :] = jnp.where(lane_l < HEAD_DIM, 0.0, q).astype(BF16)
        ks_s[g, dst, :] = k.astype(BF16)
        vs_s[g, dst, :LANES] = v.astype(BF16)

    def logits(g, bi, first):
        qrows = pl.ds(bi * A_BLK, A_BLK)
        q2 = jnp.concatenate([qs_s[g, 0, qrows, :], qs_s[g, 1, qrows, :]], axis=0)
        if first:
            krows = qrows
            bias = bias_ref[0, g, :, A_BLK:]
        else:
            krows = pl.ds((bi - 1) * A_BLK, 2 * A_BLK)
            bias = bias_ref[0, g]
        return _nt_dot(q2, ks_s[g, krows, :]) + bias, krows

    def finish(g, s, krows, out_rows):
        m = jnp.max(s, axis=-1, keepdims=True)
        p = jnp.exp2(s - m).astype(BF16)
        r = jnp.dot(p, vs_s[g, krows, :], preferred_element_type=F32)
        m_s[g, out_rows, :] = jnp.where(head0, m[:A_BLK], m[A_BLK:])
        l_s[g, out_rows, :] = jnp.where(head0, r[:A_BLK, LANES:], r[A_BLK:, LANES:])
        acc_s[g, out_rows, :] = jnp.where(head0, r[:A_BLK, :LANES], r[A_BLK:, :LANES])

    def run(blocks):
        pending = []
        for g, bi, first, out_rows in blocks:
            pending.append((g, logits(g, bi, first), out_rows))
            if len(pending) > A_AHEAD:
                g0, (s, krows), rows = pending.pop(0)
                finish(g0, s, krows, rows)
        for g0, (s, krows), rows in pending:
            finish(g0, s, krows, rows)

    put(0, pl.ds(0, seq), q_ref[0] * A_QSCALE, k_ref[0], v_ref[0])
    for r in range(d2):
        src = pl.ds(r, coarse, stride=d2)
        dst = pl.ds(r * coarse, coarse)
        q, k, v = q_ref[0, src, :] * A_QSCALE, k_ref[0, src, :], v_ref[0, src, :]
        x_s[0, dst, :], x_s[1, dst, :], x_s[2, dst, :] = q, k, v
        put(1, dst, q, k, v)
    fine = seq // d3
    for r in range(d3):
        src = pl.ds((r % d2) * coarse + r // d2, fine, stride=d2)
        put(2, pl.ds(r * fine, fine), x_s[0, src, :], x_s[1, src, :], x_s[2, src, :])

    nb1, nb2 = coarse // A_BLK, fine // A_BLK
    run([(0, bi, bi == 0, pl.ds(bi * A_BLK, A_BLK)) for bi in range(seq // A_BLK)]
        + [(1, bi, bi % nb1 == 0, pl.ds(bi * A_BLK, A_BLK)) for bi in range(d2 * nb1)]
        + [(2, r * nb2 + n, n == 0, pl.ds((r % d2) * coarse + r // d2 + n * A_BLK * d2, A_BLK, stride=d2))
           for r in range(d3) for n in range(nb2)])

    for r in range(d2):
        rows = pl.ds(r * coarse, coarse)
        tok = pl.ds(r, coarse, stride=d2)
        m1, m2, m3 = m_s[0, tok, :], m_s[1, rows, :], m_s[2, rows, :]
        m = jnp.maximum(jnp.maximum(m1, m2), m3)
        e1, e2, e3 = jnp.exp2(m1 - m), jnp.exp2(m2 - m), jnp.exp2(m3 - m)
        num = e1 * acc_s[0, tok, :] + e2 * acc_s[1, rows, :] + e3 * acc_s[2, rows, :]
        den = e1 * l_s[0, tok, :] + e2 * l_s[1, rows, :] + e3 * l_s[2, rows, :]
        o_ref[0, tok, :] = num / den


def _dilated(pa, bias_a, batch, seq):
    npair = A_HEADS // PAIR
    nbr = len(A_BRANCHES)
    blk = lambda off: pl.BlockSpec((1, seq, LANES), lambda p, b: (b, 0, off + p))
    return pl.pallas_call(
        _dilated_kernel,
        grid=(npair, batch),
        in_specs=[
            blk(0), blk(npair), blk(2 * npair),
            pl.BlockSpec((1, nbr, PAIR * A_BLK, 2 * A_BLK), lambda p, b: (p, 0, 0, 0)),
        ],
        out_specs=pl.BlockSpec((1, seq, LANES), lambda p, b: (b, 0, p)),
        out_shape=jax.ShapeDtypeStruct((batch, seq, A_WIDTH), F32),
        scratch_shapes=[
            pltpu.VMEM((nbr, PAIR, seq, LANES), BF16),
            pltpu.VMEM((nbr, seq, LANES), BF16),
            pltpu.VMEM((nbr, seq, 2 * LANES), BF16),
            pltpu.VMEM((3, seq, LANES), F32),
            pltpu.VMEM((nbr, seq, LANES), F32),
            pltpu.VMEM((nbr, seq, LANES), F32),
            pltpu.VMEM((nbr, seq, LANES), F32),
        ],
        compiler_params=pltpu.CompilerParams(vmem_limit_bytes=VMEM_LIMIT),
        name="dilated_attn",
    )(pa, pa, pa, bias_a)


GLA_TILE = 128
GLA_LEVELS = GLA_TILE.bit_length() - 1
GLA_GROUP = 16
PB_Q, PB_K, PB_V, PB_R, PB_A = 0, B_QK, 2 * B_QK, 2 * B_QK + B_WIDTH, 2 * B_QK + 2 * B_WIDTH
PB_WIDTH = PB_A + LANES


def _log_sigmoid(z):
    return jnp.minimum(z, 0.0) - jnp.log1p(jnp.exp(-jnp.abs(z)))


def _gla_constants():
    t = GLA_TILE
    i = np.arange(t)[:, None]
    c = np.arange(t)[None, :]
    blocks = [(c <= i).astype(np.float32)]
    for lvl in range(1, GLA_LEVELS + 1):
        w = t >> lvl
        split = (i // (2 * w)) * (2 * w) + w - 1
        blocks.append((c <= i).astype(np.float32) - (c <= split).astype(np.float32))
    x = np.arange(t)[:, None] ^ np.arange(t)[None, :]
    level_of_pair = np.where(np.arange(t)[None, :] < np.arange(t)[:, None],
                             np.floor(np.log2(np.maximum(x, 1))).astype(np.int32), -1)
    return np.concatenate(blocks, axis=0), np.tile(level_of_pair, (1, B_HEADS)).astype(np.int32)


def _gla_kernel(pb_ref, w2_ref, ba_ref, nw_ref, g_ref, lvl_ref, y_ref):
    seq = pb_ref.shape[1]
    t = GLA_TILE
    kl = lax.broadcasted_iota(jnp.int32, (B_QK, B_WIDTH), 0) // B_KEY_DIM
    vl = lax.broadcasted_iota(jnp.int32, (B_QK, B_WIDTH), 1) // B_VAL_DIM
    expand = jnp.where(kl == vl, 1.0, 0.0).astype(BF16)
    vlt = lax.broadcasted_iota(jnp.int32, (B_WIDTH, B_QK), 0) // B_VAL_DIM
    klt = lax.broadcasted_iota(jnp.int32, (B_WIDTH, B_QK), 1) // B_KEY_DIM
    head_t = vlt == klt
    va = (lax.broadcasted_iota(jnp.int32, (2 * B_WIDTH, B_WIDTH), 0) % B_WIDTH) // B_VAL_DIM
    vb = lax.broadcasted_iota(jnp.int32, (2 * B_WIDTH, B_WIDTH), 1) // B_VAL_DIM
    head_mean2 = jnp.where(va == vb, 1.0 / B_VAL_DIM, 0.0).astype(BF16)
    row = lax.broadcasted_iota(jnp.int32, (t, B_QK), 0)
    rep_head_k = lax.broadcasted_iota(jnp.int32, (B_HEADS * t, B_QK), 0) // t
    own_key_lanes = rep_head_k == lax.broadcasted_iota(jnp.int32, (B_HEADS * t, B_QK), 1) // B_KEY_DIM
    rep_head_v = lax.broadcasted_iota(jnp.int32, (B_HEADS * t, B_WIDTH), 0) // t
    own_val_lanes = rep_head_v == lax.broadcasted_iota(jnp.int32, (B_HEADS * t, B_WIDTH), 1) // B_VAL_DIM
    zero_k4 = jnp.zeros((B_HEADS * t, B_QK), BF16)
    zero_v4 = jnp.zeros((B_HEADS * t, B_WIDTH), BF16)

    def log_decay(ti):
        rows = pl.ds(pl.multiple_of(ti * t, t), t)
        a = pb_ref[0, rows, PB_A:PB_A + LANES]
        a_hi = a.astype(BF16)
        a_lo = (a - a_hi.astype(F32)).astype(BF16)
        z = jnp.dot(jnp.concatenate([a_hi, a_hi, a_lo], axis=1), w2_ref[...], preferred_element_type=F32)
        return _log_sigmoid(z + ba_ref[...]) * (LOG2E / B_GATE_TAU)

    def decays(log_a):
        a1 = log_a.astype(BF16)
        a2 = (log_a - a1.astype(F32)).astype(BF16)
        dd = jnp.dot(g_ref[...], jnp.concatenate([a1, a2], axis=1), preferred_element_type=F32)
        return dd[:, :B_QK] + dd[:, B_QK:]

    def pairs(ti, dist):
        rows = pl.ds(pl.multiple_of(ti * t, t), t)
        q = pb_ref[0, rows, PB_Q:PB_Q + B_QK] * (B_KEY_DIM ** -0.5)
        k = pb_ref[0, rows, PB_K:PB_K + B_QK]
        attn = jnp.zeros((t, B_HEADS * t), F32)
        for lvl in range(1, GLA_LEVELS + 1):
            w = t >> lvl
            decay = jnp.exp2(-jnp.abs(dist[lvl * t:(lvl + 1) * t]))
            after = (row & w) != 0
            q_l = (jnp.where(after, q * decay, 0.0)).astype(BF16)
            k_l = jnp.where(after, 0.0, k * decay).astype(BF16)
            k4 = jnp.where(own_key_lanes, jnp.concatenate([k_l] * B_HEADS, axis=0), zero_k4)
            attn = jnp.where(lvl_ref[...] == GLA_LEVELS - lvl, _nt_dot(q_l, k4), attn)
        return attn.astype(BF16), q, k

    def within(ti, dist, attn, q, k):
        rows = pl.ds(pl.multiple_of(ti * t, t), t)
        b = dist[:t]
        b_last = b[t - 1:t, :]
        v = pb_ref[0, rows, PB_V:PB_V + B_WIDTH]
        v_bf = v.astype(BF16)
        v4 = jnp.where(own_val_lanes, jnp.concatenate([v_bf] * B_HEADS, axis=0), zero_v4)
        o = jnp.dot(attn, v4, preferred_element_type=F32)
        o = o + jnp.dot((q * k).astype(BF16), expand, preferred_element_type=F32) * v
        q_dec = (q * jnp.exp2(b)).astype(BF16)
        k_dec = (k * jnp.exp2(b_last - b)).astype(BF16)
        u_t = jnp.dot(v.T.astype(BF16), k_dec, preferred_element_type=F32)
        return o, q_dec, jnp.where(head_t, u_t, 0.0), jnp.exp2(b_last)

    def finish(ti, o):
        rows = pl.ds(pl.multiple_of(ti * t, t), t)
        sq = o * o
        sq_hi = sq.astype(BF16)
        sq_lo = (sq - sq_hi.astype(F32)).astype(BF16)
        ms = jnp.dot(jnp.concatenate([sq_hi, sq_lo], axis=1), head_mean2, preferred_element_type=F32)
        r = pb_ref[0, rows, PB_R:PB_R + B_WIDTH]
        silu = r / (1.0 + jnp.exp(-r))
        y_ref[0, rows, :] = o * lax.rsqrt(ms + EPS) * nw_ref[...] * silu

    def group(gi, state):
        tiles = [gi * GLA_GROUP + j for j in range(GLA_GROUP)]
        dists = [decays(log_a) for log_a in [log_decay(ti) for ti in tiles]]
        weights = [pairs(ti, dist) for ti, dist in zip(tiles, dists)]
        parts = [within(ti, dist, *w) for ti, dist, w in zip(tiles, dists, weights)]
        outs = []
        for o, q_dec, u_t, gate in parts:
            outs.append(o + _nt_dot(q_dec, state.astype(BF16)))
            state = state * gate + u_t
        for ti, o in zip(tiles, outs):
            finish(ti, o)
        return state

    lax.fori_loop(0, seq // (t * GLA_GROUP), group, jnp.zeros((B_WIDTH, B_QK), F32))


def _gla(pb, w2p, b_a, norm_w, batch, seq):
    const = lambda b: (0, 0)
    g_all, lvl = _gla_constants()
    g_all = jnp.asarray(g_all, BF16)
    lvl = jnp.asarray(lvl)
    w_hi = w2p.astype(BF16)
    w_lo = (w2p - w_hi.astype(F32)).astype(BF16)
    w2p = jnp.concatenate([w_hi, w_lo, w_hi], axis=0)
    return pl.pallas_call(
        _gla_kernel,
        grid=(batch,),
        in_specs=[
            pl.BlockSpec((1, seq, PB_WIDTH), lambda b: (b, 0, 0)),
            pl.BlockSpec(w2p.shape, const),
            pl.BlockSpec((1, B_QK), const),
            pl.BlockSpec((1, B_WIDTH), const),
            pl.BlockSpec(g_all.shape, const),
            pl.BlockSpec(lvl.shape, const),
        ],
        out_specs=pl.BlockSpec((1, seq, B_WIDTH), lambda b: (b, 0, 0)),
        out_shape=jax.ShapeDtypeStruct((batch, seq, B_WIDTH), F32),
        compiler_params=pltpu.CompilerParams(vmem_limit_bytes=VMEM_LIMIT),
        name="gla",
    )(pb, w2p, b_a, norm_w, g_all, lvl)


C_AHEAD = 2


def _moba_kernel(q_ref, k_ref, v_ref, bias_ref, o_ref, ka_s, qa_s, va_s):
    seq = q_ref.shape[1]
    nblk = seq // C_BLOCK
    head0 = lax.broadcasted_iota(jnp.int32, (C_BLOCK, LANES), 1) < HEAD_DIM

    k = k_ref[0]
    q = q_ref[0]
    ka_s[:, :LANES] = k.astype(BF16)
    row_blk = lax.broadcasted_iota(jnp.int32, (seq, LANES), 0) // C_BLOCK
    lane_id = lax.broadcasted_iota(jnp.int32, (seq, LANES), 1)
    ka_s[:, LANES:] = jnp.where(lane_id == row_blk, 1.0, 0.0).astype(BF16)
    va_s[:, :LANES] = v_ref[0].astype(BF16)
    va_s[:, LANES:] = jnp.ones((seq, LANES), BF16)
    k_mean = jnp.mean(k.reshape(nblk, C_BLOCK, LANES), axis=1)
    q_scaled = q * (HEAD_DIM ** -0.5 * LOG2E)
    blk = lax.broadcasted_iota(jnp.int32, (nblk, seq), 0)
    q_blk = lax.broadcasted_iota(jnp.int32, (nblk, seq), 1) // C_BLOCK
    for hh in range(PAIR):
        qa_s[hh, :, :LANES] = jnp.where((lane_id < HEAD_DIM) == (hh == 0), q_scaled, 0.0).astype(BF16)
        gate = _nt_dot(k_mean.astype(BF16), qa_s[hh, :, :LANES])
        rank = jnp.zeros((nblk, seq), jnp.int32)
        for m in range(nblk):
            gm = gate[m:m + 1, :]
            beats = (gm > gate) | ((gm == gate) & (m < blk))
            rank = rank + jnp.where(beats & (m < q_blk), 1, 0)
        dropped = (blk < q_blk) & (rank >= C_TOPK)
        pen = jnp.where(dropped, NEG_INF, 0.0)
        assert nblk == SUBLANES
        pen_t = jnp.concatenate([pen, jnp.zeros((LANES - nblk, seq), F32)], axis=0).T
        qa_s[hh, :, LANES:] = pen_t.astype(BF16)

    def logits(qb):
        qrows = slice(qb * C_BLOCK, (qb + 1) * C_BLOCK)
        kdim = slice(0, LANES) if qb <= C_TOPK else slice(0, 2 * LANES)
        q2 = jnp.concatenate([qa_s[0, qrows, kdim], qa_s[1, qrows, kdim]], axis=0)
        return _nt_dot(q2, ka_s[:(qb + 1) * C_BLOCK, kdim]) + bias_ref[0, :, (nblk - 1 - qb) * C_BLOCK:]

    def finish(qb, s):
        m = jnp.max(s, axis=-1, keepdims=True)
        p = jnp.exp2(s - m).astype(BF16)
        r = jnp.dot(p, va_s[:(qb + 1) * C_BLOCK, :], preferred_element_type=F32)
        o_ref[0, qb * C_BLOCK:(qb + 1) * C_BLOCK, :] = jnp.where(
            head0, r[:C_BLOCK, :LANES] / r[:C_BLOCK, LANES:], r[C_BLOCK:, :LANES] / r[C_BLOCK:, LANES:])

    early = [qb for qb in reversed(range(nblk)) if qb <= C_TOPK]
    late = [qb for qb in reversed(range(nblk)) if qb > C_TOPK]
    order = [qb for pair in zip(early, late) for qb in pair] + early[len(late):] + late[len(early):]
    pending = []
    for qb in order:
        pending.append((qb, logits(qb)))
        if len(pending) > C_AHEAD:
            finish(*pending.pop(0))
    for item in pending:
        finish(*item)


def _moba(pc, bias_c, batch, seq):
    npair = C_HEADS // PAIR
    blk = lambda off: pl.BlockSpec((1, seq, LANES), lambda p, b: (b, 0, off + p))
    return pl.pallas_call(
        _moba_kernel,
        grid=(npair, batch),
        in_specs=[
            blk(0), blk(npair), blk(2 * npair),
            pl.BlockSpec((1, PAIR * C_BLOCK, seq), lambda p, b: (p, 0, 0)),
        ],
        out_specs=pl.BlockSpec((1, seq, LANES), lambda p, b: (b, 0, p)),
        out_shape=jax.ShapeDtypeStruct((batch, seq, C_WIDTH), F32),
        scratch_shapes=[
            pltpu.VMEM((seq, 2 * LANES), BF16),
            pltpu.VMEM((PAIR, seq, 2 * LANES), BF16),
            pltpu.VMEM((seq, 2 * LANES), BF16),
        ],
        compiler_params=pltpu.CompilerParams(vmem_limit_bytes=VMEM_LIMIT),
        name="moba_attn",
    )(pc, pc, pc, bias_c)


FF_CHUNK = 1024


def _outffn_kernel(x_ref, ya_ref, yb_ref, yc_ref, wo_ref, n2_ref, w1_ref, w2_ref, fn_ref, o_ref, *, final):
    mix = jnp.concatenate([ya_ref[...].astype(BF16), yb_ref[...].astype(BF16), yc_ref[...].astype(BF16)], axis=1)
    x = x_ref[...] + jnp.dot(mix, wo_ref[...], preferred_element_type=F32)
    h = _rms(x, n2_ref[...]).astype(BF16)
    ffn = None
    for j in range(D_FF // FF_CHUNK):
        cols = slice(j * FF_CHUNK, (j + 1) * FF_CHUNK)
        a = jnp.dot(h, w1_ref[:, cols], preferred_element_type=F32)
        a = jnp.square(jnp.maximum(a, 0.0)).astype(BF16)
        part = jnp.dot(a, w2_ref[cols, :], preferred_element_type=F32)
        ffn = part if ffn is None else ffn + part
    x = x + ffn
    if final:
        x = _rms(x, fn_ref[...])
    o_ref[...] = x


def _outffn(x2d, ya, yb, yc, wo, n2, w1, w2, fn, tm, final):
    m = x2d.shape[0]
    const = lambda i: (0, 0)
    row = lambda i: (i, 0)
    full = lambda a: pl.BlockSpec(a.shape, const, pipeline_mode=pl.Buffered(1))
    return pl.pallas_call(
        functools.partial(_outffn_kernel, final=final),
        grid=(m // tm,),
        in_specs=[
            pl.BlockSpec((tm, D_MODEL), row),
            pl.BlockSpec((tm, ya.shape[1]), row),
            pl.BlockSpec((tm, yb.shape[1]), row),
            pl.BlockSpec((tm, yc.shape[1]), row),
            full(wo), full(n2), full(w1), full(w2), full(fn),
        ],
        out_specs=pl.BlockSpec((tm, D_MODEL), row),
        out_shape=jax.ShapeDtypeStruct((m, D_MODEL), F32),
        compiler_params=pltpu.CompilerParams(vmem_limit_bytes=VMEM_LIMIT),
        name="outproj_ffn",
    )(x2d, ya, yb, yc, wo, n2, w1, w2, fn)


def _split_w_in(w):
    sizes = (A_WIDTH, A_WIDTH, A_WIDTH, B_QK, B_QK, B_WIDTH, B_WIDTH, B_GATE_RANK, C_WIDTH, C_WIDTH, C_WIDTH)
    offs = np.concatenate([[0], np.cumsum(sizes)])
    aq, ak, av, bq, bk, bv, br, ba, cq, ck, cv = [w[:, offs[i]:offs[i + 1]] for i in range(len(sizes))]
    ba = jnp.pad(ba, ((0, 0), (0, LANES - B_GATE_RANK)))
    cat = lambda parts: jnp.concatenate(parts, axis=1).astype(BF16)
    return cat([aq, ak, av]), cat([cq, ck, cv]), cat([bq, bk, bv, br, ba])


def kernel(x, norm1_w, w_in, gla_w_a2, gla_b_a, gla_norm_w, w_out, norm2_w, w_ff1, w_ff2, rel_bias, final_norm_w):
    batch, seq, d = x.shape
    depth = w_in.shape[0]
    assert d == D_MODEL and seq % C_BLOCK == 0 and seq % (A_BLK * A_BRANCHES[-1][1]) == 0
    tokens = batch * seq
    tm = 512 if tokens % 512 == 0 else seq
    tm_ffn = 1024 if tokens % 1024 == 0 else tm
    bias_a, bias_c = _bias_tiles(rel_bias, seq)
    x2d = x.reshape(tokens, d)
    fn = final_norm_w.reshape(1, d)
    for i in range(depth):
        wa, wc, wb = _split_w_in(w_in[i])
        pa, pc, pb = _inproj(x2d, norm1_w[i].reshape(1, d), wa, wc, wb, tm_ffn)
        ya = _dilated(pa.reshape(batch, seq, -1), bias_a, batch, seq)
        w2p = jnp.pad(gla_w_a2[i], ((0, LANES - B_GATE_RANK), (0, 0)))
        yb = _gla(pb.reshape(batch, seq, -1), w2p, gla_b_a[i].reshape(1, B_QK),
                  gla_norm_w[i].reshape(1, B_WIDTH), batch, seq)
        yc = _moba(pc.reshape(batch, seq, -1), bias_c, batch, seq)
        x2d = _outffn(
            x2d, ya.reshape(tokens, -1), yb.reshape(tokens, -1), yc.reshape(tokens, -1), w_out[i].astype(BF16),
            norm2_w[i].reshape(1, d), w_ff1[i].astype(BF16), w_ff2[i].astype(BF16), fn,
            tm_ffn, final=(i == depth - 1))
    return x2d.reshape(batch, seq, d)
```

```python
import functools
import math

import numpy as np
import jax
import jax.numpy as jnp
from jax import lax
from jax.experimental import pallas as pl
from jax.experimental.pallas import tpu as pltpu

D_MODEL = 1024
HEAD_DIM = 64
A_HEADS = 6
A_BRANCHES = ((128, 1), (512, 4), (2048, 16))
B_HEADS = 4
B_KEY_DIM = 32
B_VAL_DIM = 64
B_GATE_RANK = 16
B_GATE_TAU = 16.0
C_HEADS = 6
C_BLOCK = 256
C_TOPK = 3
REL_BUCKETS = 32
REL_MAX_DIST = 2048
D_FF = 4 * D_MODEL
EPS = 1e-6
NEG_INF = -1e30
LOG2E = math.log2(math.e)

A_WIDTH = A_HEADS * HEAD_DIM
B_QK = B_HEADS * B_KEY_DIM
B_WIDTH = B_HEADS * B_VAL_DIM
C_WIDTH = C_HEADS * HEAD_DIM

LANES = 128
SUBLANES = 8
A_BLK = 128
PAIR = LANES // HEAD_DIM
VMEM_LIMIT = 56 * 1024 * 1024
F32 = jnp.float32
BF16 = jnp.bfloat16


def _bucket_thresholds():
    n = np.arange(0, REL_MAX_DIST + 1)
    exact = REL_BUCKETS // 2
    logv = (np.log(np.maximum(n, 1).astype(np.float32) / np.float32(exact))
            / np.float32(math.log(REL_MAX_DIST / exact))).astype(np.float32)
    large = np.minimum(exact + (logv * np.float32(REL_BUCKETS - exact)).astype(np.int32), REL_BUCKETS - 1)
    bucket = np.where(n < exact, n, large)
    assert np.all(np.diff(bucket) >= 0)
    return [int(np.argmax(bucket >= k)) for k in range(REL_BUCKETS)]


_THRESH = _bucket_thresholds()


def _nt_dot(a, b):
    return lax.dot_general(a, b, (((1,), (1,)), ((), ())), preferred_element_type=F32)


def _rms(x, w):
    return x * lax.rsqrt(jnp.mean(x * x, axis=-1, keepdims=True) + EPS) * w


def _bias_lookup(dist, rb_ref, col):
    val = jnp.full(dist.shape, rb_ref[0, col], F32)
    for k in range(1, REL_BUCKETS):
        val = jnp.where(dist >= _THRESH[k], rb_ref[k, col], val)
    return val


def _bias_kernel(rb_ref, ba_ref, bc_ref):
    h = pl.program_id(0)
    qi = lax.broadcasted_iota(jnp.int32, (A_BLK, 2 * A_BLK), 0)
    kj = lax.broadcasted_iota(jnp.int32, (A_BLK, 2 * A_BLK), 1)
    steps = qi + A_BLK - kj
    band = (steps >= 0) & (steps <= A_BLK)
    for g, (_, dil) in enumerate(A_BRANCHES):
        ba_ref[0, g] = jnp.where(band, _bias_lookup(steps * dil, rb_ref, h) * LOG2E, NEG_INF)
    qi = lax.broadcasted_iota(jnp.int32, (C_BLOCK, C_BLOCK), 0)
    kj = lax.broadcasted_iota(jnp.int32, (C_BLOCK, C_BLOCK), 1)
    nblk = bc_ref.shape[2] // C_BLOCK
    for delta in range(nblk):
        dist = delta * C_BLOCK + qi - kj
        bias = _bias_lookup(dist, rb_ref, A_HEADS + h) * LOG2E
        if delta == 0:
            bias = jnp.where(dist >= 0, bias, NEG_INF)
        bc_ref[0, :, (nblk - 1 - delta) * C_BLOCK:(nblk - delta) * C_BLOCK] = bias


def _bias_tiles(rel_bias, seq):
    return pl.pallas_call(
        _bias_kernel,
        grid=(A_HEADS,),
        in_specs=[pl.BlockSpec(memory_space=pltpu.SMEM)],
        out_specs=[
            pl.BlockSpec((1, len(A_BRANCHES), A_BLK, 2 * A_BLK), lambda h: (h // PAIR, 0, h % PAIR, 0)),
            pl.BlockSpec((1, C_BLOCK, seq), lambda h: (h // PAIR, h % PAIR, 0)),
        ],
        out_shape=[
            jax.ShapeDtypeStruct((A_HEADS // PAIR, len(A_BRANCHES), PAIR * A_BLK, 2 * A_BLK), F32),
            jax.ShapeDtypeStruct((C_HEADS // PAIR, PAIR * C_BLOCK, seq), F32),
        ],
        name="bias_tiles",
    )(rel_bias)


C_QSCALE = HEAD_DIM ** -0.5 * LOG2E


def _inproj_kernel(x_ref, nw_ref, wa_ref, wc_ref, wb_ref, pa_ref, pc_ref, pb_ref):
    h = _rms(x_ref[...], nw_ref[...]).astype(BF16)
    pa_ref[...] = jnp.dot(h, wa_ref[...], preferred_element_type=F32)
    pc = jnp.dot(h, wc_ref[...], preferred_element_type=F32)
    is_q = lax.broadcasted_iota(jnp.int32, pc.shape, 1) < C_WIDTH
    pc_ref[...] = jnp.where(is_q, pc * C_QSCALE, pc).astype(BF16)
    pb_ref[...] = jnp.dot(h, wb_ref[...], preferred_element_type=F32)


def _inproj(x2d, nw, wa, wc, wb, tm):
    m = x2d.shape[0]
    const = lambda i: (0, 0)
    row = lambda i: (i, 0)
    return pl.pallas_call(
        _inproj_kernel,
        grid=(m // tm,),
        in_specs=[
            pl.BlockSpec((tm, D_MODEL), row),
            pl.BlockSpec((1, D_MODEL), const),
            pl.BlockSpec(wa.shape, const, pipeline_mode=pl.Buffered(1)),
            pl.BlockSpec(wc.shape, const, pipeline_mode=pl.Buffered(1)),
            pl.BlockSpec(wb.shape, const, pipeline_mode=pl.Buffered(1)),
        ],
        out_specs=[
            pl.BlockSpec((tm, wa.shape[1]), row),
            pl.BlockSpec((tm, wc.shape[1]), row),
            pl.BlockSpec((tm, wb.shape[1]), row),
        ],
        out_shape=[
            jax.ShapeDtypeStruct((m, wa.shape[1]), F32),
            jax.ShapeDtypeStruct((m, wc.shape[1]), BF16),
            jax.ShapeDtypeStruct((m, wb.shape[1]), F32),
        ],
        compiler_params=pltpu.CompilerParams(vmem_limit_bytes=VMEM_LIMIT),
        name="inproj",
    )(x2d, nw, wa, wc, wb)


A_QSCALE = HEAD_DIM ** -0.5 * LOG2E
A_AHEAD = 5


def _dilated_kernel(q_ref, k_ref, v_ref, bias_ref, o_ref, qs_s, ks_s, vs_s, x_s, m_s, l_s, acc_s):
    seq = q_ref.shape[1]
    (w1, d1), (w2, d2), (w3, d3) = A_BRANCHES
    assert d1 == 1 and d3 == d2 * d2 and all(w // d == A_BLK for w, d in A_BRANCHES)
    coarse = seq // d2
    lane = lax.broadcasted_iota(jnp.int32, (A_BLK, LANES), 1)
    head0 = lane < HEAD_DIM
    vs_s[:, :, LANES:] = jnp.ones((len(A_BRANCHES), seq, LANES), BF16)

    def put(g, dst, q, k, v):
        lane_l = lax.broadcasted_iota(jnp.int32, q.shape, 1)
        qs_s[g, 0, dst, :] = jnp.where(lane_l < HEAD_DIM, q, 0.0).astype(BF16)
        qs_s[g, 1, dst, :] = jnp.where(lane_l < HEAD_DIM, 0.0, q).astype(BF16)
        ks_s[g, dst, :] = k.astype(BF16)
        vs_s[g, dst, :LANES] = v.astype(BF16)

    def logits(g, bi, first):
        qrows = pl.ds(bi * A_BLK, A_BLK)
        q2 = jnp.concatenate([qs_s[g, 0, qrows, :], qs_s[g, 1, qrows, :]], axis=0)
        if first:
            krows = qrows
            bias = bias_ref[0, g, :, A_BLK:]
        else:
            krows = pl.ds((bi - 1) * A_BLK, 2 * A_BLK)
            bias = bias_ref[0, g]
        return _nt_dot(q2, ks_s[g, krows, :]) + bias, krows

    def finish(g, s, krows, out_rows):
        m = jnp.max(s, axis=-1, keepdims=True)
        p = jnp.exp2(s - m).astype(BF16)
        r = jnp.dot(p, vs_s[g, krows, :], preferred_element_type=F32)
        m_s[g, out_rows, :] = jnp.where(head0, m[:A_BLK], m[A_BLK:])
        l_s[g, out_rows, :] = jnp.where(head0, r[:A_BLK, LANES:], r[A_BLK:, LANES:])
        acc_s[g, out_rows, :] = jnp.where(head0, r[:A_BLK, :LANES], r[A_BLK:, :LANES])

    def run(blocks):
        pending = []
        for g, bi, first, out_rows in blocks:
            pending.append((g, logits(g, bi, first), out_rows))
            if len(pending) > A_AHEAD:
                g0, (s, krows), rows = pending.pop(0)
                finish(g0, s, krows, rows)
        for g0, (s, krows), rows in pending:
            finish(g0, s, krows, rows)

    put(0, pl.ds(0, seq), q_ref[0] * A_QSCALE, k_ref[0], v_ref[0])
    for r in range(d2):
        src = pl.ds(r, coarse, stride=d2)
        dst = pl.ds(r * coarse, coarse)
        q, k, v = q_ref[0, src, :] * A_QSCALE, k_ref[0, src, :], v_ref[0, src, :]
        x_s[0, dst, :], x_s[1, dst, :], x_s[2, dst, :] = q, k, v
        put(1, dst, q, k, v)
    fine = seq // d3
    for r in range(d3):
        src = pl.ds((r % d2) * coarse + r // d2, fine, stride=d2)
        put(2, pl.ds(r * fine, fine), x_s[0, src, :], x_s[1, src, :], x_s[2, src, :])

    nb1, nb2 = coarse // A_BLK, fine // A_BLK
    run([(0, bi, bi == 0, pl.ds(bi * A_BLK, A_BLK)) for bi in range(seq // A_BLK)]
        + [(1, bi, bi % nb1 == 0, pl.ds(bi * A_BLK, A_BLK)) for bi in range(d2 * nb1)]
        + [(2, r * nb2 + n, n == 0, pl.ds((r % d2) * coarse + r // d2 + n * A_BLK * d2, A_BLK, stride=d2))
           for r in range(d3) for n in range(nb2)])

    for r in range(d2):
        rows = pl.ds(r * coarse, coarse)
        tok = pl.ds(r, coarse, stride=d2)
        m1, m2, m3 = m_s[0, tok, :], m_s[1, rows, :], m_s[2, rows, :]
        m = jnp.maximum(jnp.maximum(m1, m2), m3)
        e1, e2, e3 = jnp.exp2(m1 - m), jnp.exp2(m2 - m), jnp.exp2(m3 - m)
        num = e1 * acc_s[0, tok, :] + e2 * acc_s[1, rows, :] + e3 * acc_s[2, rows, :]
        den = e1 * l_s[0, tok, :] + e2 * l_s[1, rows, :] + e3 * l_s[2, rows, :]
        o_ref[0, tok, :] = num / den


def _dilated(pa, bias_a, batch, seq):
    npair = A_HEADS // PAIR
    nbr = len(A_BRANCHES)
    blk = lambda off: pl.BlockSpec((1, seq, LANES), lambda p, b: (b, 0, off + p))
    return pl.pallas_call(
        _dilated_kernel,
        grid=(npair, batch),
        in_specs=[
            blk(0), blk(npair), blk(2 * npair),
            pl.BlockSpec((1, nbr, PAIR * A_BLK, 2 * A_BLK), lambda p, b: (p, 0, 0, 0)),
        ],
        out_specs=pl.BlockSpec((1, seq, LANES), lambda p, b: (b, 0, p)),
        out_shape=jax.ShapeDtypeStruct((batch, seq, A_WIDTH), F32),
        scratch_shapes=[
            pltpu.VMEM((nbr, PAIR, seq, LANES), BF16),
            pltpu.VMEM((nbr, seq, LANES), BF16),
            pltpu.VMEM((nbr, seq, 2 * LANES), BF16),
            pltpu.VMEM((3, seq, LANES), F32),
            pltpu.VMEM((nbr, seq, LANES), F32),
            pltpu.VMEM((nbr, seq, LANES), F32),
            pltpu.VMEM((nbr, seq, LANES), F32),
        ],
        compiler_params=pltpu.CompilerParams(vmem_limit_bytes=VMEM_LIMIT),
        name="dilated_attn",
    )(pa, pa, pa, bias_a)


GLA_TILE = 128
GLA_LEVELS = GLA_TILE.bit_length() - 1
GLA_GROUP = 16
PB_Q, PB_K, PB_V, PB_R, PB_A = 0, B_QK, 2 * B_QK, 2 * B_QK + B_WIDTH, 2 * B_QK + 2 * B_WIDTH
PB_WIDTH = PB_A + LANES


def _log_sigmoid(z):
    return jnp.minimum(z, 0.0) - jnp.log1p(jnp.exp(-jnp.abs(z)))


def _gla_constants():
    t = GLA_TILE
    i = np.arange(t)[:, None]
    c = np.arange(t)[None, :]
    blocks = [(c <= i).astype(np.float32)]
    for lvl in range(1, GLA_LEVELS + 1):
        w = t >> lvl
        split = (i // (2 * w)) * (2 * w) + w - 1
        blocks.append((c <= i).astype(np.float32) - (c <= split).astype(np.float32))
    x = np.arange(t)[:, None] ^ np.arange(t)[None, :]
    level_of_pair = np.where(np.arange(t)[None, :] < np.arange(t)[:, None],
                             np.floor(np.log2(np.maximum(x, 1))).astype(np.int32), -1)
    return np.concatenate(blocks, axis=0), np.tile(level_of_pair, (1, B_HEADS)).astype(np.int32)


def _gla_kernel(pb_ref, w2_ref, ba_ref, nw_ref, g_ref, lvl_ref, y_ref):
    seq = pb_ref.shape[1]
    t = GLA_TILE
    kl = lax.broadcasted_iota(jnp.int32, (B_QK, B_WIDTH), 0) // B_KEY_DIM
    vl = lax.broadcasted_iota(jnp.int32, (B_QK, B_WIDTH), 1) // B_VAL_DIM
    expand = jnp.where(kl == vl, 1.0, 0.0).astype(BF16)
    vlt = lax.broadcasted_iota(jnp.int32, (B_WIDTH, B_QK), 0) // B_VAL_DIM
    klt = lax.broadcasted_iota(jnp.int32, (B_WIDTH, B_QK), 1) // B_KEY_DIM
    head_t = vlt == klt
    va = (lax.broadcasted_iota(jnp.int32, (2 * B_WIDTH, B_WIDTH), 0) % B_WIDTH) // B_VAL_DIM
    vb = lax.broadcasted_iota(jnp.int32, (2 * B_WIDTH, B_WIDTH), 1) // B_VAL_DIM
    head_mean2 = jnp.where(va == vb, 1.0 / B_VAL_DIM, 0.0).astype(BF16)
    row = lax.broadcasted_iota(jnp.int32, (t, B_QK), 0)
    rep_head_k = lax.broadcasted_iota(jnp.int32, (B_HEADS * t, B_QK), 0) // t
    own_key_lanes = rep_head_k == lax.broadcasted_iota(jnp.int32, (B_HEADS * t, B_QK), 1) // B_KEY_DIM
    rep_head_v = lax.broadcasted_iota(jnp.int32, (B_HEADS * t, B_WIDTH), 0) // t
    own_val_lanes = rep_head_v == lax.broadcasted_iota(jnp.int32, (B_HEADS * t, B_WIDTH), 1) // B_VAL_DIM
    zero_k4 = jnp.zeros((B_HEADS * t, B_QK), BF16)
    zero_v4 = jnp.zeros((B_HEADS * t, B_WIDTH), BF16)

    def log_decay(ti):
        rows = pl.ds(pl.multiple_of(ti * t, t), t)
        a = pb_ref[0, rows, PB_A:PB_A + LANES]
        a_hi = a.astype(BF16)
        a_lo = (a - a_hi.astype(F32)).astype(BF16)
        z = jnp.dot(jnp.concatenate([a_hi, a_hi, a_lo], axis=1), w2_ref[...], preferred_element_type=F32)
        return _log_sigmoid(z + ba_ref[...]) * (LOG2E / B_GATE_TAU)

    def decays(log_a):
        a1 = log_a.astype(BF16)
        a2 = (log_a - a1.astype(F32)).astype(BF16)
        dd = jnp.dot(g_ref[...], jnp.concatenate([a1, a2], axis=1), preferred_element_type=F32)
        return dd[:, :B_QK] + dd[:, B_QK:]

    def pairs(ti, dist):
        rows = pl.ds(pl.multiple_of(ti * t, t), t)
        q = pb_ref[0, rows, PB_Q:PB_Q + B_QK] * (B_KEY_DIM ** -0.5)
        k = pb_ref[0, rows, PB_K:PB_K + B_QK]
        attn = jnp.zeros((t, B_HEADS * t), F32)
        for lvl in range(1, GLA_LEVELS + 1):
            w = t >> lvl
            decay = jnp.exp2(-jnp.abs(dist[lvl * t:(lvl + 1) * t]))
            after = (row & w) != 0
            q_l = (jnp.where(after, q * decay, 0.0)).astype(BF16)
            k_l = jnp.where(after, 0.0, k * decay).astype(BF16)
            k4 = jnp.where(own_key_lanes, jnp.concatenate([k_l] * B_HEADS, axis=0), zero_k4)
            attn = jnp.where(lvl_ref[...] == GLA_LEVELS - lvl, _nt_dot(q_l, k4), attn)
        return attn.astype(BF16), q, k

    def within(ti, dist, attn, q, k):
        rows = pl.ds(pl.multiple_of(ti * t, t), t)
        b = dist[:t]
        b_last = b[t - 1:t, :]
        v = pb_ref[0, rows, PB_V:PB_V + B_WIDTH]
        v_bf = v.astype(BF16)
        v4 = jnp.where(own_val_lanes, jnp.concatenate([v_bf] * B_HEADS, axis=0), zero_v4)
        o = jnp.dot(attn, v4, preferred_element_type=F32)
        o = o + jnp.dot((q * k).astype(BF16), expand, preferred_element_type=F32) * v
        q_dec = (q * jnp.exp2(b)).astype(BF16)
        k_dec = (k * jnp.exp2(b_last - b)).astype(BF16)
        u_t = jnp.dot(v.T.astype(BF16), k_dec, preferred_element_type=F32)
        return o, q_dec, jnp.where(head_t, u_t, 0.0), jnp.exp2(b_last)

    def finish(ti, o):
        rows = pl.ds(pl.multiple_of(ti * t, t), t)
        sq = o * o
        sq_hi = sq.astype(BF16)
        sq_lo = (sq - sq_hi.astype(F32)).astype(BF16)
        ms = jnp.dot(jnp.concatenate([sq_hi, sq_lo], axis=1), head_mean2, preferred_element_type=F32)
        r = pb_ref[0, rows, PB_R:PB_R + B_WIDTH]
        silu = r / (1.0 + jnp.exp(-r))
        y_ref[0, rows, :] = o * lax.rsqrt(ms + EPS) * nw_ref[...] * silu

    def group(gi, state):
        tiles = [gi * GLA_GROUP + j for j in range(GLA_GROUP)]
        dists = [decays(log_a) for log_a in [log_decay(ti) for ti in tiles]]
        weights = [pairs(ti, dist) for ti, dist in zip(tiles, dists)]
        parts = [within(ti, dist, *w) for ti, dist, w in zip(tiles, dists, weights)]
        outs = []
        for o, q_dec, u_t, gate in parts:
            outs.append(o + _nt_dot(q_dec, state.astype(BF16)))
            state = state * gate + u_t
        for ti, o in zip(tiles, outs):
            finish(ti, o)
        return state

    lax.fori_loop(0, seq // (t * GLA_GROUP), group, jnp.zeros((B_WIDTH, B_QK), F32))


def _gla(pb, w2p, b_a, norm_w, batch, seq):
    const = lambda b: (0, 0)
    g_all, lvl = _gla_constants()
    g_all = jnp.asarray(g_all, BF16)
    lvl = jnp.asarray(lvl)
    w_hi = w2p.astype(BF16)
    w_lo = (w2p - w_hi.astype(F32)).astype(BF16)
    w2p = jnp.concatenate([w_hi, w_lo, w_hi], axis=0)
    return pl.pallas_call(
        _gla_kernel,
        grid=(batch,),
        in_specs=[
            pl.BlockSpec((1, seq, PB_WIDTH), lambda b: (b, 0, 0)),
            pl.BlockSpec(w2p.shape, const),
            pl.BlockSpec((1, B_QK), const),
            pl.BlockSpec((1, B_WIDTH), const),
            pl.BlockSpec(g_all.shape, const),
            pl.BlockSpec(lvl.shape, const),
        ],
        out_specs=pl.BlockSpec((1, seq, B_WIDTH), lambda b: (b, 0, 0)),
        out_shape=jax.ShapeDtypeStruct((batch, seq, B_WIDTH), F32),
        compiler_params=pltpu.CompilerParams(vmem_limit_bytes=VMEM_LIMIT),
        name="gla",
    )(pb, w2p, b_a, norm_w, g_all, lvl)


C_AHEAD = 2


def _moba_kernel(q_ref, k_ref, v_ref, bias_ref, o_ref, ka_s, qa_s, va_s):
    seq = q_ref.shape[1]
    nblk = seq // C_BLOCK
    head0 = lax.broadcasted_iota(jnp.int32, (C_BLOCK, LANES), 1) < HEAD_DIM

    k = k_ref[0]
    q = q_ref[0]
    ka_s[:, :LANES] = k
    row_blk = lax.broadcasted_iota(jnp.int32, (seq, LANES), 0) // C_BLOCK
    lane_id = lax.broadcasted_iota(jnp.int32, (seq, LANES), 1)
    ka_s[:, LANES:] = jnp.where(lane_id == row_blk, 1.0, 0.0).astype(BF16)
    va_s[:, :LANES] = v_ref[0]
    va_s[:, LANES:] = jnp.ones((seq, LANES), BF16)
    k_mean = jnp.mean(k.astype(F32).reshape(nblk, C_BLOCK, LANES), axis=1)
    blk = lax.broadcasted_iota(jnp.int32, (nblk, seq), 0)
    q_blk = lax.broadcasted_iota(jnp.int32, (nblk, seq), 1) // C_BLOCK
    for hh in range(PAIR):
        qa_s[hh, :, :LANES] = jnp.where((lane_id < HEAD_DIM) == (hh == 0), q, jnp.zeros_like(q))
        gate = _nt_dot(k_mean.astype(BF16), qa_s[hh, :, :LANES])
        rank = jnp.zeros((nblk, seq), jnp.int32)
        for m in range(nblk):
            gm = gate[m:m + 1, :]
            beats = (gm > gate) | ((gm == gate) & (m < blk))
            rank = rank + jnp.where(beats & (m < q_blk), 1, 0)
        dropped = (blk < q_blk) & (rank >= C_TOPK)
        pen = jnp.where(dropped, NEG_INF, 0.0)
        assert nblk == SUBLANES
        pen_t = jnp.concatenate([pen, jnp.zeros((LANES - nblk, seq), F32)], axis=0).T
        qa_s[hh, :, LANES:] = pen_t.astype(BF16)

    def logits(qb):
        qrows = slice(qb * C_BLOCK, (qb + 1) * C_BLOCK)
        kdim = slice(0, LANES) if qb <= C_TOPK else slice(0, 2 * LANES)
        q2 = jnp.concatenate([qa_s[0, qrows, kdim], qa_s[1, qrows, kdim]], axis=0)
        return _nt_dot(q2, ka_s[:(qb + 1) * C_BLOCK, kdim]) + bias_ref[0, :, (nblk - 1 - qb) * C_BLOCK:]

    def finish(qb, s):
        m = jnp.max(s, axis=-1, keepdims=True)
        p = jnp.exp2(s - m).astype(BF16)
        r = jnp.dot(p, va_s[:(qb + 1) * C_BLOCK, :], preferred_element_type=F32)
        o_ref[0, qb * C_BLOCK:(qb + 1) * C_BLOCK, :] = jnp.where(
            head0, r[:C_BLOCK, :LANES] / r[:C_BLOCK, LANES:], r[C_BLOCK:, :LANES] / r[C_BLOCK:, LANES:])

    early = [qb for qb in reversed(range(nblk)) if qb <= C_TOPK]
    late = [qb for qb in reversed(range(nblk)) if qb > C_TOPK]
    order = [qb for pair in zip(early, late) for qb in pair] + early[len(late):] + late[len(early):]
    pending = []
    for qb in order:
        pending.append((qb, logits(qb)))
        if len(pending) > C_AHEAD:
            finish(*pending.pop(0))
    for item in pending:
        finish(*item)


def _moba(pc, bias_c, batch, seq):
    npair = C_HEADS // PAIR
    blk = lambda off: pl.BlockSpec((1, seq, LANES), lambda p, b: (b, 0, off + p))
    return pl.pallas_call(
        _moba_kernel,
        grid=(npair, batch),
        in_specs=[
            blk(0), blk(npair), blk(2 * npair),
            pl.BlockSpec((1, PAIR * C_BLOCK, seq), lambda p, b: (p, 0, 0)),
        ],
        out_specs=pl.BlockSpec((1, seq, LANES), lambda p, b: (b, 0, p)),
        out_shape=jax.ShapeDtypeStruct((batch, seq, C_WIDTH), F32),
        scratch_shapes=[
            pltpu.VMEM((seq, 2 * LANES), BF16),
            pltpu.VMEM((PAIR, seq, 2 * LANES), BF16),
            pltpu.VMEM((seq, 2 * LANES), BF16),
        ],
        compiler_params=pltpu.CompilerParams(vmem_limit_bytes=VMEM_LIMIT),
        name="moba_attn",
    )(pc, pc, pc, bias_c)


FF_CHUNK = 1024


def _outffn_kernel(x_ref, ya_ref, yb_ref, yc_ref, wo_ref, n2_ref, w1_ref, w2_ref, fn_ref, o_ref, *, final):
    mix = jnp.concatenate([ya_ref[...].astype(BF16), yb_ref[...].astype(BF16), yc_ref[...].astype(BF16)], axis=1)
    x = x_ref[...] + jnp.dot(mix, wo_ref[...], preferred_element_type=F32)
    h = _rms(x, n2_ref[...]).astype(BF16)
    ffn = None
    for j in range(D_FF // FF_CHUNK):
        cols = slice(j * FF_CHUNK, (j + 1) * FF_CHUNK)
        a = jnp.dot(h, w1_ref[:, cols], preferred_element_type=F32)
        a = jnp.square(jnp.maximum(a, 0.0)).astype(BF16)
        part = jnp.dot(a, w2_ref[cols, :], preferred_element_type=F32)
        ffn = part if ffn is None else ffn + part
    x = x + ffn
    if final:
        x = _rms(x, fn_ref[...])
    o_ref[...] = x


def _outffn(x2d, ya, yb, yc, wo, n2, w1, w2, fn, tm, final):
    m = x2d.shape[0]
    const = lambda i: (0, 0)
    row = lambda i: (i, 0)
    full = lambda a: pl.BlockSpec(a.shape, const, pipeline_mode=pl.Buffered(1))
    return pl.pallas_call(
        functools.partial(_outffn_kernel, final=final),
        grid=(m // tm,),
        in_specs=[
            pl.BlockSpec((tm, D_MODEL), row),
            pl.BlockSpec((tm, ya.shape[1]), row),
            pl.BlockSpec((tm, yb.shape[1]), row),
            pl.BlockSpec((tm, yc.shape[1]), row),
            full(wo), full(n2), full(w1), full(w2), full(fn),
        ],
        out_specs=pl.BlockSpec((tm, D_MODEL), row),
        out_shape=jax.ShapeDtypeStruct((m, D_MODEL), F32),
        compiler_params=pltpu.CompilerParams(vmem_limit_bytes=VMEM_LIMIT),
        name="outproj_ffn",
    )(x2d, ya, yb, yc, wo, n2, w1, w2, fn)


def _split_w_in(w):
    sizes = (A_WIDTH, A_WIDTH, A_WIDTH, B_QK, B_QK, B_WIDTH, B_WIDTH, B_GATE_RANK, C_WIDTH, C_WIDTH, C_WIDTH)
    offs = np.concatenate([[0], np.cumsum(sizes)])
    aq, ak, av, bq, bk, bv, br, ba, cq, ck, cv = [w[:, offs[i]:offs[i + 1]] for i in range(len(sizes))]
    ba = jnp.pad(ba, ((0, 0), (0, LANES - B_GATE_RANK)))
    cat = lambda parts: jnp.concatenate(parts, axis=1).astype(BF16)
    return cat([aq, ak, av]), cat([cq, ck, cv]), cat([bq, bk, bv, br, ba])


def kernel(x, norm1_w, w_in, gla_w_a2, gla_b_a, gla_norm_w, w_out, norm2_w, w_ff1, w_ff2, rel_bias, final_norm_w):
    batch, seq, d = x.shape
    depth = w_in.shape[0]
    assert d == D_MODEL and seq % C_BLOCK == 0 and seq % (A_BLK * A_BRANCHES[-1][1]) == 0
    tokens = batch * seq
    tm = 512 if tokens % 512 == 0 else seq
    tm_ffn = 1024 if tokens % 1024 == 0 else tm
    bias_a, bias_c = _bias_tiles(rel_bias, seq)
    x2d = x.reshape(tokens, d)
    fn = final_norm_w.reshape(1, d)
    for i in range(depth):
        wa, wc, wb = _split_w_in(w_in[i])
        pa, pc, pb = _inproj(x2d, norm1_w[i].reshape(1, d), wa, wc, wb, tm_ffn)
        ya = _dilated(pa.reshape(batch, seq, -1), bias_a, batch, seq)
        w2p = jnp.pad(gla_w_a2[i], ((0, LANES - B_GATE_RANK), (0, 0)))
        yb = _gla(pb.reshape(batch, seq, -1), w2p, gla_b_a[i].reshape(1, B_QK),
                  gla_norm_w[i].reshape(1, B_WIDTH), batch, seq)
        yc = _moba(pc.reshape(batch, seq, -1), bias_c, batch, seq)
        x2d = _outffn(
            x2d, ya.reshape(tokens, -1), yb.reshape(tokens, -1), yc.reshape(tokens, -1), w_out[i].astype(BF16),
            norm2_w[i].reshape(1, d), w_ff1[i].astype(BF16), w_ff2[i].astype(BF16), fn,
            tm_ffn, final=(i == depth - 1))
    return x2d.reshape(batch, seq, d)
```
